```python
import jax, jax.numpy as jnp
from jax import lax
import numpy as np

D_MODEL = 1024
BATCH = 4
SEQ = 8192
DEPTH = 4

HEAD_DIM = 64
HEADS_MLA = 4
HEADS_DIL = 8
HEADS_DSA = 4
D_MIX = (HEADS_MLA + HEADS_DIL + HEADS_DSA) * HEAD_DIM
Q_LORA = 256
KV_LORA = 128
MLA_NOPE = 64
MLA_ROPE = 32
MLA_V = HEAD_DIM
DIL_PAIRS = ((128, 1), (512, 4), (2048, 16))
IDX_HEADS = 8
IDX_DIM = 32
DSA_TOPK = 256
D_FF = 2816
CONV_WIDTH = 3
ROPE_THETA = 500000.0
ROT_FRAC = 4
BLOCK = 128
NORM_EPS = 1e-6
DIL_PAD = 16 * BLOCK
IN_SPLITS = (Q_LORA, KV_LORA, MLA_ROPE,
             3 * HEADS_DIL * HEAD_DIM,
             3 * HEADS_DSA * HEAD_DIM,
             IDX_HEADS * IDX_DIM, IDX_DIM, IDX_HEADS)
N_IN = sum(IN_SPLITS)

kernel_name = "hybrid_mla_dilated_dsa_convffn"


def rms_norm(x, g):
    xf = x.astype(jnp.float32)
    y = xf * lax.rsqrt(jnp.mean(xf * xf, axis=-1, keepdims=True) + NORM_EPS)
    return (y * g.astype(jnp.float32)).astype(x.dtype)


def rope_tables(seq, dim):
    inv = jnp.power(jnp.float32(ROPE_THETA), -jnp.arange(0, dim, 2, dtype=jnp.float32) / dim)
    ang = jnp.arange(seq, dtype=jnp.float32)[:, None] * inv[None, :]
    return jnp.cos(ang), jnp.sin(ang)


def apply_rope(x, cos, sin):
    h = x.shape[-1] // 2
    xf = x.astype(jnp.float32)
    x1, x2 = xf[..., :h], xf[..., h:]
    c, s = cos[None, :, None, :], sin[None, :, None, :]
    return jnp.concatenate([x1 * c - x2 * s, x2 * c + x1 * s], axis=-1).astype(x.dtype)


def partial_rope(x, cos, sin):
    r = 2 * cos.shape[-1]
    return jnp.concatenate([apply_rope(x[..., :r], cos, sin), x[..., r:]], axis=-1)


def mla_attention(c_q, c_kv, k_pe_raw, g_q, w_uq, g_kv, w_ukv, cos_a, sin_a):
    B, S, _ = c_q.shape
    q = (rms_norm(c_q, g_q) @ w_uq).reshape(B, S, HEADS_MLA, MLA_NOPE + MLA_ROPE)
    q_nope = q[..., :MLA_NOPE]
    q_pe = apply_rope(q[..., MLA_NOPE:], cos_a, sin_a)
    kv = (rms_norm(c_kv, g_kv) @ w_ukv).reshape(B, S, HEADS_MLA, MLA_NOPE + MLA_V)
    k_nope, v = kv[..., :MLA_NOPE], kv[..., MLA_NOPE:]
    k_pe = apply_rope(k_pe_raw[:, :, None, :], cos_a, sin_a)[:, :, 0]
    scale = (MLA_NOPE + MLA_ROPE) ** -0.5
    nb = S // BLOCK
    qn_b = q_nope.reshape(B, nb, BLOCK, HEADS_MLA, MLA_NOPE).transpose(1, 0, 2, 3, 4)
    qp_b = q_pe.reshape(B, nb, BLOCK, HEADS_MLA, MLA_ROPE).transpose(1, 0, 2, 3, 4)
    kpos = jnp.arange(S)

    def block(args):
        qn, qp, i = args
        s = (jnp.einsum('bqhe,bshe->bhqs', qn, k_nope).astype(jnp.float32)
             + jnp.einsum('bqhr,bsr->bhqs', qp, k_pe).astype(jnp.float32)) * scale
        qpos = i * BLOCK + jnp.arange(BLOCK)
        s = jnp.where((kpos[None, :] <= qpos[:, None])[None, None], s, -jnp.inf)
        p = jax.nn.softmax(s, axis=-1).astype(v.dtype)
        return jnp.einsum('bhqs,bshe->bqhe', p, v)

    o = lax.map(block, (qn_b, qp_b, jnp.arange(nb)))
    return o.transpose(1, 0, 2, 3, 4).reshape(B, S, HEADS_MLA, MLA_V)


def dilated_window(q, k, v, dilation, n_win):
    B, Sp, H, Dh = q.shape
    n = Sp // dilation
    nb = n // BLOCK

    def stride_split(t):
        return t.reshape(B, n, dilation, H, Dh).transpose(0, 2, 1, 3, 4)

    qs, ks, vs = stride_split(q), stride_split(k), stride_split(v)
    qb = qs.reshape(B, dilation, nb, BLOCK, H, Dh)

    def band(t):
        tp = jnp.pad(t, ((0, 0), (0, 0), (BLOCK, 0), (0, 0), (0, 0)))
        prev = tp[:, :, :n].reshape(B, dilation, nb, BLOCK, H, Dh)
        return jnp.concatenate([prev, t.reshape(B, dilation, nb, BLOCK, H, Dh)], axis=3)

    kb, vb = band(ks), band(vs)
    s = jnp.einsum('bdnqhe,bdnkhe->bdnhqk', qb, kb).astype(jnp.float32) * (Dh ** -0.5)
    qi = jnp.arange(BLOCK)[None, :, None]
    kj = jnp.arange(2 * BLOCK)[None, None, :]
    blk = jnp.arange(nb)[:, None, None]
    dist = BLOCK + qi - kj
    valid = (dist >= 0) & (dist <= n_win) & ((blk > 0) | (kj >= BLOCK))
    s = jnp.where(valid[None, None, :, None], s, -jnp.inf)
    m = jnp.max(s, axis=-1, keepdims=True)
    p = jnp.exp(s - m)
    l = jnp.sum(p, axis=-1, keepdims=True)
    o = jnp.einsum('bdnhqk,bdnkhe->bdnqhe', p / l, vb.astype(jnp.float32))
    o = o.reshape(B, dilation, n, H, Dh).transpose(0, 2, 1, 3, 4).reshape(B, Sp, H, Dh)

    def stat_back(t):
        t = t[..., 0].transpose(0, 1, 2, 4, 3).reshape(B, dilation, n, H)
        return t.transpose(0, 2, 1, 3).reshape(B, Sp, H)

    return o, stat_back(m), stat_back(l)


def dilated_attention(q, k, v):
    B, S, H, Dh = q.shape
    Sp = -(-S // DIL_PAD) * DIL_PAD
    pad = ((0, 0), (0, Sp - S), (0, 0), (0, 0))
    qp, kp, vp = jnp.pad(q, pad), jnp.pad(k, pad), jnp.pad(v, pad)
    os_, ms, ls = [], [], []
    for window, dil in DIL_PAIRS:
        o, m, l = dilated_window(qp, kp, vp, dil, window // dil)
        os_.append(o); ms.append(m); ls.append(l)
    m_all = jnp.stack(ms)
    a = jnp.stack(ls) * jnp.exp(m_all - jnp.max(m_all, axis=0))
    o = jnp.einsum('pbsh,pbshe->bshe', a, jnp.stack(os_)) / jnp.sum(a, axis=0)[..., None]
    return o[:, :S].astype(q.dtype)


def dsa_attention(q, k, v, q_idx, k_idx, w_idx):
    B, S, H, Dh = q.shape
    k_sel = min(DSA_TOPK, S // 4)
    nb = S // BLOCK
    idx_scale = (IDX_HEADS * IDX_DIM) ** -0.5
    kpos = jnp.arange(S)

    def to_blocks(t):
        return t.reshape((B, nb, BLOCK) + t.shape[2:]).swapaxes(0, 1)

    def block(args):
        qb, qib, wib, i = args
        qpos = i * BLOCK + jnp.arange(BLOCK)
        causal = kpos[None, :] <= qpos[:, None]
        dots = jnp.einsum('bqhd,bsd->bqhs', qib, k_idx).astype(jnp.float32)
        score = jnp.einsum('bqhs,bqh->bqs', jax.nn.relu(dots), wib.astype(jnp.float32) * idx_scale)
        score = jnp.where(causal[None], score, -jnp.inf)
        _, sel = lax.top_k(score, k_sel)
        flat = sel.reshape(B, BLOCK * k_sel)
        kg = jax.vmap(lambda a, j: a[j])(k, flat).reshape(B, BLOCK, k_sel, H, Dh)
        vg = jax.vmap(lambda a, j: a[j])(v, flat).reshape(B, BLOCK, k_sel, H, Dh)
        s = jnp.einsum('bqhe,bqkhe->bhqk', qb, kg).astype(jnp.float32) * (Dh ** -0.5)
        ok = sel <= qpos[None, :, None]
        s = jnp.where(ok[:, None], s, -jnp.inf)
        p = jax.nn.softmax(s, axis=-1).astype(v.dtype)
        return jnp.einsum('bhqk,bqkhe->bqhe', p, vg)

    o = lax.map(block, (to_blocks(q), to_blocks(q_idx), to_blocks(w_idx), jnp.arange(nb)))
    return o.swapaxes(0, 1).reshape(B, S, H, Dh)


def causal_dwconv(u, w, b):
    K = w.shape[0]
    S = u.shape[1]
    up = jnp.pad(u, ((0, 0), (K - 1, 0), (0, 0)))
    y = up[:, 0:S] * w[0]
    for j in range(1, K):
        y = y + up[:, j:j + S] * w[j]
    return y + b


def setup_inputs(seed: int = 0) -> dict:
    key = jax.random.key(seed)
    ks = jax.random.split(key, 16)
    f32 = jnp.float32

    def nrm(k, shape, scale):
        return jax.random.normal(k, shape, f32) * scale

    def gain(k, shape):
        return 1.0 + 0.02 * jax.random.normal(k, shape, f32)

    L = DEPTH
    return {
        "x": nrm(ks[0], (BATCH, SEQ, D_MODEL), 1.0),
        "g_attn": gain(ks[1], (L, D_MODEL)),
        "w_in": nrm(ks[2], (L, D_MODEL, N_IN), D_MODEL ** -0.5),
        "g_q_lat": gain(ks[3], (L, Q_LORA)),
        "w_uq": nrm(ks[4], (L, Q_LORA, HEADS_MLA * (MLA_NOPE + MLA_ROPE)), Q_LORA ** -0.5),
        "g_kv_lat": gain(ks[5], (L, KV_LORA)),
        "w_ukv": nrm(ks[6], (L, KV_LORA, HEADS_MLA * (MLA_NOPE + MLA_V)), KV_LORA ** -0.5),
        "w_o": nrm(ks[7], (L, D_MIX, D_MODEL), D_MIX ** -0.5),
        "g_ffn": gain(ks[8], (L, D_MODEL)),
        "w_up": nrm(ks[9], (L, D_MODEL, 2 * D_FF), D_MODEL ** -0.5),
        "conv_w": nrm(ks[10], (L, CONV_WIDTH, 2 * D_FF), CONV_WIDTH ** -0.5),
        "conv_b": nrm(ks[11], (L, 2 * D_FF), 0.01),
        "w_down": nrm(ks[12], (L, D_FF, D_MODEL), D_FF ** -0.5),
        "g_final": gain(ks[13], (D_MODEL,)),
    }


def reference(x, g_attn, w_in, g_q_lat, w_uq, g_kv_lat, w_ukv, w_o,
              g_ffn, w_up, conv_w, conv_b, w_down, g_final):
    B, S, _ = x.shape
    cos_a, sin_a = rope_tables(S, MLA_ROPE)
    cos_h, sin_h = rope_tables(S, HEAD_DIM // ROT_FRAC)
    cos_i, sin_i = rope_tables(S, IDX_DIM // ROT_FRAC)
    offsets = np.cumsum(IN_SPLITS)[:-1].tolist()
    for l in range(DEPTH):
        h = rms_norm(x, g_attn[l])
        z = h @ w_in[l]
        c_q, c_kv, k_pe, z_dil, z_dsa, z_qi, z_ki, z_wi = jnp.split(z, offsets, axis=-1)

        o_a = mla_attention(c_q, c_kv, k_pe, g_q_lat[l], w_uq[l], g_kv_lat[l], w_ukv[l], cos_a, sin_a)

        qkv_b = z_dil.reshape(B, S, 3, HEADS_DIL, HEAD_DIM)
        o_b = dilated_attention(partial_rope(qkv_b[:, :, 0], cos_h, sin_h),
                                partial_rope(qkv_b[:, :, 1], cos_h, sin_h),
                                qkv_b[:, :, 2])

        qkv_c = z_dsa.reshape(B, S, 3, HEADS_DSA, HEAD_DIM)
        q_idx = partial_rope(z_qi.reshape(B, S, IDX_HEADS, IDX_DIM), cos_i, sin_i)
        k_idx = partial_rope(z_ki[:, :, None, :], cos_i, sin_i)[:, :, 0]
        o_c = dsa_attention(partial_rope(qkv_c[:, :, 0], cos_h, sin_h),
                            partial_rope(qkv_c[:, :, 1], cos_h, sin_h),
                            qkv_c[:, :, 2], q_idx, k_idx, z_wi)

        mix = jnp.concatenate([o_a.reshape(B, S, -1), o_b.reshape(B, S, -1),
                               o_c.reshape(B, S, -1)], axis=-1)
        x = x + mix @ w_o[l]

        h = rms_norm(x, g_ffn[l])
        u = causal_dwconv(h @ w_up[l], conv_w[l], conv_b[l])
        gate, up = u[..., :D_FF], u[..., D_FF:]
        x = x + (jax.nn.silu(gate) * up) @ w_down[l]
    return rms_norm(x, g_final)
```

```python
import functools
import math

import jax
import jax.numpy as jnp
from jax import lax
from jax.experimental import pallas as pl
from jax.experimental.pallas import tpu as pltpu

D_MODEL = 1024
HEAD_DIM = 64
HEADS_MLA = 4
HEADS_DIL = 8
HEADS_DSA = 4
Q_LORA = 256
KV_LORA = 128
MLA_NOPE = 64
MLA_ROPE = 32
DIL_PAIRS = ((128, 1), (512, 4), (2048, 16))
IDX_HEADS = 8
IDX_DIM = 32
DSA_TOPK = 256
D_FF = 2816
CONV_WIDTH = 3
ROPE_THETA = 500000.0
ROT_FRAC = 4
NORM_EPS = 1e-6

LANES = 128
BF16_SUBLANES = 16
VMEM_LIMIT = 56 * 1024 * 1024
NEG = -1e30
F32 = jnp.float32
BF16 = jnp.bfloat16

TQ = 256
TM_ROW = 256
TM_FFN = 512
TN_FFN = 256
DIL_BAND = max(w for w, _ in DIL_PAIRS)

C_CQ = 0
C_CKV = C_CQ + Q_LORA
C_KPE = C_CKV + KV_LORA
C_DIL = C_KPE + LANES
C_DSA = C_DIL + 3 * HEADS_DIL * HEAD_DIM
C_QI = C_DSA + 3 * HEADS_DSA * HEAD_DIM
C_KI = C_QI + IDX_HEADS * IDX_DIM
C_WI = C_KI + IDX_HEADS * IDX_DIM
N_PROJ = C_WI + LANES


def _cparams(sem):
    return pltpu.CompilerParams(dimension_semantics=sem, vmem_limit_bytes=VMEM_LIMIT)


def _rms(x, g):
    return x * lax.rsqrt(jnp.mean(x * x, axis=-1, keepdims=True) + NORM_EPS) * g


def _dot(a, b):
    return jnp.dot(a, b, preferred_element_type=F32)


def _dot_nt(a, b):
    return lax.dot_general(a, b, (((1,), (1,)), ((), ())), preferred_element_type=F32)


def _rope(x, tab_ref, t0, half):
    c = tab_ref[t0]
    sn = tab_ref[t0 + 1]
    sp = tab_ref[t0 + 2]
    return (x * c + pltpu.roll(x, LANES - half, axis=1) * sn
            + pltpu.roll(x, half, axis=1) * sp)


def _proj_kernel(x_ref, g_ref, w1_ref, gq_ref, wuq_ref, gkv_ref, wukv_ref, tab_ref,
                 mq_ref, mk_ref, mv_ref, dq_ref, dk_ref, dv_ref,
                 sq_ref, sk_ref, sv_ref, iq_ref, ik_ref, iw_ref):
    h = _rms(x_ref[...], g_ref[...]).astype(BF16)

    def proj(lo, width):
        return _dot(h, w1_ref[:, lo:lo + width])

    mla_scale = (MLA_NOPE + MLA_ROPE) ** -0.5
    cq = _rms(proj(C_CQ, Q_LORA), gq_ref[...]).astype(BF16)
    for hd in range(HEADS_MLA):
        base = 2 * LANES * hd
        qn = _dot(cq, wuq_ref[:, base:base + LANES])
        qp = _rope(_dot(cq, wuq_ref[:, base + LANES:base + 2 * LANES]), tab_ref, 0, MLA_ROPE // 2)
        mq_ref[:, base:base + LANES] = (qn * mla_scale).astype(BF16)
        mq_ref[:, base + LANES:base + 2 * LANES] = (qp * mla_scale).astype(BF16)
    ckv = _rms(proj(C_CKV, KV_LORA), gkv_ref[...]).astype(BF16)
    kpe = _rope(proj(C_KPE, LANES), tab_ref, 0, MLA_ROPE // 2).astype(BF16)
    for pr in range(HEADS_MLA // 2):
        mk_ref[:, 2 * LANES * pr:2 * LANES * pr + LANES] = _dot(
            ckv, wukv_ref[:, LANES * pr:LANES * (pr + 1)]).astype(BF16)
        mk_ref[:, 2 * LANES * pr + LANES:2 * LANES * (pr + 1)] = kpe
    mv_ref[...] = _dot(ckv, wukv_ref[:, HEADS_MLA * MLA_NOPE:]).astype(BF16)

    head_scale = HEAD_DIM ** -0.5
    half_h = HEAD_DIM // ROT_FRAC // 2

    def qkv(c0, nheads, q_ref, k_ref, v_ref):
        width = nheads * HEAD_DIM
        for gidx in range(width // LANES):
            off = gidx * LANES
            q = _rope(proj(c0 + off, LANES), tab_ref, 3, half_h)
            q_ref[:, off:off + LANES] = (q * head_scale).astype(BF16)
            k = _rope(proj(c0 + width + off, LANES), tab_ref, 3, half_h)
            k_ref[:, off:off + LANES] = k.astype(BF16)
        v_ref[...] = proj(c0 + 2 * width, width).astype(BF16)

    qkv(C_DIL, HEADS_DIL, dq_ref, dk_ref, dv_ref)
    qkv(C_DSA, HEADS_DSA, sq_ref, sk_ref, sv_ref)

    half_i = IDX_DIM // ROT_FRAC // 2
    for gidx in range(IDX_HEADS * IDX_DIM // LANES):
        off = gidx * LANES
        iq_ref[:, off:off + LANES] = _rope(proj(C_QI + off, LANES), tab_ref, 6, half_i).astype(BF16)
        ik_ref[:, off:off + LANES] = _rope(proj(C_KI + off, LANES), tab_ref, 6, half_i).astype(BF16)
    iw_ref[...] = proj(C_WI, LANES) * ((IDX_HEADS * IDX_DIM) ** -0.5)


def _projection(x2, g, w1, gq, wuq, gkv, wukv, tabs, seq):
    T = x2.shape[0]
    tm = TM_ROW
    nsb = seq // tm
    row = lambda i: (i, 0)
    const = lambda i: (0, 0)
    widths = [2 * LANES * HEADS_MLA, 2 * LANES * (HEADS_MLA // 2), HEADS_MLA * HEAD_DIM,
              HEADS_DIL * HEAD_DIM, HEADS_DIL * HEAD_DIM, HEADS_DIL * HEAD_DIM,
              HEADS_DSA * HEAD_DIM, HEADS_DSA * HEAD_DIM, HEADS_DSA * HEAD_DIM,
              IDX_HEADS * IDX_DIM, IDX_HEADS * IDX_DIM]
    out_shape = [jax.ShapeDtypeStruct((T, w), BF16) for w in widths]
    out_shape.append(jax.ShapeDtypeStruct((T, LANES), F32))
    out_specs = [pl.BlockSpec((tm, w), row) for w in widths] + [pl.BlockSpec((tm, LANES), row)]
    return pl.pallas_call(
        _proj_kernel,
        grid=(T // tm,),
        in_specs=[
            pl.BlockSpec((tm, D_MODEL), row),
            pl.BlockSpec((1, D_MODEL), const),
            pl.BlockSpec(w1.shape, const),
            pl.BlockSpec((1, Q_LORA), const),
            pl.BlockSpec(wuq.shape, const),
            pl.BlockSpec((1, KV_LORA), const),
            pl.BlockSpec(wukv.shape, const),
            pl.BlockSpec((9, tm, LANES), lambda i: (0, i % nsb, 0)),
        ],
        out_specs=out_specs,
        out_shape=out_shape,
        compiler_params=_cparams(("parallel",)),
        name="proj",
    )(x2, g, w1, gq, wuq, gkv, wukv, tabs)


def _rep(x, n):
    return x if n == 1 else jnp.concatenate([x] * n, axis=1)


def _softmax_step(s, v, m_ref, l_ref, acc_ref, slot):
    tk = s.shape[1]
    dv = v.shape[1]
    m_prev = m_ref[slot]
    m_new = jnp.maximum(m_prev, jnp.max(s, axis=1, keepdims=True))
    alpha = jnp.exp(m_prev - m_new)
    p = jnp.exp(s - _rep(m_new, tk // LANES))
    l_ref[slot] = alpha * l_ref[slot] + jnp.sum(p, axis=1, keepdims=True)
    acc_ref[slot] = _rep(alpha, dv // LANES) * acc_ref[slot] + _dot(p.astype(BF16), v)
    m_ref[slot] = m_new


def _init_softmax(m_ref, l_ref, acc_ref):
    m_ref[...] = jnp.full(m_ref.shape, NEG, F32)
    l_ref[...] = jnp.zeros(l_ref.shape, F32)
    acc_ref[...] = jnp.zeros(acc_ref.shape, F32)


def _causal_tile(tq):
    r = lax.broadcasted_iota(jnp.int32, (tq, tq), 0)
    c = lax.broadcasted_iota(jnp.int32, (tq, tq), 1)
    return r >= c


def _mla_kernel(q_ref, k_ref, v_ref, o_ref, m_ref, l_ref, acc_ref):
    tq = q_ref.shape[0]
    i = pl.program_id(2)
    _init_softmax(m_ref, l_ref, acc_ref)
    qs = (q_ref[:, :2 * LANES], q_ref[:, 2 * LANES:])

    def step(j, diag):
        start = pl.multiple_of(j * tq, tq)
        k = k_ref[pl.ds(start, tq), :]
        v = v_ref[pl.ds(start, tq), :]
        for hh in range(2):
            s = _dot_nt(qs[hh], k)
            if diag:
                s = jnp.where(_causal_tile(tq), s, NEG)
            _softmax_step(s, v, m_ref, l_ref, acc_ref, hh)

    def body(j, carry):
        step(j, False)
        return carry

    lax.fori_loop(0, i, body, 0)
    step(i, True)
    lane = lax.broadcasted_iota(jnp.int32, (tq, LANES), 1)
    out = jnp.where(lane < HEAD_DIM, acc_ref[0] / l_ref[0], acc_ref[1] / l_ref[1])
    o_ref[...] = out.astype(o_ref.dtype)


def _mla_attention(mq, mk, mv, batch, seq):
    T = mq.shape[0]
    tq = TQ
    nq = seq // tq
    npair = HEADS_MLA // 2
    return pl.pallas_call(
        _mla_kernel,
        grid=(batch, npair, nq),
        in_specs=[
            pl.BlockSpec((tq, 4 * LANES), lambda b, p, i: (b * nq + i, p)),
            pl.BlockSpec((seq, 2 * LANES), lambda b, p, i: (b, p)),
            pl.BlockSpec((seq, LANES), lambda b, p, i: (b, p)),
        ],
        out_specs=pl.BlockSpec((tq, LANES), lambda b, p, i: (b * nq + i, p)),
        out_shape=jax.ShapeDtypeStruct((T, HEADS_MLA * HEAD_DIM), BF16),
        scratch_shapes=[pltpu.VMEM((2, tq, LANES), F32)] * 3,
        compiler_params=_cparams(("parallel", "parallel", "arbitrary")),
        name="mla_attn",
    )(mq, mk, mv)


def _dil_bias(tq):
    nband = DIL_BAND // tq + 1
    d = (lax.broadcasted_iota(jnp.int32, (nband, tq, tq), 0) * tq
         + lax.broadcasted_iota(jnp.int32, (nband, tq, tq), 1)
         - lax.broadcasted_iota(jnp.int32, (nband, tq, tq), 2))
    cnt = jnp.zeros(d.shape, F32)
    for window, dil in DIL_PAIRS:
        ok = (d >= 0) & (d % dil == 0) & (d <= (window // dil) * dil)
        cnt = cnt + ok.astype(F32)
    return jnp.where(cnt > 0, jnp.log(jnp.maximum(cnt, 1.0)), NEG)


def _dil_kernel(q_ref, k_ref, v_ref, bias_ref, o_ref, m_ref, l_ref, acc_ref):
    tq = q_ref.shape[0]
    nband = bias_ref.shape[0]
    i = pl.program_id(2)
    _init_softmax(m_ref, l_ref, acc_ref)
    q = q_ref[...]
    lane = lax.broadcasted_iota(jnp.int32, (tq, LANES), 1)
    zero = jnp.zeros_like(q)
    qs = (jnp.where(lane < HEAD_DIM, q, zero), jnp.where(lane >= HEAD_DIM, q, zero))

    def step(delta):
        start = pl.multiple_of((i - delta) * tq, tq)
        k = k_ref[pl.ds(start, tq), :]
        v = v_ref[pl.ds(start, tq), :]
        bias = bias_ref[delta]
        for hh in range(2):
            _softmax_step(_dot_nt(qs[hh], k) + bias, v, m_ref, l_ref, acc_ref, hh)

    step(0)

    def body(delta, carry):
        step(delta)
        return carry

    lax.fori_loop(1, jnp.minimum(i, nband - 1) + 1, body, 0)
    out = jnp.where(lane < HEAD_DIM, acc_ref[0] / l_ref[0], acc_ref[1] / l_ref[1])
    o_ref[...] = out.astype(o_ref.dtype)


def _dil_attention(dq, dk, dv, bias, batch, seq):
    T = dq.shape[0]
    tq = TQ
    nq = seq // tq
    npair = HEADS_DIL // 2
    return pl.pallas_call(
        _dil_kernel,
        grid=(batch, npair, nq),
        in_specs=[
            pl.BlockSpec((tq, LANES), lambda b, p, i: (b * nq + i, p)),
            pl.BlockSpec((seq, LANES), lambda b, p, i: (b, p)),
            pl.BlockSpec((seq, LANES), lambda b, p, i: (b, p)),
            pl.BlockSpec(bias.shape, lambda b, p, i: (0, 0, 0)),
        ],
        out_specs=pl.BlockSpec((tq, LANES), lambda b, p, i: (b * nq + i, p)),
        out_shape=jax.ShapeDtypeStruct((T, HEADS_DIL * HEAD_DIM), BF16),
        scratch_shapes=[pltpu.VMEM((2, tq, LANES), F32)] * 3,
        compiler_params=_cparams(("parallel", "parallel", "arbitrary")),
        name="dil_attn",
    )(dq, dk, dv, bias)


_INT_MIN = -(2 ** 31)
_KEY_NEG_INF = _INT_MIN + 0x7FFFFF


def _key_to_float(key):
    bits = key ^ ((key >> 31) & 0x7FFFFFFF)
    return lax.bitcast_convert_type(bits, F32)


def _dsa_kernel(iq_ref, iw_ref, ik_ref, q_ref, k_ref, v_ref, o_ref,
                sc_ref, need_ref, m_ref, l_ref, acc_ref, *, ksel):
    tq = q_ref.shape[0]
    nlt = tq // LANES
    i = pl.program_id(1)
    ntile = i + 1
    causal = _causal_tile(tq)
    neg_inf = jnp.float32(-jnp.inf)

    iq = iq_ref[...]
    w = iw_ref[...]
    lane_q = lax.broadcasted_iota(jnp.int32, iq.shape, 1)
    zero_q = jnp.zeros_like(iq)
    iqs = [jnp.where((lane_q // IDX_DIM) == h, iq, zero_q) for h in range(IDX_HEADS)]
    wcols = [w[:, h:h + 1] for h in range(IDX_HEADS)]

    def score_tile(j, diag):
        kk = ik_ref[pl.ds(pl.multiple_of(j * tq, tq), tq), :]
        sc = jnp.zeros((tq, tq), F32)
        for h in range(IDX_HEADS):
            sc = sc + jnp.maximum(_dot_nt(iqs[h], kk), 0.0) * wcols[h]
        if diag:
            sc = jnp.where(causal, sc, neg_inf)
        sc_ref[j] = sc

    def score_body(j, carry):
        score_tile(j, False)
        return carry

    lax.fori_loop(0, i, score_body, 0)
    score_tile(i, True)

    def count_ge(thr):
        thr_b = jnp.broadcast_to(thr, (tq, LANES))

        def body(j, cnt):
            sc = sc_ref[j]
            for t in range(nlt):
                cnt = cnt + jnp.where(sc[:, t * LANES:(t + 1) * LANES] >= thr_b, 1, 0)
            return cnt

        cnt = lax.fori_loop(0, ntile, body, jnp.zeros((tq, LANES), jnp.int32))
        return jnp.sum(cnt, axis=1, keepdims=True)

    def bisect(b, ans):
        bit = lax.shift_left(jnp.int32(1), 31 - b)
        cand = ans | bit
        cnt = count_ge(_key_to_float(cand ^ _INT_MIN))
        return jnp.where(cnt >= ksel, cand, ans)

    ans = lax.fori_loop(0, 32, bisect, jnp.zeros((tq, 1), jnp.int32))
    upper_key = jnp.maximum((ans ^ _INT_MIN) + 1, _KEY_NEG_INF + 1)
    upper = _key_to_float(upper_key)
    upper_b = jnp.broadcast_to(upper, (tq, tq))
    need_ref[...] = jnp.broadcast_to(ksel - count_ge(upper), (tq, LANES))

    col = lax.broadcasted_iota(jnp.int32, (tq, tq), 1)
    big = jnp.int32(2 ** 30)

    def fill_cond(go):
        return go > 0

    def fill_round(go):
        need = need_ref[:, :1]

        def max_body(j, mx):
            sc = sc_ref[j]
            return jnp.maximum(mx, jnp.max(jnp.where(sc < upper_b, sc, neg_inf), axis=1, keepdims=True))

        mx = lax.fori_loop(0, ntile, max_body, jnp.full((tq, 1), neg_inf, F32))
        active = (need > 0) & (mx > neg_inf)
        mx_b = jnp.broadcast_to(mx, (tq, tq))

        def idx_body(j, best):
            sc = sc_ref[j]
            hit = (sc == mx_b) & (sc < upper_b)
            return jnp.minimum(best, jnp.min(jnp.where(hit, col + j * tq, big), axis=1, keepdims=True))

        best = lax.fori_loop(0, ntile, idx_body, jnp.full((tq, 1), big, jnp.int32))
        best = jnp.where(active, best, big)
        best_b = jnp.broadcast_to(best, (tq, tq))

        def mark_body(j, carry):
            sc = sc_ref[j]
            sc_ref[j] = jnp.where(col + j * tq == best_b, jnp.float32(jnp.inf), sc)
            return carry

        lax.fori_loop(0, ntile, mark_body, 0)
        new_need = jnp.where(active, need - 1, 0)
        need_ref[...] = jnp.broadcast_to(new_need, (tq, LANES))
        return jnp.max(new_need)

    lax.while_loop(fill_cond, fill_round, jnp.max(need_ref[:, :1]))

    _init_softmax(m_ref, l_ref, acc_ref)
    q = q_ref[...]
    lane = lax.broadcasted_iota(jnp.int32, q.shape, 1)
    zero = jnp.zeros_like(q)
    qs = [jnp.where((lane // HEAD_DIM) == h, q, zero) for h in range(HEADS_DSA)]

    def attn_tile(j, diag):
        start = pl.multiple_of(j * tq, tq)
        k = k_ref[pl.ds(start, tq), :]
        v = v_ref[pl.ds(start, tq), :]
        sel = sc_ref[j] >= upper_b
        if diag:
            sel = sel & causal
        for h in range(HEADS_DSA):
            s = jnp.where(sel, _dot_nt(qs[h], k), NEG)
            _softmax_step(s, v, m_ref, l_ref, acc_ref, h)

    def attn_body(j, carry):
        attn_tile(j, False)
        return carry

    lax.fori_loop(0, i, attn_body, 0)
    attn_tile(i, True)
    out = jnp.zeros(q.shape, F32)
    for h in range(HEADS_DSA):
        out = jnp.where((lane // HEAD_DIM) == h, acc_ref[h] / _rep(l_ref[h], q.shape[1] // LANES), out)
    o_ref[...] = out.astype(o_ref.dtype)


def _dsa_attention(iq, iw, ik, sq, sk, sv, batch, seq):
    T = sq.shape[0]
    tq = TQ
    nq = seq // tq
    ksel = min(DSA_TOPK, seq // 4)
    width = HEADS_DSA * HEAD_DIM
    qrow = lambda b, i: (b * nq + i, 0)
    full = lambda b, i: (b, 0)
    return pl.pallas_call(
        functools.partial(_dsa_kernel, ksel=ksel),
        grid=(batch, nq),
        in_specs=[
            pl.BlockSpec((tq, IDX_HEADS * IDX_DIM), qrow),
            pl.BlockSpec((tq, LANES), qrow),
            pl.BlockSpec((seq, IDX_HEADS * IDX_DIM), full),
            pl.BlockSpec((tq, width), qrow),
            pl.BlockSpec((seq, width), full),
            pl.BlockSpec((seq, width), full),
        ],
        out_specs=pl.BlockSpec((tq, width), qrow),
        out_shape=jax.ShapeDtypeStruct((T, width), BF16),
        scratch_shapes=[
            pltpu.VMEM((nq, tq, tq), F32),
            pltpu.VMEM((tq, LANES), jnp.int32),
            pltpu.VMEM((HEADS_DSA, tq, LANES), F32),
            pltpu.VMEM((HEADS_DSA, tq, LANES), F32),
            pltpu.VMEM((HEADS_DSA, tq, width), F32),
        ],
        compiler_params=_cparams(("parallel", "arbitrary")),
        name="dsa_attn",
    )(iq, iw, ik, sq, sk, sv)


def _oproj_kernel(x_ref, oa_ref, ob_ref, oc_ref, wo_ref, g_ref, x1_ref, h_ref):
    na = HEADS_MLA * HEAD_DIM
    nb = HEADS_DIL * HEAD_DIM
    x1 = (x_ref[...] + _dot(oa_ref[...], wo_ref[:na, :])
          + _dot(ob_ref[...], wo_ref[na:na + nb, :])
          + _dot(oc_ref[...], wo_ref[na + nb:, :]))
    x1_ref[...] = x1
    h_ref[...] = _rms(x1, g_ref[...]).astype(BF16)


def _out_projection(x2, oa, ob, oc, wo, g):
    T = x2.shape[0]
    tm = TM_FFN
    row = lambda i: (i, 0)
    const = lambda i: (0, 0)
    return pl.pallas_call(
        _oproj_kernel,
        grid=(T // tm,),
        in_specs=[
            pl.BlockSpec((tm, D_MODEL), row),
            pl.BlockSpec((tm, oa.shape[1]), row),
            pl.BlockSpec((tm, ob.shape[1]), row),
            pl.BlockSpec((tm, oc.shape[1]), row),
            pl.BlockSpec(wo.shape, const),
            pl.BlockSpec((1, D_MODEL), const),
        ],
        out_specs=[pl.BlockSpec((tm, D_MODEL), row), pl.BlockSpec((tm, D_MODEL), row)],
        out_shape=[jax.ShapeDtypeStruct((T, D_MODEL), F32), jax.ShapeDtypeStruct((T, D_MODEL), BF16)],
        compiler_params=_cparams(("parallel",)),
        name="oproj",
    )(x2, oa, ob, oc, wo, g)


def _ffn_kernel(h_ref, halo_ref, x1_ref, wg_ref, wu_ref, cg_ref, cu_ref, wd_ref, o_ref,
                ug_ref, uu_ref, acc_ref, *, blocks_per_seq):
    tm = h_ref.shape[0]
    hal = halo_ref.shape[0]
    nchunk = wg_ref.shape[0]
    first = (pl.program_id(0) % blocks_per_seq) == 0
    keep = jnp.where(first, 0.0, 1.0).astype(F32)
    h = h_ref[...]
    hh = halo_ref[...]
    acc_ref[...] = jnp.zeros(acc_ref.shape, F32)

    def conv(u_ref, w):
        y = w[CONV_WIDTH:CONV_WIDTH + 1, :]
        for tap in range(CONV_WIDTH):
            off = hal - (CONV_WIDTH - 1) + tap
            y = y + u_ref[pl.ds(off, tm), :] * w[tap:tap + 1, :]
        return y

    def chunk(j, carry):
        wg = wg_ref[j]
        wu = wu_ref[j]
        ug_ref[:hal, :] = _dot(hh, wg) * keep
        ug_ref[hal:, :] = _dot(h, wg)
        uu_ref[:hal, :] = _dot(hh, wu) * keep
        uu_ref[hal:, :] = _dot(h, wu)
        gate = conv(ug_ref, cg_ref[j])
        up = conv(uu_ref, cu_ref[j])
        act = gate * jax.nn.sigmoid(gate) * up
        acc_ref[...] += _dot(act.astype(BF16), wd_ref[j])
        return carry

    lax.fori_loop(0, nchunk, chunk, 0)
    o_ref[...] = x1_ref[...] + acc_ref[...]


def _ffn(h2, x1, wg, wu, cg, cu, wd, seq):
    T = x1.shape[0]
    tm = TM_FFN
    hal = BF16_SUBLANES
    tn = wg.shape[2]
    row = lambda i: (i, 0)
    c3 = lambda i: (0, 0, 0)
    return pl.pallas_call(
        functools.partial(_ffn_kernel, blocks_per_seq=seq // tm),
        grid=(T // tm,),
        in_specs=[
            pl.BlockSpec((tm, D_MODEL), row),
            pl.BlockSpec((hal, D_MODEL), lambda i: (jnp.maximum(i * (tm // hal) - 1, 0), 0)),
            pl.BlockSpec((tm, D_MODEL), row),
            pl.BlockSpec(wg.shape, c3),
            pl.BlockSpec(wu.shape, c3),
            pl.BlockSpec(cg.shape, c3),
            pl.BlockSpec(cu.shape, c3),
            pl.BlockSpec(wd.shape, c3),
        ],
        out_specs=pl.BlockSpec((tm, D_MODEL), row),
        out_shape=jax.ShapeDtypeStruct((T, D_MODEL), F32),
        scratch_shapes=[
            pltpu.VMEM((hal + tm, tn), F32),
            pltpu.VMEM((hal + tm, tn), F32),
            pltpu.VMEM((tm, D_MODEL), F32),
        ],
        compiler_params=_cparams(("parallel",)),
        name="ffn",
    )(h2, h2, x1, wg, wu, cg, cu, wd)


def _final_norm_kernel(x_ref, g_ref, o_ref):
    o_ref[...] = _rms(x_ref[...], g_ref[...])


def _final_norm(x2, g):
    T = x2.shape[0]
    tm = TM_FFN
    return pl.pallas_call(
        _final_norm_kernel,
        grid=(T // tm,),
        in_specs=[pl.BlockSpec((tm, D_MODEL), lambda i: (i, 0)),
                  pl.BlockSpec((1, D_MODEL), lambda i: (0, 0))],
        out_specs=pl.BlockSpec((tm, D_MODEL), lambda i: (i, 0)),
        out_shape=jax.ShapeDtypeStruct((T, D_MODEL), F32),
        compiler_params=_cparams(("parallel",)),
        name="final_norm",
    )(x2, g)


def _rope_tables(seq):
    def base(dim):
        inv = jnp.power(jnp.float32(ROPE_THETA), -jnp.arange(0, dim, 2, dtype=F32) / dim)
        ang = jnp.arange(seq, dtype=F32)[:, None] * inv[None, :]
        return jnp.cos(ang), jnp.sin(ang)

    def tables(dim, period):
        cos, sin = base(dim)
        half = dim // 2
        pad = period - dim
        c = jnp.concatenate([cos, cos, jnp.ones((seq, pad), F32)], axis=1)
        sn = jnp.concatenate([-sin, jnp.zeros((seq, half + pad), F32)], axis=1)
        sp = jnp.concatenate([jnp.zeros((seq, half), F32), sin, jnp.zeros((seq, pad), F32)], axis=1)
        reps = LANES // period
        return [jnp.tile(t, (1, reps)) for t in (c, sn, sp)]

    tabs = (tables(MLA_ROPE, LANES) + tables(HEAD_DIM // ROT_FRAC, HEAD_DIM)
            + tables(IDX_DIM // ROT_FRAC, IDX_DIM))
    return jnp.stack(tabs)


def _pad_cols(w, width):
    return jnp.pad(w, ((0, 0), (0, width - w.shape[1])))


def _layer_weights(w_in, w_uq, w_ukv, w_up, conv_w, conv_b, w_down):
    o = 0
    parts = {}
    for name, width in (("cq", Q_LORA), ("ckv", KV_LORA), ("kpe", MLA_ROPE),
                        ("dil", 3 * HEADS_DIL * HEAD_DIM), ("dsa", 3 * HEADS_DSA * HEAD_DIM),
                        ("qi", IDX_HEADS * IDX_DIM), ("ki", IDX_DIM), ("wi", IDX_HEADS)):
        parts[name] = w_in[:, o:o + width]
        o += width
    w1 = jnp.concatenate([
        parts["cq"], parts["ckv"], _pad_cols(parts["kpe"], LANES), parts["dil"], parts["dsa"],
        parts["qi"], jnp.tile(parts["ki"], (1, IDX_HEADS)), _pad_cols(parts["wi"], LANES)],
        axis=1).astype(BF16)
    assert w1.shape[1] == N_PROJ

    dq = MLA_NOPE + MLA_ROPE
    cols = []
    for hd in range(HEADS_MLA):
        nope = w_uq[:, dq * hd:dq * hd + MLA_NOPE]
        pe = w_uq[:, dq * hd + MLA_NOPE:dq * (hd + 1)]
        z = jnp.zeros((Q_LORA, MLA_NOPE), w_uq.dtype)
        slot = [nope, z] if hd % 2 == 0 else [z, nope]
        cols += slot + [_pad_cols(pe, LANES)]
    wuq = jnp.concatenate(cols, axis=1).astype(BF16)

    dkv = MLA_NOPE + HEAD_DIM
    kn = [w_ukv[:, dkv * hd:dkv * hd + MLA_NOPE] for hd in range(HEADS_MLA)]
    vv = [w_ukv[:, dkv * hd + MLA_NOPE:dkv * (hd + 1)] for hd in range(HEADS_MLA)]
    wukv = jnp.concatenate(kn + vv, axis=1).astype(BF16)

    nchunk = D_FF // TN_FFN
    def chunks(w):
        return w.reshape(w.shape[0], nchunk, TN_FFN).transpose(1, 0, 2)
    wg = chunks(w_up[:, :D_FF]).astype(BF16)
    wu = chunks(w_up[:, D_FF:]).astype(BF16)
    cw = jnp.concatenate([conv_w, conv_b[None, :],
                          jnp.zeros((8 - CONV_WIDTH - 1, 2 * D_FF), F32)], axis=0)
    cg = chunks(cw[:, :D_FF])
    cu = chunks(cw[:, D_FF:])
    wd = w_down.reshape(nchunk, TN_FFN, D_MODEL).astype(BF16)
    return w1, wuq, wukv, wg, wu, cg, cu, wd


def kernel(x, g_attn, w_in, g_q_lat, w_uq, g_kv_lat, w_ukv, w_o, g_ffn, w_up, conv_w, conv_b,
           w_down, g_final):
    batch, seq, _ = x.shape
    depth = w_in.shape[0]
    assert seq % DIL_BAND == 0 and seq % TM_FFN == 0 and seq % TQ == 0
    assert D_FF % TN_FFN == 0
    T = batch * seq
    tabs = _rope_tables(seq)
    bias = _dil_bias(TQ)
    xf = x.reshape(T, D_MODEL)
    for l in range(depth):
        w1, wuq, wukv, wg, wu, cg, cu, wd = _layer_weights(
            w_in[l], w_uq[l], w_ukv[l], w_up[l], conv_w[l], conv_b[l], w_down[l])
        (mq, mk, mv, dq, dk, dv, sq, sk, sv, iq, ik, iw) = _projection(
            xf, g_attn[l][None, :], w1, g_q_lat[l][None, :], wuq, g_kv_lat[l][None, :], wukv,
            tabs, seq)
        oa = _mla_attention(mq, mk, mv, batch, seq)
        ob = _dil_attention(dq, dk, dv, bias, batch, seq)
        oc = _dsa_attention(iq, iw, ik, sq, sk, sv, batch, seq)
        x1, h2 = _out_projection(xf, oa, ob, oc, w_o[l].astype(BF16), g_ffn[l][None, :])
        xf = _ffn(h2, x1, wg, wu, cg, cu, wd, seq)
    return _final_norm(xf, g_final[None, :]).reshape(batch, seq, D_MODEL)
```

```python
import functools
import math

import jax
import jax.numpy as jnp
from jax import lax
from jax.experimental import pallas as pl
from jax.experimental.pallas import tpu as pltpu

D_MODEL = 1024
HEAD_DIM = 64
HEADS_MLA = 4
HEADS_DIL = 8
HEADS_DSA = 4
Q_LORA = 256
KV_LORA = 128
MLA_NOPE = 64
MLA_ROPE = 32
DIL_PAIRS = ((128, 1), (512, 4), (2048, 16))
IDX_HEADS = 8
IDX_DIM = 32
DSA_TOPK = 256
D_FF = 2816
CONV_WIDTH = 3
ROPE_THETA = 500000.0
ROT_FRAC = 4
NORM_EPS = 1e-6

LANES = 128
BF16_SUBLANES = 16
VMEM_LIMIT = 56 * 1024 * 1024
NEG = -1e30
LOG2E = math.log2(math.e)
F32 = jnp.float32
BF16 = jnp.bfloat16

TQ = 512
TM_ROW = 256
TM_FFN = 512
TN_FFN = 256
COUNT_ROWS = 128
DIL_BAND = max(w for w, _ in DIL_PAIRS)

C_CQ = 0
C_CKV = C_CQ + Q_LORA
C_KPE = C_CKV + KV_LORA
C_DIL = C_KPE + LANES
C_DSA = C_DIL + 3 * HEADS_DIL * HEAD_DIM
C_QI = C_DSA + 3 * HEADS_DSA * HEAD_DIM
C_KI = C_QI + IDX_HEADS * IDX_DIM
C_WI = C_KI + IDX_HEADS * IDX_DIM
N_PROJ = C_WI + LANES


def _cparams(sem):
    return pltpu.CompilerParams(dimension_semantics=sem, vmem_limit_bytes=VMEM_LIMIT)


def _resident(shape, index_map):
    return pl.BlockSpec(shape, index_map, pipeline_mode=pl.Buffered(1))


def _rms(x, g):
    return x * lax.rsqrt(jnp.mean(x * x, axis=-1, keepdims=True) + NORM_EPS) * g


def _dot(a, b):
    return jnp.dot(a, b, preferred_element_type=F32)


def _dot_nt(a, b):
    return lax.dot_general(a, b, (((1,), (1,)), ((), ())), preferred_element_type=F32)


def _rope(x, tab_ref, t0, half):
    c = tab_ref[t0]
    sn = tab_ref[t0 + 1]
    sp = tab_ref[t0 + 2]
    return (x * c + pltpu.roll(x, LANES - half, axis=1) * sn
            + pltpu.roll(x, half, axis=1) * sp)


def _proj_kernel(x_ref, g_ref, w1_ref, gq_ref, wuq_ref, gkv_ref, wukv_ref, tab_ref,
                 mq_ref, mk_ref, mv_ref, dq_ref, dk_ref, dv_ref,
                 sq_ref, sk_ref, sv_ref, iq_ref, ik_ref, iw_ref):
    h = _rms(x_ref[...], g_ref[...]).astype(BF16)

    def proj(lo, width):
        return _dot(h, w1_ref[:, lo:lo + width])

    mla_scale = (MLA_NOPE + MLA_ROPE) ** -0.5 * LOG2E
    cq = _rms(proj(C_CQ, Q_LORA), gq_ref[...]).astype(BF16)
    for hd in range(HEADS_MLA):
        base = 2 * LANES * hd
        qn = _dot(cq, wuq_ref[:, base:base + LANES])
        qp = _rope(_dot(cq, wuq_ref[:, base + LANES:base + 2 * LANES]), tab_ref, 0, MLA_ROPE // 2)
        mq_ref[:, base:base + LANES] = (qn * mla_scale).astype(BF16)
        mq_ref[:, base + LANES:base + 2 * LANES] = (qp * mla_scale).astype(BF16)
    ckv = _rms(proj(C_CKV, KV_LORA), gkv_ref[...]).astype(BF16)
    kpe = _rope(proj(C_KPE, LANES), tab_ref, 0, MLA_ROPE // 2).astype(BF16)
    for pr in range(HEADS_MLA // 2):
        mk_ref[:, 2 * LANES * pr:2 * LANES * pr + LANES] = _dot(
            ckv, wukv_ref[:, LANES * pr:LANES * (pr + 1)]).astype(BF16)
        mk_ref[:, 2 * LANES * pr + LANES:2 * LANES * (pr + 1)] = kpe
    mv_ref[...] = _dot(ckv, wukv_ref[:, HEADS_MLA * MLA_NOPE:]).astype(BF16)

    head_scale = HEAD_DIM ** -0.5 * LOG2E
    half_h = HEAD_DIM // ROT_FRAC // 2

    def qkv(c0, nheads, q_ref, k_ref, v_ref):
        width = nheads * HEAD_DIM
        for gidx in range(width // LANES):
            off = gidx * LANES
            q = _rope(proj(c0 + off, LANES), tab_ref, 3, half_h)
            q_ref[:, off:off + LANES] = (q * head_scale).astype(BF16)
            k = _rope(proj(c0 + width + off, LANES), tab_ref, 3, half_h)
            k_ref[:, off:off + LANES] = k.astype(BF16)
        v_ref[...] = proj(c0 + 2 * width, width).astype(BF16)

    qkv(C_DIL, HEADS_DIL, dq_ref, dk_ref, dv_ref)
    qkv(C_DSA, HEADS_DSA, sq_ref, sk_ref, sv_ref)

    half_i = IDX_DIM // ROT_FRAC // 2
    for gidx in range(IDX_HEADS * IDX_DIM // LANES):
        off = gidx * LANES
        iq_ref[:, off:off + LANES] = _rope(proj(C_QI + off, LANES), tab_ref, 6, half_i).astype(BF16)
        ik_ref[:, off:off + LANES] = _rope(proj(C_KI + off, LANES), tab_ref, 6, half_i).astype(BF16)
    iw_ref[...] = proj(C_WI, LANES) * ((IDX_HEADS * IDX_DIM) ** -0.5)


def _projection(x2, g, w1, gq, wuq, gkv, wukv, tabs, seq):
    T = x2.shape[0]
    tm = TM_ROW
    nsb = seq // tm
    row = lambda i: (i, 0)
    const = lambda i: (0, 0)
    widths = [2 * LANES * HEADS_MLA, 2 * LANES * (HEADS_MLA // 2), HEADS_MLA * HEAD_DIM,
              HEADS_DIL * HEAD_DIM, HEADS_DIL * HEAD_DIM, HEADS_DIL * HEAD_DIM,
              HEADS_DSA * HEAD_DIM, HEADS_DSA * HEAD_DIM, HEADS_DSA * HEAD_DIM,
              IDX_HEADS * IDX_DIM, IDX_HEADS * IDX_DIM]
    out_shape = [jax.ShapeDtypeStruct((T, w), BF16) for w in widths]
    out_shape.append(jax.ShapeDtypeStruct((T, LANES), F32))
    out_specs = [pl.BlockSpec((tm, w), row) for w in widths] + [pl.BlockSpec((tm, LANES), row)]
    return pl.pallas_call(
        _proj_kernel,
        grid=(T // tm,),
        in_specs=[
            pl.BlockSpec((tm, D_MODEL), row),
            pl.BlockSpec((1, D_MODEL), const),
            pl.BlockSpec(w1.shape, const),
            pl.BlockSpec((1, Q_LORA), const),
            pl.BlockSpec(wuq.shape, const),
            pl.BlockSpec((1, KV_LORA), const),
            pl.BlockSpec(wukv.shape, const),
            pl.BlockSpec((9, tm, LANES), lambda i: (0, i % nsb, 0)),
        ],
        out_specs=out_specs,
        out_shape=out_shape,
        compiler_params=_cparams(("parallel",)),
        name="proj",
    )(x2, g, w1, gq, wuq, gkv, wukv, tabs)


def _rep(x, n):
    return x if n == 1 else jnp.concatenate([x] * n, axis=1)


def _flash_tile(q_stack, k, v, nheads, mask_fn, m_ref, l_ref, acc_ref):
    tq = q_stack.shape[0] // nheads
    tk = k.shape[0]
    dv = v.shape[1]
    s_all = _dot_nt(q_stack, k)
    ps = []
    alphas = []
    for h in range(nheads):
        s = mask_fn(s_all[h * tq:(h + 1) * tq])
        m_prev = m_ref[h]
        m_new = jnp.maximum(m_prev, jnp.max(s, axis=1, keepdims=True))
        alpha = jnp.exp2(m_prev - m_new)
        p = jnp.exp2(s - _rep(m_new, tk // LANES))
        l_ref[h] = alpha * l_ref[h] + jnp.sum(p, axis=1, keepdims=True)
        m_ref[h] = m_new
        ps.append(p.astype(BF16))
        alphas.append(alpha)
    pv = _dot(jnp.concatenate(ps, axis=0), v)
    for h in range(nheads):
        acc_ref[h] = _rep(alphas[h], dv // LANES) * acc_ref[h] + pv[h * tq:(h + 1) * tq]


def _flash_init(m_ref, l_ref, acc_ref):
    m_ref[...] = jnp.full(m_ref.shape, NEG, F32)
    l_ref[...] = jnp.zeros(l_ref.shape, F32)
    acc_ref[...] = jnp.zeros(acc_ref.shape, F32)


def _flash_out(nheads, l_ref, acc_ref):
    tq, dv = acc_ref.shape[1:]
    lane = lax.broadcasted_iota(jnp.int32, (tq, dv), 1)
    out = jnp.zeros((tq, dv), F32)
    for h in range(nheads):
        out = jnp.where((lane // HEAD_DIM) == h, acc_ref[h] / _rep(l_ref[h], dv // LANES), out)
    return out


def _head_stack(q, nheads):
    lane = lax.broadcasted_iota(jnp.int32, q.shape, 1)
    zero = jnp.zeros_like(q)
    return jnp.concatenate([jnp.where((lane // HEAD_DIM) == h, q, zero) for h in range(nheads)], axis=0)


def _causal_tile(tq):
    r = lax.broadcasted_iota(jnp.int32, (tq, tq), 0)
    c = lax.broadcasted_iota(jnp.int32, (tq, tq), 1)
    return r >= c


def _mla_kernel(q_ref, k_ref, v_ref, o_ref, m_ref, l_ref, acc_ref):
    tq = q_ref.shape[0]
    i = pl.program_id(2)
    _flash_init(m_ref, l_ref, acc_ref)
    q_stack = jnp.concatenate([q_ref[:, :2 * LANES], q_ref[:, 2 * LANES:]], axis=0)

    def step(j, mask_fn):
        start = pl.multiple_of(j * tq, tq)
        _flash_tile(q_stack, k_ref[pl.ds(start, tq), :], v_ref[pl.ds(start, tq), :], 2,
                    mask_fn, m_ref, l_ref, acc_ref)

    def body(j, carry):
        step(j, lambda s: s)
        return carry

    lax.fori_loop(0, i, body, 0)
    step(i, lambda s: jnp.where(_causal_tile(tq), s, NEG))
    o_ref[...] = _flash_out(2, l_ref, acc_ref).astype(o_ref.dtype)


def _mla_attention(mq, mk, mv, batch, seq):
    T = mq.shape[0]
    tq = TQ
    nq = seq // tq
    npair = HEADS_MLA // 2
    return pl.pallas_call(
        _mla_kernel,
        grid=(batch, npair, nq),
        in_specs=[
            pl.BlockSpec((tq, 4 * LANES), lambda b, p, i: (b * nq + i, p)),
            _resident((seq, 2 * LANES), lambda b, p, i: (b, p)),
            _resident((seq, LANES), lambda b, p, i: (b, p)),
        ],
        out_specs=pl.BlockSpec((tq, LANES), lambda b, p, i: (b * nq + i, p)),
        out_shape=jax.ShapeDtypeStruct((T, HEADS_MLA * HEAD_DIM), BF16),
        scratch_shapes=[pltpu.VMEM((2, tq, LANES), F32)] * 3,
        compiler_params=_cparams(("parallel", "parallel", "arbitrary")),
        name="mla_attn",
    )(mq, mk, mv)


def _dil_bias(tq):
    nband = DIL_BAND // tq + 1
    d = (lax.broadcasted_iota(jnp.int32, (nband, tq, tq), 0) * tq
         + lax.broadcasted_iota(jnp.int32, (nband, tq, tq), 1)
         - lax.broadcasted_iota(jnp.int32, (nband, tq, tq), 2))
    cnt = jnp.zeros(d.shape, F32)
    for window, dil in DIL_PAIRS:
        ok = (d >= 0) & (d % dil == 0) & (d <= (window // dil) * dil)
        cnt = cnt + ok.astype(F32)
    return jnp.where(cnt > 0, jnp.log2(jnp.maximum(cnt, 1.0)), NEG)


def _dil_kernel(q_ref, k_ref, v_ref, bias_ref, o_ref, m_ref, l_ref, acc_ref):
    tq = q_ref.shape[0]
    nband = bias_ref.shape[0]
    i = pl.program_id(2)
    _flash_init(m_ref, l_ref, acc_ref)
    q_stack = _head_stack(q_ref[...], 2)

    def step(delta):
        start = pl.multiple_of((i - delta) * tq, tq)
        bias = bias_ref[delta]
        _flash_tile(q_stack, k_ref[pl.ds(start, tq), :], v_ref[pl.ds(start, tq), :], 2,
                    lambda s: s + bias, m_ref, l_ref, acc_ref)

    step(0)

    def body(delta, carry):
        step(delta)
        return carry

    lax.fori_loop(1, jnp.minimum(i, nband - 1) + 1, body, 0)
    o_ref[...] = _flash_out(2, l_ref, acc_ref).astype(o_ref.dtype)


def _dil_attention(dq, dk, dv, bias, batch, seq):
    T = dq.shape[0]
    tq = TQ
    nq = seq // tq
    npair = HEADS_DIL // 2
    return pl.pallas_call(
        _dil_kernel,
        grid=(batch, npair, nq),
        in_specs=[
            pl.BlockSpec((tq, LANES), lambda b, p, i: (b * nq + i, p)),
            _resident((seq, LANES), lambda b, p, i: (b, p)),
            _resident((seq, LANES), lambda b, p, i: (b, p)),
            _resident(bias.shape, lambda b, p, i: (0, 0, 0)),
        ],
        out_specs=pl.BlockSpec((tq, LANES), lambda b, p, i: (b * nq + i, p)),
        out_shape=jax.ShapeDtypeStruct((T, HEADS_DIL * HEAD_DIM), BF16),
        scratch_shapes=[pltpu.VMEM((2, tq, LANES), F32)] * 3,
        compiler_params=_cparams(("parallel", "parallel", "arbitrary")),
        name="dil_attn",
    )(dq, dk, dv, bias)


_INT_MIN = -(2 ** 31)
_KEY_NEG_INF = _INT_MIN + 0x7FFFFF


def _ordered_bits(x):
    return x ^ ((x >> 31) & 0x7FFFFFFF)


def _key_to_float(key):
    return lax.bitcast_convert_type(_ordered_bits(key), F32)


def _float_to_key(x):
    return _ordered_bits(lax.bitcast_convert_type(x, jnp.int32))


def _dsa_kernel(iq_ref, iw_ref, ik_ref, q_ref, k_ref, v_ref, o_ref,
                sc_ref, cm_ref, need_ref, m_ref, l_ref, acc_ref, *, ksel):
    tq = q_ref.shape[0]
    nlt = tq // LANES
    i = pl.program_id(1)
    ntile = i + 1
    causal = _causal_tile(tq)
    neg_inf = jnp.float32(-jnp.inf)

    iq_stack = jnp.concatenate(
        [jnp.where((lax.broadcasted_iota(jnp.int32, iq_ref.shape, 1) // IDX_DIM) == h,
                   iq_ref[...], jnp.zeros(iq_ref.shape, BF16)) for h in range(IDX_HEADS)], axis=0)
    w = iw_ref[...]
    wcols = [w[:, h:h + 1] for h in range(IDX_HEADS)]
    cm_ref[...] = jnp.full(cm_ref.shape, neg_inf, F32)

    def score_tile(j, diag):
        kk = ik_ref[pl.ds(pl.multiple_of(j * tq, tq), tq), :]
        sc = jnp.zeros((tq, tq), F32)
        for half in range(2):
            nh = IDX_HEADS // 2
            d = _dot_nt(iq_stack[half * nh * tq:(half + 1) * nh * tq], kk)
            for hh in range(nh):
                sc = sc + jnp.maximum(d[hh * tq:(hh + 1) * tq], 0.0) * wcols[half * nh + hh]
        if diag:
            sc = jnp.where(causal, sc, neg_inf)
        sc_ref[j] = sc
        cm_ref[...] = jnp.maximum(cm_ref[...], sc)

    def score_body(j, carry):
        score_tile(j, False)
        return carry

    lax.fori_loop(0, i, score_body, 0)
    score_tile(i, True)

    ones8 = jnp.ones((8, LANES), BF16)

    def to_rows(x):
        return jnp.transpose(jnp.broadcast_to(x[:1], (LANES, tq)))

    def to_lanes(x):
        return jnp.transpose(jnp.broadcast_to(x, (tq, LANES)))[:8]

    def count_ge(thr_rows):
        parts = []
        for r0 in range(0, tq, COUNT_ROWS):
            thr_b = thr_rows[r0:r0 + COUNT_ROWS]

            def body(j, cnt, r0=r0, thr_b=thr_b):
                sc = sc_ref[j, r0:r0 + COUNT_ROWS, :]
                for t in range(nlt):
                    cnt = cnt + jnp.where(sc[:, t * LANES:(t + 1) * LANES] >= thr_b, 1, 0)
                return cnt

            parts.append(lax.fori_loop(0, ntile, body, jnp.zeros((COUNT_ROWS, LANES), jnp.int32)))
        partial = jnp.concatenate(parts, axis=0).astype(F32).astype(BF16)
        return _dot_nt(ones8, partial).astype(jnp.int32)

    cm = cm_ref[...]
    lo0 = jnp.maximum(_float_to_key(to_lanes(jnp.min(cm, axis=1, keepdims=True))), _KEY_NEG_INF)
    hi0 = _float_to_key(to_lanes(jnp.max(cm, axis=1, keepdims=True))) + 1
    unknown = jnp.full((8, tq), 2 ** 30, jnp.int32)

    def open_rows(lo, hi, clo):
        return (clo != ksel) & (hi - 1 > lo)

    def sel_cond(st):
        return st[0] > 0

    def sel_body(st):
        _, lo, hi, clo, chi = st
        mid = (lo >> 1) + (hi >> 1) + (lo & hi & 1)
        cnt = count_ge(to_rows(_key_to_float(mid)))
        ge = cnt >= ksel
        act = open_rows(lo, hi, clo)
        up = act & ge
        dn = act & jnp.logical_not(ge)
        lo = jnp.where(up, mid, lo)
        clo = jnp.where(up, cnt, clo)
        hi = jnp.where(dn, mid, hi)
        chi = jnp.where(dn, cnt, chi)
        go = jnp.max(jnp.where(open_rows(lo, hi, clo), 1, 0))
        return go, lo, hi, clo, chi

    go0 = jnp.max(jnp.where(open_rows(lo0, hi0, unknown), 1, 0))
    _, lo, hi, clo, chi = lax.while_loop(
        sel_cond, sel_body, (go0, lo0, hi0, unknown, jnp.zeros((8, tq), jnp.int32)))
    exact = clo == ksel
    upper_b = _rep(to_rows(_key_to_float(jnp.where(exact, lo, hi))), nlt)
    need0 = jnp.where(exact, 0, ksel - chi)
    need_ref[...] = to_rows(need0.astype(F32)).astype(jnp.int32)

    col = lax.broadcasted_iota(jnp.int32, (tq, tq), 1)
    big = jnp.int32(2 ** 30)

    def fill_cond(go):
        return go > 0

    def fill_round(go):
        need = need_ref[:, :1]

        def max_body(j, mx):
            sc = sc_ref[j]
            return jnp.maximum(mx, jnp.max(jnp.where(sc < upper_b, sc, neg_inf), axis=1, keepdims=True))

        mx = lax.fori_loop(0, ntile, max_body, jnp.full((tq, 1), neg_inf, F32))
        active = (need > 0) & (mx > neg_inf)
        mx_b = jnp.broadcast_to(mx, (tq, tq))

        def idx_body(j, best):
            sc = sc_ref[j]
            hit = (sc == mx_b) & (sc < upper_b)
            return jnp.minimum(best, jnp.min(jnp.where(hit, col + j * tq, big), axis=1, keepdims=True))

        best = lax.fori_loop(0, ntile, idx_body, jnp.full((tq, 1), big, jnp.int32))
        best = jnp.where(active, best, big)
        best_b = jnp.broadcast_to(best, (tq, tq))

        def mark_body(j, carry):
            sc = sc_ref[j]
            sc_ref[j] = jnp.where(col + j * tq == best_b, jnp.float32(jnp.inf), sc)
            return carry

        lax.fori_loop(0, ntile, mark_body, 0)
        new_need = jnp.where(active, need - 1, 0)
        need_ref[...] = jnp.broadcast_to(new_need, (tq, LANES))
        return jnp.max(new_need)

    lax.while_loop(fill_cond, fill_round, jnp.max(need0))

    _flash_init(m_ref, l_ref, acc_ref)
    q_stack = _head_stack(q_ref[...], HEADS_DSA)

    def attn_tile(j, diag):
        start = pl.multiple_of(j * tq, tq)
        sel = sc_ref[j] >= upper_b
        if diag:
            sel = sel & causal
        _flash_tile(q_stack, k_ref[pl.ds(start, tq), :], v_ref[pl.ds(start, tq), :], HEADS_DSA,
                    lambda s: jnp.where(sel, s, NEG), m_ref, l_ref, acc_ref)

    def attn_body(j, carry):
        attn_tile(j, False)
        return carry

    lax.fori_loop(0, i, attn_body, 0)
    attn_tile(i, True)
    o_ref[...] = _flash_out(HEADS_DSA, l_ref, acc_ref).astype(o_ref.dtype)


def _dsa_attention(iq, iw, ik, sq, sk, sv, batch, seq):
    T = sq.shape[0]
    tq = TQ
    nq = seq // tq
    ksel = min(DSA_TOPK, seq // 4)
    assert tq >= ksel
    width = HEADS_DSA * HEAD_DIM
    qrow = lambda b, i: (b * nq + i, 0)
    full = lambda b, i: (b, 0)
    return pl.pallas_call(
        functools.partial(_dsa_kernel, ksel=ksel),
        grid=(batch, nq),
        in_specs=[
            pl.BlockSpec((tq, IDX_HEADS * IDX_DIM), qrow),
            pl.BlockSpec((tq, LANES), qrow),
            _resident((seq, IDX_HEADS * IDX_DIM), full),
            pl.BlockSpec((tq, width), qrow),
            _resident((seq, width), full),
            _resident((seq, width), full),
        ],
        out_specs=pl.BlockSpec((tq, width), qrow),
        out_shape=jax.ShapeDtypeStruct((T, width), BF16),
        scratch_shapes=[
            pltpu.VMEM((nq, tq, tq), F32),
            pltpu.VMEM((tq, tq), F32),
            pltpu.VMEM((tq, LANES), jnp.int32),
            pltpu.VMEM((HEADS_DSA, tq, LANES), F32),
            pltpu.VMEM((HEADS_DSA, tq, LANES), F32),
            pltpu.VMEM((HEADS_DSA, tq, width), F32),
        ],
        compiler_params=_cparams(("parallel", "arbitrary")),
        name="dsa_attn",
    )(iq, iw, ik, sq, sk, sv)


def _oproj_kernel(x_ref, oa_ref, ob_ref, oc_ref, wo_ref, g_ref, x1_ref, h_ref):
    na = HEADS_MLA * HEAD_DIM
    nb = HEADS_DIL * HEAD_DIM
    x1 = (x_ref[...] + _dot(oa_ref[...], wo_ref[:na, :])
          + _dot(ob_ref[...], wo_ref[na:na + nb, :])
          + _dot(oc_ref[...], wo_ref[na + nb:, :]))
    x1_ref[...] = x1
    h_ref[...] = _rms(x1, g_ref[...]).astype(BF16)


def _out_projection(x2, oa, ob, oc, wo, g):
    T = x2.shape[0]
    tm = TM_FFN
    row = lambda i: (i, 0)
    const = lambda i: (0, 0)
    return pl.pallas_call(
        _oproj_kernel,
        grid=(T // tm,),
        in_specs=[
            pl.BlockSpec((tm, D_MODEL), row),
            pl.BlockSpec((tm, oa.shape[1]), row),
            pl.BlockSpec((tm, ob.shape[1]), row),
            pl.BlockSpec((tm, oc.shape[1]), row),
            pl.BlockSpec(wo.shape, const),
            pl.BlockSpec((1, D_MODEL), const),
        ],
        out_specs=[pl.BlockSpec((tm, D_MODEL), row), pl.BlockSpec((tm, D_MODEL), row)],
        out_shape=[jax.ShapeDtypeStruct((T, D_MODEL), F32), jax.ShapeDtypeStruct((T, D_MODEL), BF16)],
        compiler_params=_cparams(("parallel",)),
        name="oproj",
    )(x2, oa, ob, oc, wo, g)


def _ffn_kernel(h_ref, halo_ref, x1_ref, wg_ref, wu_ref, cg_ref, cu_ref, wd_ref, o_ref,
                ug_ref, uu_ref, acc_ref, *, blocks_per_seq):
    tm = h_ref.shape[0]
    hal = halo_ref.shape[0]
    nchunk = wg_ref.shape[0]
    first = (pl.program_id(0) % blocks_per_seq) == 0
    keep = jnp.where(first, 0.0, 1.0).astype(F32)
    h = h_ref[...]
    hh = halo_ref[...]
    acc_ref[...] = jnp.zeros(acc_ref.shape, F32)

    def conv(u_ref, w):
        y = w[CONV_WIDTH:CONV_WIDTH + 1, :]
        for tap in range(CONV_WIDTH):
            off = hal - (CONV_WIDTH - 1) + tap
            y = y + u_ref[pl.ds(off, tm), :] * w[tap:tap + 1, :]
        return y

    def chunk(j, carry):
        wg = wg_ref[j]
        wu = wu_ref[j]
        ug_ref[:hal, :] = _dot(hh, wg) * keep
        ug_ref[hal:, :] = _dot(h, wg)
        uu_ref[:hal, :] = _dot(hh, wu) * keep
        uu_ref[hal:, :] = _dot(h, wu)
        gate = conv(ug_ref, cg_ref[j])
        up = conv(uu_ref, cu_ref[j])
        act = gate * jax.nn.sigmoid(gate) * up
        acc_ref[...] += _dot(act.astype(BF16), wd_ref[j])
        return carry

    lax.fori_loop(0, nchunk, chunk, 0)
    o_ref[...] = x1_ref[...] + acc_ref[...]


def _ffn(h2, x1, wg, wu, cg, cu, wd, seq):
    T = x1.shape[0]
    tm = TM_FFN
    hal = BF16_SUBLANES
    tn = wg.shape[2]
    row = lambda i: (i, 0)
    c3 = lambda i: (0, 0, 0)
    return pl.pallas_call(
        functools.partial(_ffn_kernel, blocks_per_seq=seq // tm),
        grid=(T // tm,),
        in_specs=[
            pl.BlockSpec((tm, D_MODEL), row),
            pl.BlockSpec((hal, D_MODEL), lambda i: (jnp.maximum(i * (tm // hal) - 1, 0), 0)),
            pl.BlockSpec((tm, D_MODEL), row),
            pl.BlockSpec(wg.shape, c3),
            pl.BlockSpec(wu.shape, c3),
            pl.BlockSpec(cg.shape, c3),
            pl.BlockSpec(cu.shape, c3),
            pl.BlockSpec(wd.shape, c3),
        ],
        out_specs=pl.BlockSpec((tm, D_MODEL), row),
        out_shape=jax.ShapeDtypeStruct((T, D_MODEL), F32),
        scratch_shapes=[
            pltpu.VMEM((hal + tm, tn), F32),
            pltpu.VMEM((hal + tm, tn), F32),
            pltpu.VMEM((tm, D_MODEL), F32),
        ],
        compiler_params=_cparams(("parallel",)),
        name="ffn",
    )(h2, h2, x1, wg, wu, cg, cu, wd)


def _final_norm_kernel(x_ref, g_ref, o_ref):
    o_ref[...] = _rms(x_ref[...], g_ref[...])


def _final_norm(x2, g):
    T = x2.shape[0]
    tm = TM_FFN
    return pl.pallas_call(
        _final_norm_kernel,
        grid=(T // tm,),
        in_specs=[pl.BlockSpec((tm, D_MODEL), lambda i: (i, 0)),
                  pl.BlockSpec((1, D_MODEL), lambda i: (0, 0))],
        out_specs=pl.BlockSpec((tm, D_MODEL), lambda i: (i, 0)),
        out_shape=jax.ShapeDtypeStruct((T, D_MODEL), F32),
        compiler_params=_cparams(("parallel",)),
        name="final_norm",
    )(x2, g)


def _rope_tables(seq):
    def base(dim):
        inv = jnp.power(jnp.float32(ROPE_THETA), -jnp.arange(0, dim, 2, dtype=F32) / dim)
        ang = jnp.arange(seq, dtype=F32)[:, None] * inv[None, :]
        return jnp.cos(ang), jnp.sin(ang)

    def tables(dim, period):
        cos, sin = base(dim)
        half = dim // 2
        pad = period - dim
        c = jnp.concatenate([cos, cos, jnp.ones((seq, pad), F32)], axis=1)
        sn = jnp.concatenate([-sin, jnp.zeros((seq, half + pad), F32)], axis=1)
        sp = jnp.concatenate([jnp.zeros((seq, half), F32), sin, jnp.zeros((seq, pad), F32)], axis=1)
        reps = LANES // period
        return [jnp.tile(t, (1, reps)) for t in (c, sn, sp)]

    tabs = (tables(MLA_ROPE, LANES) + tables(HEAD_DIM // ROT_FRAC, HEAD_DIM)
            + tables(IDX_DIM // ROT_FRAC, IDX_DIM))
    return jnp.stack(tabs)


def _pad_cols(w, width):
    return jnp.pad(w, ((0, 0), (0, width - w.shape[1])))


def _layer_weights(w_in, w_uq, w_ukv, w_up, conv_w, conv_b, w_down):
    o = 0
    parts = {}
    for name, width in (("cq", Q_LORA), ("ckv", KV_LORA), ("kpe", MLA_ROPE),
                        ("dil", 3 * HEADS_DIL * HEAD_DIM), ("dsa", 3 * HEADS_DSA * HEAD_DIM),
                        ("qi", IDX_HEADS * IDX_DIM), ("ki", IDX_DIM), ("wi", IDX_HEADS)):
        parts[name] = w_in[:, o:o + width]
        o += width
    w1 = jnp.concatenate([
        parts["cq"], parts["ckv"], _pad_cols(parts["kpe"], LANES), parts["dil"], parts["dsa"],
        parts["qi"], jnp.tile(parts["ki"], (1, IDX_HEADS)), _pad_cols(parts["wi"], LANES)],
        axis=1).astype(BF16)
    assert w1.shape[1] == N_PROJ

    dq = MLA_NOPE + MLA_ROPE
    cols = []
    for hd in range(HEADS_MLA):
        nope = w_uq[:, dq * hd:dq * hd + MLA_NOPE]
        pe = w_uq[:, dq * hd + MLA_NOPE:dq * (hd + 1)]
        z = jnp.zeros((Q_LORA, MLA_NOPE), w_uq.dtype)
        slot = [nope, z] if hd % 2 == 0 else [z, nope]
        cols += slot + [_pad_cols(pe, LANES)]
    wuq = jnp.concatenate(cols, axis=1).astype(BF16)

    dkv = MLA_NOPE + HEAD_DIM
    kn = [w_ukv[:, dkv * hd:dkv * hd + MLA_NOPE] for hd in range(HEADS_MLA)]
    vv = [w_ukv[:, dkv * hd + MLA_NOPE:dkv * (hd + 1)] for hd in range(HEADS_MLA)]
    wukv = jnp.concatenate(kn + vv, axis=1).astype(BF16)

    nchunk = D_FF // TN_FFN
    def chunks(w):
        return w.reshape(w.shape[0], nchunk, TN_FFN).transpose(1, 0, 2)
    wg = chunks(w_up[:, :D_FF]).astype(BF16)
    wu = chunks(w_up[:, D_FF:]).astype(BF16)
    cw = jnp.concatenate([conv_w, conv_b[None, :],
                          jnp.zeros((8 - CONV_WIDTH - 1, 2 * D_FF), F32)], axis=0)
    cg = chunks(cw[:, :D_FF])
    cu = chunks(cw[:, D_FF:])
    wd = w_down.reshape(nchunk, TN_FFN, D_MODEL).astype(BF16)
    return w1, wuq, wukv, wg, wu, cg, cu, wd


def kernel(x, g_attn, w_in, g_q_lat, w_uq, g_kv_lat, w_ukv, w_o, g_ffn, w_up, conv_w, conv_b,
           w_down, g_final):
    batch, seq, _ = x.shape
    depth = w_in.shape[0]
    assert seq % DIL_BAND == 0 and seq % TM_FFN == 0 and seq % TQ == 0
    assert D_FF % TN_FFN == 0
    T = batch * seq
    tabs = _rope_tables(seq)
    bias = _dil_bias(TQ)
    xf = x.reshape(T, D_MODEL)
    for l in range(depth):
        w1, wuq, wukv, wg, wu, cg, cu, wd = _layer_weights(
            w_in[l], w_uq[l], w_ukv[l], w_up[l], conv_w[l], conv_b[l], w_down[l])
        (mq, mk, mv, dq, dk, dv, sq, sk, sv, iq, ik, iw) = _projection(
            xf, g_attn[l][None, :], w1, g_q_lat[l][None, :], wuq, g_kv_lat[l][None, :], wukv,
            tabs, seq)
        oa = _mla_attention(mq, mk, mv, batch, seq)
        ob = _dil_attention(dq, dk, dv, bias, batch, seq)
        oc = _dsa_attention(iq, iw, ik, sq, sk, sv, batch, seq)
        x1, h2 = _out_projection(xf, oa, ob, oc, w_o[l].astype(BF16), g_ffn[l][None, :])
        xf = _ffn(h2, x1, wg, wu, cg, cu, wd, seq)
    return _final_norm(xf, g_final[None, :]).reshape(batch, seq, D_MODEL)
```

```python
import functools
import math

import jax
import jax.numpy as jnp
from jax import lax
from jax.experimental import pallas as pl
from jax.experimental.pallas import tpu as pltpu

D_MODEL = 1024
HEAD_DIM = 64
HEADS_MLA = 4
HEADS_DIL = 8
HEADS_DSA = 4
Q_LORA = 256
KV_LORA = 128
MLA_NOPE = 64
MLA_ROPE = 32
DIL_PAIRS = ((128, 1), (512, 4), (2048, 16))
IDX_HEADS = 8
IDX_DIM = 32
DSA_TOPK = 256
D_FF = 2816
CONV_WIDTH = 3
ROPE_THETA = 500000.0
ROT_FRAC = 4
NORM_EPS = 1e-6

LANES = 128
BF16_SUBLANES = 16
VMEM_LIMIT = 56 * 1024 * 1024
NEG = -1e30
LOG2E = math.log2(math.e)
F32 = jnp.float32
BF16 = jnp.bfloat16

TQ = 512
TM_ROW = 256
TM_FFN = 512
TN_FFN = 256
COUNT_ROWS = 128
DIL_BAND = max(w for w, _ in DIL_PAIRS)

C_CQ = 0
C_CKV = C_CQ + Q_LORA
C_KPE = C_CKV + KV_LORA
C_DIL = C_KPE + LANES
C_DSA = C_DIL + 3 * HEADS_DIL * HEAD_DIM
C_QI = C_DSA + 3 * HEADS_DSA * HEAD_DIM
C_KI = C_QI + IDX_HEADS * IDX_DIM
C_WI = C_KI + IDX_HEADS * IDX_DIM
N_PROJ = C_WI + LANES


def _cparams(sem):
    return pltpu.CompilerParams(dimension_semantics=sem, vmem_limit_bytes=VMEM_LIMIT)


def _resident(shape, index_map):
    return pl.BlockSpec(shape, index_map, pipeline_mode=pl.Buffered(1))


def _rms(x, g):
    return x * lax.rsqrt(jnp.mean(x * x, axis=-1, keepdims=True) + NORM_EPS) * g


def _dot(a, b):
    return jnp.dot(a, b, preferred_element_type=F32)


def _dot_nt(a, b):
    return lax.dot_general(a, b, (((1,), (1,)), ((), ())), preferred_element_type=F32)


def _rope(x, tab_ref, t0, half):
    c = tab_ref[t0]
    sn = tab_ref[t0 + 1]
    sp = tab_ref[t0 + 2]
    return (x * c + pltpu.roll(x, LANES - half, axis=1) * sn
            + pltpu.roll(x, half, axis=1) * sp)


def _proj_kernel(x_ref, g_ref, w1_ref, gq_ref, wuq_ref, gkv_ref, wukv_ref, tab_ref,
                 mq_ref, mk_ref, mv_ref, dq_ref, dk_ref, dv_ref,
                 sq_ref, sk_ref, sv_ref, iq_ref, ik_ref, iw_ref):
    h = _rms(x_ref[...], g_ref[...]).astype(BF16)

    def proj(lo, width):
        return _dot(h, w1_ref[:, lo:lo + width])

    mla_scale = (MLA_NOPE + MLA_ROPE) ** -0.5 * LOG2E
    cq = _rms(proj(C_CQ, Q_LORA), gq_ref[...]).astype(BF16)
    for hd in range(HEADS_MLA):
        base = 2 * LANES * hd
        q2 = _dot(cq, wuq_ref[:, base:base + 2 * LANES])
        qp = _rope(q2[:, LANES:], tab_ref, 0, MLA_ROPE // 2)
        mq_ref[:, base:base + LANES] = (q2[:, :LANES] * mla_scale).astype(BF16)
        mq_ref[:, base + LANES:base + 2 * LANES] = (qp * mla_scale).astype(BF16)
    zkv = proj(C_CKV, KV_LORA + LANES)
    ckv = _rms(zkv[:, :KV_LORA], gkv_ref[...]).astype(BF16)
    kpe = _rope(zkv[:, KV_LORA:], tab_ref, 0, MLA_ROPE // 2).astype(BF16)
    kv = _dot(ckv, wukv_ref[...])
    for pr in range(HEADS_MLA // 2):
        mk_ref[:, 2 * LANES * pr:2 * LANES * pr + LANES] = kv[:, LANES * pr:LANES * (pr + 1)].astype(BF16)
        mk_ref[:, 2 * LANES * pr + LANES:2 * LANES * (pr + 1)] = kpe
    mv_ref[...] = kv[:, HEADS_MLA * MLA_NOPE:].astype(BF16)

    head_scale = HEAD_DIM ** -0.5 * LOG2E
    half_h = HEAD_DIM // ROT_FRAC // 2

    def roped(c0, width, t0, half, scale, o_ref):
        step = min(width, 2 * LANES)
        for c in range(0, width, step):
            z = proj(c0 + c, step)
            for off in range(0, step, LANES):
                r = _rope(z[:, off:off + LANES], tab_ref, t0, half)
                if scale is not None:
                    r = r * scale
                o_ref[:, c + off:c + off + LANES] = r.astype(BF16)

    def qkv(c0, nheads, q_ref, k_ref, v_ref):
        width = nheads * HEAD_DIM
        roped(c0, width, 3, half_h, head_scale, q_ref)
        roped(c0 + width, width, 3, half_h, None, k_ref)
        v_ref[...] = proj(c0 + 2 * width, width).astype(BF16)

    qkv(C_DIL, HEADS_DIL, dq_ref, dk_ref, dv_ref)
    qkv(C_DSA, HEADS_DSA, sq_ref, sk_ref, sv_ref)

    half_i = IDX_DIM // ROT_FRAC // 2
    roped(C_QI, IDX_HEADS * IDX_DIM, 6, half_i, None, iq_ref)
    roped(C_KI, IDX_HEADS * IDX_DIM, 6, half_i, None, ik_ref)
    iw_ref[...] = proj(C_WI, LANES) * ((IDX_HEADS * IDX_DIM) ** -0.5)


def _projection(x2, g, w1, gq, wuq, gkv, wukv, tabs, seq):
    T = x2.shape[0]
    tm = TM_ROW
    nsb = seq // tm
    row = lambda i: (i, 0)
    const = lambda i: (0, 0)
    widths = [2 * LANES * HEADS_MLA, 2 * LANES * (HEADS_MLA // 2), HEADS_MLA * HEAD_DIM,
              HEADS_DIL * HEAD_DIM, HEADS_DIL * HEAD_DIM, HEADS_DIL * HEAD_DIM,
              HEADS_DSA * HEAD_DIM, HEADS_DSA * HEAD_DIM, HEADS_DSA * HEAD_DIM,
              IDX_HEADS * IDX_DIM, IDX_HEADS * IDX_DIM]
    out_shape = [jax.ShapeDtypeStruct((T, w), BF16) for w in widths]
    out_shape.append(jax.ShapeDtypeStruct((T, LANES), F32))
    out_specs = [pl.BlockSpec((tm, w), row) for w in widths] + [pl.BlockSpec((tm, LANES), row)]
    return pl.pallas_call(
        _proj_kernel,
        grid=(T // tm,),
        in_specs=[
            pl.BlockSpec((tm, D_MODEL), row),
            pl.BlockSpec((1, D_MODEL), const),
            pl.BlockSpec(w1.shape, const),
            pl.BlockSpec((1, Q_LORA), const),
            pl.BlockSpec(wuq.shape, const),
            pl.BlockSpec((1, KV_LORA), const),
            pl.BlockSpec(wukv.shape, const),
            pl.BlockSpec((9, tm, LANES), lambda i: (0, i % nsb, 0)),
        ],
        out_specs=out_specs,
        out_shape=out_shape,
        compiler_params=_cparams(("parallel",)),
        name="proj",
    )(x2, g, w1, gq, wuq, gkv, wukv, tabs)


def _rep(x, n):
    return x if n == 1 else jnp.concatenate([x] * n, axis=1)


def _flash_tile(q_stack, k, v, nheads, mask_fn, m_ref, l_ref, acc_ref):
    tq = q_stack.shape[0] // nheads
    tk = k.shape[0]
    dv = v.shape[1]
    s_all = _dot_nt(q_stack, k)
    ps = []
    alphas = []
    for h in range(nheads):
        s = mask_fn(s_all[h * tq:(h + 1) * tq])
        m_prev = m_ref[h]
        m_new = jnp.maximum(m_prev, jnp.max(s, axis=1, keepdims=True))
        alpha = jnp.exp2(m_prev - m_new)
        p = jnp.exp2(s - _rep(m_new, tk // LANES))
        l_ref[h] = alpha * l_ref[h] + jnp.sum(p, axis=1, keepdims=True)
        m_ref[h] = m_new
        ps.append(p.astype(BF16))
        alphas.append(alpha)
    pv = _dot(jnp.concatenate(ps, axis=0), v)
    for h in range(nheads):
        acc_ref[h] = _rep(alphas[h], dv // LANES) * acc_ref[h] + pv[h * tq:(h + 1) * tq]


def _flash_init(m_ref, l_ref, acc_ref):
    m_ref[...] = jnp.full(m_ref.shape, NEG, F32)
    l_ref[...] = jnp.zeros(l_ref.shape, F32)
    acc_ref[...] = jnp.zeros(acc_ref.shape, F32)


def _flash_out(nheads, l_ref, acc_ref):
    tq, dv = acc_ref.shape[1:]
    lane = lax.broadcasted_iota(jnp.int32, (tq, dv), 1)
    out = jnp.zeros((tq, dv), F32)
    for h in range(nheads):
        out = jnp.where((lane // HEAD_DIM) == h, acc_ref[h] / _rep(l_ref[h], dv // LANES), out)
    return out


def _head_stack(q, nheads):
    lane = lax.broadcasted_iota(jnp.int32, q.shape, 1)
    zero = jnp.zeros_like(q)
    return jnp.concatenate([jnp.where((lane // HEAD_DIM) == h, q, zero) for h in range(nheads)], axis=0)


def _causal_tile(tq):
    r = lax.broadcasted_iota(jnp.int32, (tq, tq), 0)
    c = lax.broadcasted_iota(jnp.int32, (tq, tq), 1)
    return r >= c


def _mla_kernel(q_ref, k_ref, v_ref, o_ref, m_ref, l_ref, acc_ref):
    tq = q_ref.shape[0]
    i = pl.program_id(2)
    _flash_init(m_ref, l_ref, acc_ref)
    q_stack = jnp.concatenate([q_ref[:, :2 * LANES], q_ref[:, 2 * LANES:]], axis=0)

    def step(j, mask_fn):
        start = pl.multiple_of(j * tq, tq)
        _flash_tile(q_stack, k_ref[pl.ds(start, tq), :], v_ref[pl.ds(start, tq), :], 2,
                    mask_fn, m_ref, l_ref, acc_ref)

    def body(j, carry):
        step(j, lambda s: s)
        return carry

    lax.fori_loop(0, i, body, 0)
    step(i, lambda s: jnp.where(_causal_tile(tq), s, NEG))
    o_ref[...] = _flash_out(2, l_ref, acc_ref).astype(o_ref.dtype)


def _mla_attention(mq, mk, mv, batch, seq):
    T = mq.shape[0]
    tq = TQ
    nq = seq // tq
    npair = HEADS_MLA // 2
    return pl.pallas_call(
        _mla_kernel,
        grid=(batch, npair, nq),
        in_specs=[
            pl.BlockSpec((tq, 4 * LANES), lambda b, p, i: (b * nq + i, p)),
            _resident((seq, 2 * LANES), lambda b, p, i: (b, p)),
            _resident((seq, LANES), lambda b, p, i: (b, p)),
        ],
        out_specs=pl.BlockSpec((tq, LANES), lambda b, p, i: (b * nq + i, p)),
        out_shape=jax.ShapeDtypeStruct((T, HEADS_MLA * HEAD_DIM), BF16),
        scratch_shapes=[pltpu.VMEM((2, tq, LANES), F32)] * 3,
        compiler_params=_cparams(("parallel", "parallel", "arbitrary")),
        name="mla_attn",
    )(mq, mk, mv)


def _dil_bias(tq):
    nband = DIL_BAND // tq + 1
    d = (lax.broadcasted_iota(jnp.int32, (nband, tq, tq), 0) * tq
         + lax.broadcasted_iota(jnp.int32, (nband, tq, tq), 1)
         - lax.broadcasted_iota(jnp.int32, (nband, tq, tq), 2))
    cnt = jnp.zeros(d.shape, F32)
    for window, dil in DIL_PAIRS:
        ok = (d >= 0) & (d % dil == 0) & (d <= (window // dil) * dil)
        cnt = cnt + ok.astype(F32)
    return jnp.where(cnt > 0, jnp.log2(jnp.maximum(cnt, 1.0)), NEG)


def _dil_kernel(q_ref, k_ref, v_ref, bias_ref, o_ref, m_ref, l_ref, acc_ref):
    tq = q_ref.shape[0]
    nband = bias_ref.shape[0]
    i = pl.program_id(2)
    _flash_init(m_ref, l_ref, acc_ref)
    q_stack = _head_stack(q_ref[...], 2)

    def step(delta):
        start = pl.multiple_of((i - delta) * tq, tq)
        bias = bias_ref[delta]
        _flash_tile(q_stack, k_ref[pl.ds(start, tq), :], v_ref[pl.ds(start, tq), :], 2,
                    lambda s: s + bias, m_ref, l_ref, acc_ref)

    step(0)

    def body(delta, carry):
        step(delta)
        return carry

    lax.fori_loop(1, jnp.minimum(i, nband - 1) + 1, body, 0)
    o_ref[...] = _flash_out(2, l_ref, acc_ref).astype(o_ref.dtype)


def _dil_attention(dq, dk, dv, bias, batch, seq):
    T = dq.shape[0]
    tq = TQ
    nq = seq // tq
    npair = HEADS_DIL // 2
    return pl.pallas_call(
        _dil_kernel,
        grid=(batch, npair, nq),
        in_specs=[
            pl.BlockSpec((tq, LANES), lambda b, p, i: (b * nq + i, p)),
            _resident((seq, LANES), lambda b, p, i: (b, p)),
            _resident((seq, LANES), lambda b, p, i: (b, p)),
            _resident(bias.shape, lambda b, p, i: (0, 0, 0)),
        ],
        out_specs=pl.BlockSpec((tq, LANES), lambda b, p, i: (b * nq + i, p)),
        out_shape=jax.ShapeDtypeStruct((T, HEADS_DIL * HEAD_DIM), BF16),
        scratch_shapes=[pltpu.VMEM((2, tq, LANES), F32)] * 3,
        compiler_params=_cparams(("parallel", "parallel", "arbitrary")),
        name="dil_attn",
    )(dq, dk, dv, bias)


_INT_MIN = -(2 ** 31)
_KEY_NEG_INF = _INT_MIN + 0x7FFFFF


def _ordered_bits(x):
    return x ^ ((x >> 31) & 0x7FFFFFFF)


def _key_to_float(key):
    return lax.bitcast_convert_type(_ordered_bits(key), F32)


def _float_to_key(x):
    return _ordered_bits(lax.bitcast_convert_type(x, jnp.int32))


def _dsa_kernel(iq_ref, iw_ref, ik_ref, q_ref, k_ref, v_ref, o_ref,
                sc_ref, cm_ref, m_ref, l_ref, acc_ref, *, ksel):
    tq = q_ref.shape[0]
    nlt = tq // LANES
    i = pl.program_id(1)
    ntile = i + 1
    causal = _causal_tile(tq)
    neg_inf = jnp.float32(-jnp.inf)
    as_f32 = lambda v: lax.bitcast_convert_type(v, F32)
    as_i32 = lambda v: lax.bitcast_convert_type(v, jnp.int32)

    iq_stack = jnp.concatenate(
        [jnp.where((lax.broadcasted_iota(jnp.int32, iq_ref.shape, 1) // IDX_DIM) == h,
                   iq_ref[...], jnp.zeros(iq_ref.shape, BF16)) for h in range(IDX_HEADS)], axis=0)
    w = iw_ref[...]
    wcols = [w[:, h:h + 1] for h in range(IDX_HEADS)]
    cm_ref[...] = jnp.full(cm_ref.shape, neg_inf, F32)

    def score_tile(j, diag):
        kk = ik_ref[pl.ds(pl.multiple_of(j * tq, tq), tq), :]
        sc = jnp.zeros((tq, tq), F32)
        for half in range(2):
            nh = IDX_HEADS // 2
            d = _dot_nt(iq_stack[half * nh * tq:(half + 1) * nh * tq], kk)
            for hh in range(nh):
                sc = sc + jnp.maximum(d[hh * tq:(hh + 1) * tq], 0.0) * wcols[half * nh + hh]
        if diag:
            sc = jnp.where(causal, sc, neg_inf)
        sc_ref[j] = as_i32(sc)
        cm_ref[...] = jnp.maximum(cm_ref[...], sc)

    def score_body(j, carry):
        score_tile(j, False)
        return carry

    lax.fori_loop(0, i, score_body, 0)
    score_tile(i, True)

    ones8 = jnp.ones((8, LANES), BF16)

    def to_rows(x):
        return jnp.transpose(jnp.broadcast_to(x[:1], (LANES, tq)))

    def to_lanes(x):
        return jnp.transpose(jnp.broadcast_to(x, (tq, LANES)))[:8]

    def count_rows(hit, thr_rows):
        parts = []
        for r0 in range(0, tq, COUNT_ROWS):
            thr_b = thr_rows[r0:r0 + COUNT_ROWS]

            def body(j, cnt, r0=r0, thr_b=thr_b):
                words = sc_ref[j, r0:r0 + COUNT_ROWS, :]
                for t in range(nlt):
                    cnt = cnt + jnp.where(hit(words[:, t * LANES:(t + 1) * LANES], thr_b), 1, 0)
                return cnt

            parts.append(lax.fori_loop(0, ntile, body, jnp.zeros((COUNT_ROWS, LANES), jnp.int32)))
        partial = jnp.concatenate(parts, axis=0).astype(F32).astype(BF16)
        return _dot_nt(ones8, partial).astype(jnp.int32)

    def count_ge(thr):
        return count_rows(lambda words, t: as_f32(words) >= t, to_rows(thr))

    cm = cm_ref[...]
    lo = jnp.maximum(_float_to_key(to_lanes(jnp.min(cm, axis=1, keepdims=True))), _KEY_NEG_INF)
    hi = _float_to_key(to_lanes(jnp.max(cm, axis=1, keepdims=True))) + 1
    clo = jnp.full((8, tq), 2 ** 30, jnp.int32)
    chi = jnp.zeros((8, tq), jnp.int32)
    key_zero = 0
    key_tiny = 0x00800000
    c_zero = count_ge(jnp.zeros((8, tq), F32))
    c_tiny = count_ge(jnp.full((8, tq), jnp.finfo(F32).tiny, F32))
    in_zero = (c_tiny < ksel) & (c_zero >= ksel)
    above = (c_tiny >= ksel) & (lo < key_tiny)
    below = (c_zero < ksel) & (hi > key_zero)
    lo = jnp.where(in_zero, key_zero, jnp.where(above, key_tiny, lo))
    clo = jnp.where(in_zero, c_zero, jnp.where(above, c_tiny, clo))
    hi = jnp.where(in_zero, key_tiny, jnp.where(below, key_zero, hi))
    chi = jnp.where(in_zero, c_tiny, jnp.where(below, c_zero, chi))
    closed = jnp.where(in_zero, 1, 0)

    def open_rows(lo, hi, clo, closed):
        return (closed == 0) & (clo != ksel) & (hi - 1 > lo)

    def any_rows(mask):
        return jnp.max(jnp.where(mask, 1, 0))

    def sel_body(st):
        _, lo, hi, clo, chi = st
        mid = (lo >> 1) + (hi >> 1) + (lo & hi & 1)
        cnt = count_ge(_key_to_float(mid))
        ge = cnt >= ksel
        act = open_rows(lo, hi, clo, closed)
        up = act & ge
        dn = act & jnp.logical_not(ge)
        lo = jnp.where(up, mid, lo)
        clo = jnp.where(up, cnt, clo)
        hi = jnp.where(dn, mid, hi)
        chi = jnp.where(dn, cnt, chi)
        return any_rows(open_rows(lo, hi, clo, closed)), lo, hi, clo, chi

    _, lo, hi, clo, chi = lax.while_loop(
        lambda st: st[0] > 0, sel_body, (any_rows(open_rows(lo, hi, clo, closed)), lo, hi, clo, chi))
    exact = clo == ksel
    upper_b = _rep(to_rows(_key_to_float(jnp.where(exact, lo, hi))), nlt)
    lower_b = _rep(to_rows(_key_to_float(lo)), nlt)


    col = lax.broadcasted_iota(jnp.int32, (tq, tq), 1)
    big = jnp.int32(2 ** 30)

    def rank_body(j, carry):
        sc = as_f32(sc_ref[j])
        sc_ref[j] = jnp.where(sc >= upper_b, -1, jnp.where(sc >= lower_b, col + j * tq, big))
        return carry

    lax.fori_loop(0, ntile, rank_body, 0)

    ja = jnp.where(exact, -1, 0)
    jb = jnp.where(exact, 0, ntile * tq)

    def idx_body(st):
        _, ja, jb = st
        mid = (ja + jb) >> 1
        cnt = count_rows(lambda words, t: words < t, to_rows(mid.astype(F32)).astype(jnp.int32))
        act = jb - ja > 1
        hit = act & (cnt == ksel)
        dn = act & (cnt >= ksel)
        up = act & (cnt < ksel)
        jb = jnp.where(dn, mid, jb)
        ja = jnp.where(hit, mid - 1, jnp.where(up, mid, ja))
        return any_rows(jb - ja > 1), ja, jb

    _, _, jb = lax.while_loop(lambda st: st[0] > 0, idx_body, (any_rows(jb - ja > 1), ja, jb))
    bound_b = _rep(to_rows(jb.astype(F32)).astype(jnp.int32), nlt)

    _flash_init(m_ref, l_ref, acc_ref)
    q_stack = _head_stack(q_ref[...], HEADS_DSA)

    def attn_tile(j, diag):
        start = pl.multiple_of(j * tq, tq)
        sel = sc_ref[j] < bound_b
        if diag:
            sel = sel & causal
        _flash_tile(q_stack, k_ref[pl.ds(start, tq), :], v_ref[pl.ds(start, tq), :], HEADS_DSA,
                    lambda s: jnp.where(sel, s, NEG), m_ref, l_ref, acc_ref)

    def attn_body(j, carry):
        attn_tile(j, False)
        return carry

    lax.fori_loop(0, i, attn_body, 0)
    attn_tile(i, True)
    o_ref[...] = _flash_out(HEADS_DSA, l_ref, acc_ref).astype(o_ref.dtype)


def _dsa_attention(iq, iw, ik, sq, sk, sv, batch, seq):
    T = sq.shape[0]
    tq = TQ
    nq = seq // tq
    ksel = min(DSA_TOPK, seq // 4)
    assert tq >= ksel
    width = HEADS_DSA * HEAD_DIM
    qrow = lambda b, i: (b * nq + i, 0)
    full = lambda b, i: (b, 0)
    return pl.pallas_call(
        functools.partial(_dsa_kernel, ksel=ksel),
        grid=(batch, nq),
        in_specs=[
            pl.BlockSpec((tq, IDX_HEADS * IDX_DIM), qrow),
            pl.BlockSpec((tq, LANES), qrow),
            _resident((seq, IDX_HEADS * IDX_DIM), full),
            pl.BlockSpec((tq, width), qrow),
            _resident((seq, width), full),
            _resident((seq, width), full),
        ],
        out_specs=pl.BlockSpec((tq, width), qrow),
        out_shape=jax.ShapeDtypeStruct((T, width), BF16),
        scratch_shapes=[
            pltpu.VMEM((nq, tq, tq), jnp.int32),
            pltpu.VMEM((tq, tq), F32),
            pltpu.VMEM((HEADS_DSA, tq, LANES), F32),
            pltpu.VMEM((HEADS_DSA, tq, LANES), F32),
            pltpu.VMEM((HEADS_DSA, tq, width), F32),
        ],
        compiler_params=_cparams(("parallel", "arbitrary")),
        name="dsa_attn",
    )(iq, iw, ik, sq, sk, sv)


def _oproj_kernel(x_ref, oa_ref, ob_ref, oc_ref, wo_ref, g_ref, x1_ref, h_ref):
    na = HEADS_MLA * HEAD_DIM
    nb = HEADS_DIL * HEAD_DIM
    x1 = (x_ref[...] + _dot(oa_ref[...], wo_ref[:na, :])
          + _dot(ob_ref[...], wo_ref[na:na + nb, :])
          + _dot(oc_ref[...], wo_ref[na + nb:, :]))
    x1_ref[...] = x1
    h_ref[...] = _rms(x1, g_ref[...]).astype(BF16)


def _out_projection(x2, oa, ob, oc, wo, g):
    T = x2.shape[0]
    tm = TM_FFN
    row = lambda i: (i, 0)
    const = lambda i: (0, 0)
    return pl.pallas_call(
        _oproj_kernel,
        grid=(T // tm,),
        in_specs=[
            pl.BlockSpec((tm, D_MODEL), row),
            pl.BlockSpec((tm, oa.shape[1]), row),
            pl.BlockSpec((tm, ob.shape[1]), row),
            pl.BlockSpec((tm, oc.shape[1]), row),
            pl.BlockSpec(wo.shape, const),
            pl.BlockSpec((1, D_MODEL), const),
        ],
        out_specs=[pl.BlockSpec((tm, D_MODEL), row), pl.BlockSpec((tm, D_MODEL), row)],
        out_shape=[jax.ShapeDtypeStruct((T, D_MODEL), F32), jax.ShapeDtypeStruct((T, D_MODEL), BF16)],
        compiler_params=_cparams(("parallel",)),
        name="oproj",
    )(x2, oa, ob, oc, wo, g)


def _ffn_kernel(h_ref, halo_ref, x1_ref, wg_ref, wu_ref, cg_ref, cu_ref, wd_ref, o_ref,
                uga_ref, uua_ref, ugb_ref, uub_ref, acc_ref, *, blocks_per_seq):
    tm = h_ref.shape[0]
    hal = halo_ref.shape[0]
    nchunk = wg_ref.shape[0]
    first = (pl.program_id(0) % blocks_per_seq) == 0
    keep = jnp.where(first, 0.0, 1.0).astype(F32)
    h = h_ref[...]
    hh = halo_ref[...]
    acc_ref[...] = jnp.zeros(acc_ref.shape, F32)

    def conv(u_ref, w):
        y = w[CONV_WIDTH:CONV_WIDTH + 1, :]
        for tap in range(CONV_WIDTH):
            off = hal - (CONV_WIDTH - 1) + tap
            y = y + u_ref[pl.ds(off, tm), :] * w[tap:tap + 1, :]
        return y

    def up_proj(j, slot):
        ug_ref, uu_ref = slot
        wg = wg_ref[j]
        wu = wu_ref[j]
        ug_ref[:hal, :] = _dot(hh, wg) * keep
        ug_ref[hal:, :] = _dot(h, wg)
        uu_ref[:hal, :] = _dot(hh, wu) * keep
        uu_ref[hal:, :] = _dot(h, wu)

    def down_proj(j, slot):
        ug_ref, uu_ref = slot
        gate = conv(ug_ref, cg_ref[j])
        up = conv(uu_ref, cu_ref[j])
        act = gate * jax.nn.sigmoid(gate) * up
        acc_ref[...] += _dot(act.astype(BF16), wd_ref[j])

    slot_a = (uga_ref, uua_ref)
    slot_b = (ugb_ref, uub_ref)
    up_proj(0, slot_a)

    def chunk_pair(jj, carry):
        j = 2 * jj
        up_proj(j + 1, slot_b)
        down_proj(j, slot_a)
        up_proj(j + 2, slot_a)
        down_proj(j + 1, slot_b)
        return carry

    lax.fori_loop(0, (nchunk - 1) // 2, chunk_pair, 0)
    if nchunk % 2 == 0:
        up_proj(nchunk - 1, slot_b)
        down_proj(nchunk - 2, slot_a)
        down_proj(nchunk - 1, slot_b)
    else:
        down_proj(nchunk - 1, slot_a)
    o_ref[...] = x1_ref[...] + acc_ref[...]


def _ffn(h2, x1, wg, wu, cg, cu, wd, seq):
    T = x1.shape[0]
    tm = TM_FFN
    hal = BF16_SUBLANES
    tn = wg.shape[2]
    row = lambda i: (i, 0)
    c3 = lambda i: (0, 0, 0)
    return pl.pallas_call(
        functools.partial(_ffn_kernel, blocks_per_seq=seq // tm),
        grid=(T // tm,),
        in_specs=[
            pl.BlockSpec((tm, D_MODEL), row),
            pl.BlockSpec((hal, D_MODEL), lambda i: (jnp.maximum(i * (tm // hal) - 1, 0), 0)),
            pl.BlockSpec((tm, D_MODEL), row),
            pl.BlockSpec(wg.shape, c3),
            pl.BlockSpec(wu.shape, c3),
            pl.BlockSpec(cg.shape, c3),
            pl.BlockSpec(cu.shape, c3),
            pl.BlockSpec(wd.shape, c3),
        ],
        out_specs=pl.BlockSpec((tm, D_MODEL), row),
        out_shape=jax.ShapeDtypeStruct((T, D_MODEL), F32),
        scratch_shapes=[
            pltpu.VMEM((hal + tm, tn), F32),
            pltpu.VMEM((hal + tm, tn), F32),
            pltpu.VMEM((hal + tm, tn), F32),
            pltpu.VMEM((hal + tm, tn), F32),
            pltpu.VMEM((tm, D_MODEL), F32),
        ],
        compiler_params=_cparams(("parallel",)),
        name="ffn",
    )(h2, h2, x1, wg, wu, cg, cu, wd)


def _final_norm_kernel(x_ref, g_ref, o_ref):
    o_ref[...] = _rms(x_ref[...], g_ref[...])


def _final_norm(x2, g):
    T = x2.shape[0]
    tm = TM_FFN
    return pl.pallas_call(
        _final_norm_kernel,
        grid=(T // tm,),
        in_specs=[pl.BlockSpec((tm, D_MODEL), lambda i: (i, 0)),
                  pl.BlockSpec((1, D_MODEL), lambda i: (0, 0))],
        out_specs=pl.BlockSpec((tm, D_MODEL), lambda i: (i, 0)),
        out_shape=jax.ShapeDtypeStruct((T, D_MODEL), F32),
        compiler_params=_cparams(("parallel",)),
        name="final_norm",
    )(x2, g)


def _rope_tables(seq):
    def base(dim):
        inv = jnp.power(jnp.float32(ROPE_THETA), -jnp.arange(0, dim, 2, dtype=F32) / dim)
        ang = jnp.arange(seq, dtype=F32)[:, None] * inv[None, :]
        return jnp.cos(ang), jnp.sin(ang)

    def tables(dim, period):
        cos, sin = base(dim)
        half = dim // 2
        pad = period - dim
        c = jnp.concatenate([cos, cos, jnp.ones((seq, pad), F32)], axis=1)
        sn = jnp.concatenate([-sin, jnp.zeros((seq, half + pad), F32)], axis=1)
        sp = jnp.concatenate([jnp.zeros((seq, half), F32), sin, jnp.zeros((seq, pad), F32)], axis=1)
        reps = LANES // period
        return [jnp.tile(t, (1, reps)) for t in (c, sn, sp)]

    tabs = (tables(MLA_ROPE, LANES) + tables(HEAD_DIM // ROT_FRAC, HEAD_DIM)
            + tables(IDX_DIM // ROT_FRAC, IDX_DIM))
    return jnp.stack(tabs)


def _pad_cols(w, width):
    return jnp.pad(w, ((0, 0), (0, width - w.shape[1])))


def _layer_weights(w_in, w_uq, w_ukv, w_up, conv_w, conv_b, w_down):
    o = 0
    parts = {}
    for name, width in (("cq", Q_LORA), ("ckv", KV_LORA), ("kpe", MLA_ROPE),
                        ("dil", 3 * HEADS_DIL * HEAD_DIM), ("dsa", 3 * HEADS_DSA * HEAD_DIM),
                        ("qi", IDX_HEADS * IDX_DIM), ("ki", IDX_DIM), ("wi", IDX_HEADS)):
        parts[name] = w_in[:, o:o + width]
        o += width
    w1 = jnp.concatenate([
        parts["cq"], parts["ckv"], _pad_cols(parts["kpe"], LANES), parts["dil"], parts["dsa"],
        parts["qi"], jnp.tile(parts["ki"], (1, IDX_HEADS)), _pad_cols(parts["wi"], LANES)],
        axis=1).astype(BF16)
    assert w1.shape[1] == N_PROJ

    dq = MLA_NOPE + MLA_ROPE
    cols = []
    for hd in range(HEADS_MLA):
        nope = w_uq[:, dq * hd:dq * hd + MLA_NOPE]
        pe = w_uq[:, dq * hd + MLA_NOPE:dq * (hd + 1)]
        z = jnp.zeros((Q_LORA, MLA_NOPE), w_uq.dtype)
        slot = [nope, z] if hd % 2 == 0 else [z, nope]
        cols += slot + [_pad_cols(pe, LANES)]
    wuq = jnp.concatenate(cols, axis=1).astype(BF16)

    dkv = MLA_NOPE + HEAD_DIM
    kn = [w_ukv[:, dkv * hd:dkv * hd + MLA_NOPE] for hd in range(HEADS_MLA)]
    vv = [w_ukv[:, dkv * hd + MLA_NOPE:dkv * (hd + 1)] for hd in range(HEADS_MLA)]
    wukv = jnp.concatenate(kn + vv, axis=1).astype(BF16)

    nchunk = D_FF // TN_FFN
    def chunks(w):
        return w.reshape(w.shape[0], nchunk, TN_FFN).transpose(1, 0, 2)
    wg = chunks(w_up[:, :D_FF]).astype(BF16)
    wu = chunks(w_up[:, D_FF:]).astype(BF16)
    cw = jnp.concatenate([conv_w, conv_b[None, :],
                          jnp.zeros((8 - CONV_WIDTH - 1, 2 * D_FF), F32)], axis=0)
    cg = chunks(cw[:, :D_FF])
    cu = chunks(cw[:, D_FF:])
    wd = w_down.reshape(nchunk, TN_FFN, D_MODEL).astype(BF16)
    return w1, wuq, wukv, wg, wu, cg, cu, wd


def kernel(x, g_attn, w_in, g_q_lat, w_uq, g_kv_lat, w_ukv, w_o, g_ffn, w_up, conv_w, conv_b,
           w_down, g_final):
    batch, seq, _ = x.shape
    depth = w_in.shape[0]
    assert seq % DIL_BAND == 0 and seq % TM_FFN == 0 and seq % TQ == 0
    assert D_FF % TN_FFN == 0
    T = batch * seq
    tabs = _rope_tables(seq)
    bias = _dil_bias(TQ)
    xf = x.reshape(T, D_MODEL)
    for l in range(depth):
        w1, wuq, wukv, wg, wu, cg, cu, wd = _layer_weights(
            w_in[l], w_uq[l], w_ukv[l], w_up[l], conv_w[l], conv_b[l], w_down[l])
        (mq, mk, mv, dq, dk, dv, sq, sk, sv, iq, ik, iw) = _projection(
            xf, g_attn[l][None, :], w1, g_q_lat[l][None, :], wuq, g_kv_lat[l][None, :], wukv,
            tabs, seq)
        oa = _mla_attention(mq, mk, mv, batch, seq)
        ob = _dil_attention(dq, dk, dv, bias, batch, seq)
        oc = _dsa_attention(iq, iw, ik, sq, sk, sv, batch, seq)
        x1, h2 = _out_projection(xf, oa, ob, oc, w_o[l].astype(BF16), g_ffn[l][None, :])
        xf = _ffn(h2, x1, wg, wu, cg, cu, wd, seq)
    return _final_norm(xf, g_final[None, :]).reshape(batch, seq, D_MODEL)
```

```python
import functools
import math

import jax
import jax.numpy as jnp
from jax import lax
from jax.experimental import pallas as pl
from jax.experimental.pallas import tpu as pltpu

D_MODEL = 1024
HEAD_DIM = 64
HEADS_MLA = 4
HEADS_DIL = 8
HEADS_DSA = 4
Q_LORA = 256
KV_LORA = 128
MLA_NOPE = 64
MLA_ROPE = 32
DIL_PAIRS = ((128, 1), (512, 4), (2048, 16))
IDX_HEADS = 8
IDX_DIM = 32
DSA_TOPK = 256
D_FF = 2816
CONV_WIDTH = 3
ROPE_THETA = 500000.0
ROT_FRAC = 4
NORM_EPS = 1e-6

LANES = 128
BF16_SUBLANES = 16
VMEM_LIMIT = 56 * 1024 * 1024
NEG = -1e30
LOG2E = math.log2(math.e)
F32 = jnp.float32
BF16 = jnp.bfloat16

TQ = 512
TM_ROW = 256
TM_FFN = 512
TN_FFN = 256
DIL_BAND = max(w for w, _ in DIL_PAIRS)

C_CQ = 0
C_CKV = C_CQ + Q_LORA
C_KPE = C_CKV + KV_LORA
C_DIL = C_KPE + LANES
C_DSA = C_DIL + 3 * HEADS_DIL * HEAD_DIM
C_QI = C_DSA + 3 * HEADS_DSA * HEAD_DIM
C_KI = C_QI + IDX_HEADS * IDX_DIM
C_WI = C_KI + IDX_HEADS * IDX_DIM
N_PROJ = C_WI + LANES


def _cparams(sem):
    return pltpu.CompilerParams(dimension_semantics=sem, vmem_limit_bytes=VMEM_LIMIT)


def _resident(shape, index_map):
    return pl.BlockSpec(shape, index_map, pipeline_mode=pl.Buffered(1))


def _rms(x, g):
    return x * lax.rsqrt(jnp.mean(x * x, axis=-1, keepdims=True) + NORM_EPS) * g


def _dot(a, b):
    return jnp.dot(a, b, preferred_element_type=F32)


def _dot_nt(a, b):
    return lax.dot_general(a, b, (((1,), (1,)), ((), ())), preferred_element_type=F32)


def _rope(x, tab_ref, t0, half):
    c = tab_ref[t0]
    sn = tab_ref[t0 + 1]
    sp = tab_ref[t0 + 2]
    return (x * c + pltpu.roll(x, LANES - half, axis=1) * sn
            + pltpu.roll(x, half, axis=1) * sp)


def _proj_kernel(x_ref, g_ref, w1_ref, gq_ref, wuq_ref, gkv_ref, wukv_ref, tab_ref,
                 mq_ref, mk_ref, mv_ref, dq_ref, dk_ref, dv_ref,
                 sq_ref, sk_ref, sv_ref, iq_ref, ik_ref, iw_ref):
    h = _rms(x_ref[...], g_ref[...]).astype(BF16)

    def proj(lo, width):
        return _dot(h, w1_ref[:, lo:lo + width])

    mla_scale = (MLA_NOPE + MLA_ROPE) ** -0.5 * LOG2E
    cq = _rms(proj(C_CQ, Q_LORA), gq_ref[...]).astype(BF16)
    for hd in range(HEADS_MLA):
        base = 2 * LANES * hd
        q2 = _dot(cq, wuq_ref[:, base:base + 2 * LANES])
        qp = _rope(q2[:, LANES:], tab_ref, 0, MLA_ROPE // 2)
        mq_ref[:, base:base + LANES] = (q2[:, :LANES] * mla_scale).astype(BF16)
        mq_ref[:, base + LANES:base + 2 * LANES] = (qp * mla_scale).astype(BF16)
    zkv = proj(C_CKV, KV_LORA + LANES)
    ckv = _rms(zkv[:, :KV_LORA], gkv_ref[...]).astype(BF16)
    kpe = _rope(zkv[:, KV_LORA:], tab_ref, 0, MLA_ROPE // 2).astype(BF16)
    kv = _dot(ckv, wukv_ref[...])
    for pr in range(HEADS_MLA // 2):
        mk_ref[:, 2 * LANES * pr:2 * LANES * pr + LANES] = kv[:, LANES * pr:LANES * (pr + 1)].astype(BF16)
        mk_ref[:, 2 * LANES * pr + LANES:2 * LANES * (pr + 1)] = kpe
    mv_ref[...] = kv[:, HEADS_MLA * MLA_NOPE:].astype(BF16)

    head_scale = HEAD_DIM ** -0.5 * LOG2E
    half_h = HEAD_DIM // ROT_FRAC // 2

    def roped(c0, width, t0, half, scale, o_ref):
        step = min(width, 2 * LANES)
        for c in range(0, width, step):
            z = proj(c0 + c, step)
            for off in range(0, step, LANES):
                r = _rope(z[:, off:off + LANES], tab_ref, t0, half)
                if scale is not None:
                    r = r * scale
                o_ref[:, c + off:c + off + LANES] = r.astype(BF16)

    def qkv(c0, nheads, q_ref, k_ref, v_ref):
        width = nheads * HEAD_DIM
        roped(c0, width, 3, half_h, head_scale, q_ref)
        roped(c0 + width, width, 3, half_h, None, k_ref)
        v_ref[...] = proj(c0 + 2 * width, width).astype(BF16)

    qkv(C_DIL, HEADS_DIL, dq_ref, dk_ref, dv_ref)
    qkv(C_DSA, HEADS_DSA, sq_ref, sk_ref, sv_ref)

    half_i = IDX_DIM // ROT_FRAC // 2
    roped(C_QI, IDX_HEADS * IDX_DIM, 6, half_i, None, iq_ref)
    roped(C_KI, IDX_HEADS * IDX_DIM, 6, half_i, None, ik_ref)
    iw_ref[...] = proj(C_WI, LANES) * ((IDX_HEADS * IDX_DIM) ** -0.5)


def _projection(x2, g, w1, gq, wuq, gkv, wukv, tabs, seq):
    T = x2.shape[0]
    tm = TM_ROW
    nsb = seq // tm
    row = lambda i: (i, 0)
    const = lambda i: (0, 0)
    widths = [2 * LANES * HEADS_MLA, 2 * LANES * (HEADS_MLA // 2), HEADS_MLA * HEAD_DIM,
              HEADS_DIL * HEAD_DIM, HEADS_DIL * HEAD_DIM, HEADS_DIL * HEAD_DIM,
              HEADS_DSA * HEAD_DIM, HEADS_DSA * HEAD_DIM, HEADS_DSA * HEAD_DIM,
              IDX_HEADS * IDX_DIM, IDX_HEADS * IDX_DIM]
    out_shape = [jax.ShapeDtypeStruct((T, w), BF16) for w in widths]
    out_shape.append(jax.ShapeDtypeStruct((T, LANES), F32))
    out_specs = [pl.BlockSpec((tm, w), row) for w in widths] + [pl.BlockSpec((tm, LANES), row)]
    return pl.pallas_call(
        _proj_kernel,
        grid=(T // tm,),
        in_specs=[
            pl.BlockSpec((tm, D_MODEL), row),
            pl.BlockSpec((1, D_MODEL), const),
            pl.BlockSpec(w1.shape, const),
            pl.BlockSpec((1, Q_LORA), const),
            pl.BlockSpec(wuq.shape, const),
            pl.BlockSpec((1, KV_LORA), const),
            pl.BlockSpec(wukv.shape, const),
            pl.BlockSpec((9, tm, LANES), lambda i: (0, i % nsb, 0)),
        ],
        out_specs=out_specs,
        out_shape=out_shape,
        compiler_params=_cparams(("parallel",)),
        name="proj",
    )(x2, g, w1, gq, wuq, gkv, wukv, tabs)


def _rep(x, n):
    return x if n == 1 else jnp.concatenate([x] * n, axis=1)


def _flash_tile(q_stack, k, v, nheads, mask_fn, m_ref, l_ref, acc_ref):
    tq = q_stack.shape[0] // nheads
    tk = k.shape[0]
    dv = v.shape[1]
    s_all = _dot_nt(q_stack, k)
    ps = []
    alphas = []
    for h in range(nheads):
        s = mask_fn(s_all[h * tq:(h + 1) * tq])
        m_prev = m_ref[h]
        m_new = jnp.maximum(m_prev, jnp.max(s, axis=1, keepdims=True))
        alpha = jnp.exp2(m_prev - m_new)
        p = jnp.exp2(s - _rep(m_new, tk // LANES))
        l_ref[h] = alpha * l_ref[h] + jnp.sum(p, axis=1, keepdims=True)
        m_ref[h] = m_new
        ps.append(p.astype(BF16))
        alphas.append(alpha)
    pv = _dot(jnp.concatenate(ps, axis=0), v)
    for h in range(nheads):
        acc_ref[h] = _rep(alphas[h], dv // LANES) * acc_ref[h] + pv[h * tq:(h + 1) * tq]


def _flash_init(m_ref, l_ref, acc_ref):
    m_ref[...] = jnp.full(m_ref.shape, NEG, F32)
    l_ref[...] = jnp.zeros(l_ref.shape, F32)
    acc_ref[...] = jnp.zeros(acc_ref.shape, F32)


def _flash_out(nheads, l_ref, acc_ref):
    tq, dv = acc_ref.shape[1:]
    lane = lax.broadcasted_iota(jnp.int32, (tq, dv), 1)
    out = jnp.zeros((tq, dv), F32)
    for h in range(nheads):
        out = jnp.where((lane // HEAD_DIM) == h, acc_ref[h] / _rep(l_ref[h], dv // LANES), out)
    return out


def _slab(x):
    return jnp.broadcast_to(x, (8, x.shape[1]))


def _to_rows(x, width):
    tq = x.shape[1]
    return _rep(jnp.transpose(jnp.broadcast_to(x[:1], (LANES, tq))), width // LANES)


def _flash_tile_t(q_stack, k, v, nheads, mask_fn, m_ref, l_ref, acc_ref):
    tq = q_stack.shape[0] // nheads
    dv = v.shape[1]
    s_all = _dot_nt(k, q_stack)
    ps = []
    alphas = []
    for h in range(nheads):
        s = mask_fn(s_all[:, h * tq:(h + 1) * tq])
        m_prev = m_ref[h]
        m_new = jnp.maximum(m_prev, _slab(jnp.max(s, axis=0, keepdims=True)))
        alpha = jnp.exp2(m_prev - m_new)
        p = jnp.exp2(s - m_new[:1])
        l_ref[h] = alpha * l_ref[h] + _slab(jnp.sum(p, axis=0, keepdims=True))
        m_ref[h] = m_new
        ps.append(p.astype(BF16))
        alphas.append(alpha)
    pv = lax.dot_general(jnp.concatenate(ps, axis=1), v, (((0,), (0,)), ((), ())),
                         preferred_element_type=F32)
    for h in range(nheads):
        acc_ref[h] = _to_rows(alphas[h], dv) * acc_ref[h] + pv[h * tq:(h + 1) * tq]


def _flash_out_t(nheads, l_ref, acc_ref):
    tq, dv = acc_ref.shape[1:]
    lane = lax.broadcasted_iota(jnp.int32, (tq, dv), 1)
    out = jnp.zeros((tq, dv), F32)
    for h in range(nheads):
        out = jnp.where((lane // HEAD_DIM) == h, acc_ref[h] / _to_rows(l_ref[h], dv), out)
    return out


def _head_stack(q, nheads):
    lane = lax.broadcasted_iota(jnp.int32, q.shape, 1)
    zero = jnp.zeros_like(q)
    return jnp.concatenate([jnp.where((lane // HEAD_DIM) == h, q, zero) for h in range(nheads)], axis=0)


def _causal_tile(tq):
    r = lax.broadcasted_iota(jnp.int32, (tq, tq), 0)
    c = lax.broadcasted_iota(jnp.int32, (tq, tq), 1)
    return r >= c


def _mla_kernel(q_ref, k_ref, v_ref, o_ref, m_ref, l_ref, acc_ref):
    tq = q_ref.shape[0]
    i = pl.program_id(2)
    _flash_init(m_ref, l_ref, acc_ref)
    q_stack = jnp.concatenate([q_ref[:, :2 * LANES], q_ref[:, 2 * LANES:]], axis=0)

    def step(j, mask_fn):
        start = pl.multiple_of(j * tq, tq)
        _flash_tile(q_stack, k_ref[pl.ds(start, tq), :], v_ref[pl.ds(start, tq), :], 2,
                    mask_fn, m_ref, l_ref, acc_ref)

    def body(j, carry):
        step(j, lambda s: s)
        return carry

    lax.fori_loop(0, i, body, 0)
    step(i, lambda s: jnp.where(_causal_tile(tq), s, NEG))
    o_ref[...] = _flash_out(2, l_ref, acc_ref).astype(o_ref.dtype)


def _mla_attention(mq, mk, mv, batch, seq):
    T = mq.shape[0]
    tq = TQ
    nq = seq // tq
    npair = HEADS_MLA // 2
    return pl.pallas_call(
        _mla_kernel,
        grid=(batch, npair, nq),
        in_specs=[
            pl.BlockSpec((tq, 4 * LANES), lambda b, p, i: (b * nq + i, p)),
            _resident((seq, 2 * LANES), lambda b, p, i: (b, p)),
            _resident((seq, LANES), lambda b, p, i: (b, p)),
        ],
        out_specs=pl.BlockSpec((tq, LANES), lambda b, p, i: (b * nq + i, p)),
        out_shape=jax.ShapeDtypeStruct((T, HEADS_MLA * HEAD_DIM), BF16),
        scratch_shapes=[pltpu.VMEM((2, tq, LANES), F32)] * 3,
        compiler_params=_cparams(("parallel", "parallel", "arbitrary")),
        name="mla_attn",
    )(mq, mk, mv)


def _dil_bias(tq):
    nband = DIL_BAND // tq + 1
    d = (lax.broadcasted_iota(jnp.int32, (nband, tq, tq), 0) * tq
         + lax.broadcasted_iota(jnp.int32, (nband, tq, tq), 1)
         - lax.broadcasted_iota(jnp.int32, (nband, tq, tq), 2))
    cnt = jnp.zeros(d.shape, F32)
    for window, dil in DIL_PAIRS:
        ok = (d >= 0) & (d % dil == 0) & (d <= (window // dil) * dil)
        cnt = cnt + ok.astype(F32)
    return jnp.where(cnt > 0, jnp.log2(jnp.maximum(cnt, 1.0)), NEG)


def _dil_kernel(q_ref, k_ref, v_ref, bias_ref, o_ref, m_ref, l_ref, acc_ref):
    tq = q_ref.shape[0]
    nband = bias_ref.shape[0]
    i = pl.program_id(2)
    _flash_init(m_ref, l_ref, acc_ref)
    q_stack = _head_stack(q_ref[...], 2)

    def step(delta):
        start = pl.multiple_of((i - delta) * tq, tq)
        bias = bias_ref[delta]
        _flash_tile(q_stack, k_ref[pl.ds(start, tq), :], v_ref[pl.ds(start, tq), :], 2,
                    lambda s: s + bias, m_ref, l_ref, acc_ref)

    step(0)

    def body(delta, carry):
        step(delta)
        return carry

    lax.fori_loop(1, jnp.minimum(i, nband - 1) + 1, body, 0)
    o_ref[...] = _flash_out(2, l_ref, acc_ref).astype(o_ref.dtype)


def _dil_attention(dq, dk, dv, bias, batch, seq):
    T = dq.shape[0]
    tq = TQ
    nq = seq // tq
    npair = HEADS_DIL // 2
    return pl.pallas_call(
        _dil_kernel,
        grid=(batch, npair, nq),
        in_specs=[
            pl.BlockSpec((tq, LANES), lambda b, p, i: (b * nq + i, p)),
            _resident((seq, LANES), lambda b, p, i: (b, p)),
            _resident((seq, LANES), lambda b, p, i: (b, p)),
            _resident(bias.shape, lambda b, p, i: (0, 0, 0)),
        ],
        out_specs=pl.BlockSpec((tq, LANES), lambda b, p, i: (b * nq + i, p)),
        out_shape=jax.ShapeDtypeStruct((T, HEADS_DIL * HEAD_DIM), BF16),
        scratch_shapes=[pltpu.VMEM((2, tq, LANES), F32)] * 3,
        compiler_params=_cparams(("parallel", "parallel", "arbitrary")),
        name="dil_attn",
    )(dq, dk, dv, bias)


_INT_MIN = -(2 ** 31)
_KEY_NEG_INF = _INT_MIN + 0x7FFFFF


def _ordered_bits(x):
    return x ^ ((x >> 31) & 0x7FFFFFFF)


def _key_to_float(key):
    return lax.bitcast_convert_type(_ordered_bits(key), F32)


def _float_to_key(x):
    return _ordered_bits(lax.bitcast_convert_type(x, jnp.int32))


def _dsa_kernel(iq_ref, iwt_ref, ik_ref, q_ref, k_ref, v_ref, o_ref,
                sc_ref, cm_ref, m_ref, l_ref, acc_ref, *, ksel):
    tq = q_ref.shape[0]
    i = pl.program_id(1)
    ntile = i + 1
    key_row = lax.broadcasted_iota(jnp.int32, (tq, tq), 0)
    causal = key_row <= lax.broadcasted_iota(jnp.int32, (tq, tq), 1)
    neg_inf = jnp.float32(-jnp.inf)
    as_f32 = lambda v: lax.bitcast_convert_type(v, F32)
    as_i32 = lambda v: lax.bitcast_convert_type(v, jnp.int32)

    iq_stack = jnp.concatenate(
        [jnp.where((lax.broadcasted_iota(jnp.int32, iq_ref.shape, 1) // IDX_DIM) == h,
                   iq_ref[...], jnp.zeros(iq_ref.shape, BF16)) for h in range(IDX_HEADS)], axis=0)
    wt = iwt_ref[...]
    cm_ref[...] = jnp.full(cm_ref.shape, neg_inf, F32)

    def score_tile(j, diag):
        kk = ik_ref[pl.ds(pl.multiple_of(j * tq, tq), tq), :]
        sc = jnp.zeros((tq, tq), F32)
        for half in range(2):
            nh = IDX_HEADS // 2
            d = _dot_nt(kk, iq_stack[half * nh * tq:(half + 1) * nh * tq])
            for hh in range(nh):
                h = half * nh + hh
                sc = sc + jnp.maximum(d[:, hh * tq:(hh + 1) * tq], 0.0) * wt[h:h + 1, :]
        if diag:
            sc = jnp.where(causal, sc, neg_inf)
        sc_ref[j] = as_i32(sc)
        cm_ref[...] = jnp.maximum(cm_ref[...], sc)

    def score_body(j, carry):
        score_tile(j, False)
        return carry

    lax.fori_loop(0, i, score_body, 0)
    score_tile(i, True)

    def count_cols(hit, thr):
        def body(j, cnt):
            words = sc_ref[j]
            for g in range(tq // 8):
                cnt = cnt + jnp.where(hit(words[8 * g:8 * (g + 1)], thr), 1, 0)
            return cnt

        cnt = lax.fori_loop(0, ntile, body, jnp.zeros((8, tq), jnp.int32))
        return _slab(jnp.sum(cnt, axis=0, keepdims=True))

    def count_ge(thr):
        return count_cols(lambda words, t: as_f32(words) >= t, thr)

    cm = cm_ref[...]
    lo = jnp.maximum(_float_to_key(_slab(jnp.min(cm, axis=0, keepdims=True))), _KEY_NEG_INF)
    hi = _float_to_key(_slab(jnp.max(cm, axis=0, keepdims=True))) + 1
    clo = jnp.full((8, tq), 2 ** 30, jnp.int32)
    chi = jnp.zeros((8, tq), jnp.int32)
    key_zero = 0
    key_tiny = 0x00800000
    c_zero = count_ge(jnp.zeros((8, tq), F32))
    c_tiny = count_ge(jnp.full((8, tq), jnp.finfo(F32).tiny, F32))
    in_zero = (c_tiny < ksel) & (c_zero >= ksel)
    above = (c_tiny >= ksel) & (lo < key_tiny)
    below = (c_zero < ksel) & (hi > key_zero)
    lo = jnp.where(in_zero, key_zero, jnp.where(above, key_tiny, lo))
    clo = jnp.where(in_zero, c_zero, jnp.where(above, c_tiny, clo))
    hi = jnp.where(in_zero, key_tiny, jnp.where(below, key_zero, hi))
    chi = jnp.where(in_zero, c_tiny, jnp.where(below, c_zero, chi))
    closed = jnp.where(in_zero, 1, 0)

    def open_rows(lo, hi, clo, closed):
        return (closed == 0) & (clo != ksel) & (hi - 1 > lo)

    def any_rows(mask):
        return jnp.max(jnp.where(mask, 1, 0))

    def sel_body(st):
        _, lo, hi, clo, chi = st
        mid = (lo >> 1) + (hi >> 1) + (lo & hi & 1)
        cnt = count_ge(_key_to_float(mid))
        ge = cnt >= ksel
        act = open_rows(lo, hi, clo, closed)
        up = act & ge
        dn = act & jnp.logical_not(ge)
        lo = jnp.where(up, mid, lo)
        clo = jnp.where(up, cnt, clo)
        hi = jnp.where(dn, mid, hi)
        chi = jnp.where(dn, cnt, chi)
        return any_rows(open_rows(lo, hi, clo, closed)), lo, hi, clo, chi

    _, lo, hi, clo, chi = lax.while_loop(
        lambda st: st[0] > 0, sel_body, (any_rows(open_rows(lo, hi, clo, closed)), lo, hi, clo, chi))
    exact = clo == ksel
    upper = _key_to_float(jnp.where(exact, lo, hi))[:1]
    lower = _key_to_float(lo)[:1]

    big = jnp.int32(2 ** 30)

    def rank_body(j, carry):
        sc = as_f32(sc_ref[j])
        sc_ref[j] = jnp.where(sc >= upper, -1, jnp.where(sc >= lower, key_row + j * tq, big))
        return carry

    lax.fori_loop(0, ntile, rank_body, 0)

    ja = jnp.where(exact, -1, 0)
    jb = jnp.where(exact, 0, ntile * tq)

    def idx_body(st):
        _, ja, jb = st
        mid = (ja + jb) >> 1
        cnt = count_cols(lambda words, t: words < t, mid)
        act = jb - ja > 1
        hit = act & (cnt == ksel)
        dn = act & (cnt >= ksel)
        up = act & (cnt < ksel)
        jb = jnp.where(dn, mid, jb)
        ja = jnp.where(hit, mid - 1, jnp.where(up, mid, ja))
        return any_rows(jb - ja > 1), ja, jb

    _, _, jb = lax.while_loop(lambda st: st[0] > 0, idx_body, (any_rows(jb - ja > 1), ja, jb))
    bound = jb[:1]

    _flash_init(m_ref, l_ref, acc_ref)
    q_stack = _head_stack(q_ref[...], HEADS_DSA)

    def attn_tile(j, diag):
        start = pl.multiple_of(j * tq, tq)
        sel = sc_ref[j] < bound
        if diag:
            sel = sel & causal
        cap = jnp.where(sel, -NEG, NEG)
        _flash_tile_t(q_stack, k_ref[pl.ds(start, tq), :], v_ref[pl.ds(start, tq), :], HEADS_DSA,
                      lambda s: jnp.minimum(s, cap), m_ref, l_ref, acc_ref)

    def attn_body(j, carry):
        attn_tile(j, False)
        return carry

    lax.fori_loop(0, i, attn_body, 0)
    attn_tile(i, True)
    o_ref[...] = _flash_out_t(HEADS_DSA, l_ref, acc_ref).astype(o_ref.dtype)


def _dsa_attention(iq, iw, ik, sq, sk, sv, batch, seq):
    T = sq.shape[0]
    tq = TQ
    nq = seq // tq
    ksel = min(DSA_TOPK, seq // 4)
    assert tq >= ksel
    width = HEADS_DSA * HEAD_DIM
    qrow = lambda b, i: (b * nq + i, 0)
    full = lambda b, i: (b, 0)
    return pl.pallas_call(
        functools.partial(_dsa_kernel, ksel=ksel),
        grid=(batch, nq),
        in_specs=[
            pl.BlockSpec((tq, IDX_HEADS * IDX_DIM), qrow),
            pl.BlockSpec((IDX_HEADS, tq), lambda b, i: (0, b * nq + i)),
            _resident((seq, IDX_HEADS * IDX_DIM), full),
            pl.BlockSpec((tq, width), qrow),
            _resident((seq, width), full),
            _resident((seq, width), full),
        ],
        out_specs=pl.BlockSpec((tq, width), qrow),
        out_shape=jax.ShapeDtypeStruct((T, width), BF16),
        scratch_shapes=[
            pltpu.VMEM((nq, tq, tq), jnp.int32),
            pltpu.VMEM((tq, tq), F32),
            pltpu.VMEM((HEADS_DSA, 8, tq), F32),
            pltpu.VMEM((HEADS_DSA, 8, tq), F32),
            pltpu.VMEM((HEADS_DSA, tq, width), F32),
        ],
        compiler_params=_cparams(("parallel", "arbitrary")),
        name="dsa_attn",
    )(iq, jnp.transpose(iw[:, :IDX_HEADS]), ik, sq, sk, sv)


def _oproj_kernel(x_ref, oa_ref, ob_ref, oc_ref, wo_ref, g_ref, x1_ref, h_ref):
    na = HEADS_MLA * HEAD_DIM
    nb = HEADS_DIL * HEAD_DIM
    x1 = (x_ref[...] + _dot(oa_ref[...], wo_ref[:na, :])
          + _dot(ob_ref[...], wo_ref[na:na + nb, :])
          + _dot(oc_ref[...], wo_ref[na + nb:, :]))
    x1_ref[...] = x1
    h_ref[...] = _rms(x1, g_ref[...]).astype(BF16)


def _out_projection(x2, oa, ob, oc, wo, g):
    T = x2.shape[0]
    tm = TM_FFN
    row = lambda i: (i, 0)
    const = lambda i: (0, 0)
    return pl.pallas_call(
        _oproj_kernel,
        grid=(T // tm,),
        in_specs=[
            pl.BlockSpec((tm, D_MODEL), row),
            pl.BlockSpec((tm, oa.shape[1]), row),
            pl.BlockSpec((tm, ob.shape[1]), row),
            pl.BlockSpec((tm, oc.shape[1]), row),
            pl.BlockSpec(wo.shape, const),
            pl.BlockSpec((1, D_MODEL), const),
        ],
        out_specs=[pl.BlockSpec((tm, D_MODEL), row), pl.BlockSpec((tm, D_MODEL), row)],
        out_shape=[jax.ShapeDtypeStruct((T, D_MODEL), F32), jax.ShapeDtypeStruct((T, D_MODEL), BF16)],
        compiler_params=_cparams(("parallel",)),
        name="oproj",
    )(x2, oa, ob, oc, wo, g)


def _ffn_kernel(h_ref, halo_ref, x1_ref, wg_ref, wu_ref, cg_ref, cu_ref, wd_ref, o_ref,
                uga_ref, uua_ref, ugb_ref, uub_ref, acc_ref, *, blocks_per_seq):
    tm = h_ref.shape[0]
    hal = halo_ref.shape[0]
    nchunk = wg_ref.shape[0]
    first = (pl.program_id(0) % blocks_per_seq) == 0
    keep = jnp.where(first, 0.0, 1.0).astype(F32)
    h = h_ref[...]
    hh = halo_ref[...]
    acc_ref[...] = jnp.zeros(acc_ref.shape, F32)

    def conv(u_ref, w):
        y = w[CONV_WIDTH:CONV_WIDTH + 1, :]
        for tap in range(CONV_WIDTH):
            off = hal - (CONV_WIDTH - 1) + tap
            y = y + u_ref[pl.ds(off, tm), :] * w[tap:tap + 1, :]
        return y

    def up_proj(j, slot):
        ug_ref, uu_ref = slot
        wg = wg_ref[j]
        wu = wu_ref[j]
        ug_ref[:hal, :] = _dot(hh, wg) * keep
        ug_ref[hal:, :] = _dot(h, wg)
        uu_ref[:hal, :] = _dot(hh, wu) * keep
        uu_ref[hal:, :] = _dot(h, wu)

    def down_proj(j, slot):
        ug_ref, uu_ref = slot
        gate = conv(ug_ref, cg_ref[j])
        up = conv(uu_ref, cu_ref[j])
        act = gate * jax.nn.sigmoid(gate) * up
        acc_ref[...] += _dot(act.astype(BF16), wd_ref[j])

    slot_a = (uga_ref, uua_ref)
    slot_b = (ugb_ref, uub_ref)
    up_proj(0, slot_a)

    def chunk_pair(jj, carry):
        j = 2 * jj
        up_proj(j + 1, slot_b)
        down_proj(j, slot_a)
        up_proj(j + 2, slot_a)
        down_proj(j + 1, slot_b)
        return carry

    lax.fori_loop(0, (nchunk - 1) // 2, chunk_pair, 0)
    if nchunk % 2 == 0:
        up_proj(nchunk - 1, slot_b)
        down_proj(nchunk - 2, slot_a)
        down_proj(nchunk - 1, slot_b)
    else:
        down_proj(nchunk - 1, slot_a)
    o_ref[...] = x1_ref[...] + acc_ref[...]


def _ffn(h2, x1, wg, wu, cg, cu, wd, seq):
    T = x1.shape[0]
    tm = TM_FFN
    hal = BF16_SUBLANES
    tn = wg.shape[2]
    row = lambda i: (i, 0)
    c3 = lambda i: (0, 0, 0)
    return pl.pallas_call(
        functools.partial(_ffn_kernel, blocks_per_seq=seq // tm),
        grid=(T // tm,),
        in_specs=[
            pl.BlockSpec((tm, D_MODEL), row),
            pl.BlockSpec((hal, D_MODEL), lambda i: (jnp.maximum(i * (tm // hal) - 1, 0), 0)),
            pl.BlockSpec((tm, D_MODEL), row),
            pl.BlockSpec(wg.shape, c3),
            pl.BlockSpec(wu.shape, c3),
            pl.BlockSpec(cg.shape, c3),
            pl.BlockSpec(cu.shape, c3),
            pl.BlockSpec(wd.shape, c3),
        ],
        out_specs=pl.BlockSpec((tm, D_MODEL), row),
        out_shape=jax.ShapeDtypeStruct((T, D_MODEL), F32),
        scratch_shapes=[
            pltpu.VMEM((hal + tm, tn), F32),
            pltpu.VMEM((hal + tm, tn), F32),
            pltpu.VMEM((hal + tm, tn), F32),
            pltpu.VMEM((hal + tm, tn), F32),
            pltpu.VMEM((tm, D_MODEL), F32),
        ],
        compiler_params=_cparams(("parallel",)),
        name="ffn",
    )(h2, h2, x1, wg, wu, cg, cu, wd)


def _final_norm_kernel(x_ref, g_ref, o_ref):
    o_ref[...] = _rms(x_ref[...], g_ref[...])


def _final_norm(x2, g):
    T = x2.shape[0]
    tm = TM_FFN
    return pl.pallas_call(
        _final_norm_kernel,
        grid=(T // tm,),
        in_specs=[pl.BlockSpec((tm, D_MODEL), lambda i: (i, 0)),
                  pl.BlockSpec((1, D_MODEL), lambda i: (0, 0))],
        out_specs=pl.BlockSpec((tm, D_MODEL), lambda i: (i, 0)),
        out_shape=jax.ShapeDtypeStruct((T, D_MODEL), F32),
        compiler_params=_cparams(("parallel",)),
        name="final_norm",
    )(x2, g)


def _rope_tables(seq):
    def base(dim):
        inv = jnp.power(jnp.float32(ROPE_THETA), -jnp.arange(0, dim, 2, dtype=F32) / dim)
        ang = jnp.arange(seq, dtype=F32)[:, None] * inv[None, :]
        return jnp.cos(ang), jnp.sin(ang)

    def tables(dim, period):
        cos, sin = base(dim)
        half = dim // 2
        pad = period - dim
        c = jnp.concatenate([cos, cos, jnp.ones((seq, pad), F32)], axis=1)
        sn = jnp.concatenate([-sin, jnp.zeros((seq, half + pad), F32)], axis=1)
        sp = jnp.concatenate([jnp.zeros((seq, half), F32), sin, jnp.zeros((seq, pad), F32)], axis=1)
        reps = LANES // period
        return [jnp.tile(t, (1, reps)) for t in (c, sn, sp)]

    tabs = (tables(MLA_ROPE, LANES) + tables(HEAD_DIM // ROT_FRAC, HEAD_DIM)
            + tables(IDX_DIM // ROT_FRAC, IDX_DIM))
    return jnp.stack(tabs)


def _pad_cols(w, width):
    return jnp.pad(w, ((0, 0), (0, width - w.shape[1])))


def _layer_weights(w_in, w_uq, w_ukv, w_up, conv_w, conv_b, w_down):
    o = 0
    parts = {}
    for name, width in (("cq", Q_LORA), ("ckv", KV_LORA), ("kpe", MLA_ROPE),
                        ("dil", 3 * HEADS_DIL * HEAD_DIM), ("dsa", 3 * HEADS_DSA * HEAD_DIM),
                        ("qi", IDX_HEADS * IDX_DIM), ("ki", IDX_DIM), ("wi", IDX_HEADS)):
        parts[name] = w_in[:, o:o + width]
        o += width
    w1 = jnp.concatenate([
        parts["cq"], parts["ckv"], _pad_cols(parts["kpe"], LANES), parts["dil"], parts["dsa"],
        parts["qi"], jnp.tile(parts["ki"], (1, IDX_HEADS)), _pad_cols(parts["wi"], LANES)],
        axis=1).astype(BF16)
    assert w1.shape[1] == N_PROJ

    dq = MLA_NOPE + MLA_ROPE
    cols = []
    for hd in range(HEADS_MLA):
        nope = w_uq[:, dq * hd:dq * hd + MLA_NOPE]
        pe = w_uq[:, dq * hd + MLA_NOPE:dq * (hd + 1)]
        z = jnp.zeros((Q_LORA, MLA_NOPE), w_uq.dtype)
        slot = [nope, z] if hd % 2 == 0 else [z, nope]
        cols += slot + [_pad_cols(pe, LANES)]
    wuq = jnp.concatenate(cols, axis=1).astype(BF16)

    dkv = MLA_NOPE + HEAD_DIM
    kn = [w_ukv[:, dkv * hd:dkv * hd + MLA_NOPE] for hd in range(HEADS_MLA)]
    vv = [w_ukv[:, dkv * hd + MLA_NOPE:dkv * (hd + 1)] for hd in range(HEADS_MLA)]
    wukv = jnp.concatenate(kn + vv, axis=1).astype(BF16)

    nchunk = D_FF // TN_FFN
    def chunks(w):
        return w.reshape(w.shape[0], nchunk, TN_FFN).transpose(1, 0, 2)
    wg = chunks(w_up[:, :D_FF]).astype(BF16)
    wu = chunks(w_up[:, D_FF:]).astype(BF16)
    cw = jnp.concatenate([conv_w, conv_b[None, :],
                          jnp.zeros((8 - CONV_WIDTH - 1, 2 * D_FF), F32)], axis=0)
    cg = chunks(cw[:, :D_FF])
    cu = chunks(cw[:, D_FF:])
    wd = w_down.reshape(nchunk, TN_FFN, D_MODEL).astype(BF16)
    return w1, wuq, wukv, wg, wu, cg, cu, wd


def kernel(x, g_attn, w_in, g_q_lat, w_uq, g_kv_lat, w_ukv, w_o, g_ffn, w_up, conv_w, conv_b,
           w_down, g_final):
    batch, seq, _ = x.shape
    depth = w_in.shape[0]
    assert seq % DIL_BAND == 0 and seq % TM_FFN == 0 and seq % TQ == 0
    assert D_FF % TN_FFN == 0
    T = batch * seq
    tabs = _rope_tables(seq)
    bias = _dil_bias(TQ)
    xf = x.reshape(T, D_MODEL)
    for l in range(depth):
        w1, wuq, wukv, wg, wu, cg, cu, wd = _layer_weights(
            w_in[l], w_uq[l], w_ukv[l], w_up[l], conv_w[l], conv_b[l], w_down[l])
        (mq, mk, mv, dq, dk, dv, sq, sk, sv, iq, ik, iw) = _projection(
            xf, g_attn[l][None, :], w1, g_q_lat[l][None, :], wuq, g_kv_lat[l][None, :], wukv,
            tabs, seq)
        oa = _mla_attention(mq, mk, mv, batch, seq)
        ob = _dil_attention(dq, dk, dv, bias, batch, seq)
        oc = _dsa_attention(iq, iw, ik, sq, sk, sv, batch, seq)
        x1, h2 = _out_projection(xf, oa, ob, oc, w_o[l].astype(BF16), g_ffn[l][None, :])
        xf = _ffn(h2, x1, wg, wu, cg, cu, wd, seq)
    return _final_norm(xf, g_final[None, :]).reshape(batch, seq, D_MODEL)
```

```python
import functools
import math

import jax
import jax.numpy as jnp
from jax import lax
from jax.experimental import pallas as pl
from jax.experimental.pallas import tpu as pltpu

D_MODEL = 1024
HEAD_DIM = 64
HEADS_MLA = 4
HEADS_DIL = 8
HEADS_DSA = 4
Q_LORA = 256
KV_LORA = 128
MLA_NOPE = 64
MLA_ROPE = 32
DIL_PAIRS = ((128, 1), (512, 4), (2048, 16))
IDX_HEADS = 8
IDX_DIM = 32
DSA_TOPK = 256
D_FF = 2816
CONV_WIDTH = 3
ROPE_THETA = 500000.0
ROT_FRAC = 4
NORM_EPS = 1e-6

LANES = 128
BF16_SUBLANES = 16
VMEM_LIMIT = 56 * 1024 * 1024
NEG = -1e30
LOG2E = math.log2(math.e)
F32 = jnp.float32
BF16 = jnp.bfloat16

TQ = 512
TM_ROW = 256
TM_FFN = 1024
TN_FFN = 256
DIL_BAND = max(w for w, _ in DIL_PAIRS)

C_CQ = 0
C_CKV = C_CQ + Q_LORA
C_KPE = C_CKV + KV_LORA
C_DIL = C_KPE + LANES
C_DSA = C_DIL + 3 * HEADS_DIL * HEAD_DIM
C_QI = C_DSA + 3 * HEADS_DSA * HEAD_DIM
C_KI = C_QI + IDX_HEADS * IDX_DIM
C_WI = C_KI + IDX_HEADS * IDX_DIM
N_PROJ = C_WI + LANES


def _cparams(sem):
    return pltpu.CompilerParams(dimension_semantics=sem, vmem_limit_bytes=VMEM_LIMIT)


def _resident(shape, index_map):
    return pl.BlockSpec(shape, index_map, pipeline_mode=pl.Buffered(1))


def _rms(x, g):
    return x * lax.rsqrt(jnp.mean(x * x, axis=-1, keepdims=True) + NORM_EPS) * g


def _dot(a, b):
    return jnp.dot(a, b, preferred_element_type=F32)


def _dot_nt(a, b):
    return lax.dot_general(a, b, (((1,), (1,)), ((), ())), preferred_element_type=F32)


def _rope(x, tab_ref, t0, half):
    c = tab_ref[t0]
    sn = tab_ref[t0 + 1]
    sp = tab_ref[t0 + 2]
    return (x * c + pltpu.roll(x, LANES - half, axis=1) * sn
            + pltpu.roll(x, half, axis=1) * sp)


def _proj_kernel(x_ref, g_ref, w1_ref, gq_ref, wuq_ref, gkv_ref, wukv_ref, tab_ref,
                 mq_ref, mk_ref, mv_ref, dq_ref, dk_ref, dv_ref,
                 sq_ref, sk_ref, sv_ref, iq_ref, ik_ref, iw_ref):
    h = _rms(x_ref[...], g_ref[...]).astype(BF16)

    def proj(lo, width):
        return _dot(h, w1_ref[:, lo:lo + width])

    mla_scale = (MLA_NOPE + MLA_ROPE) ** -0.5 * LOG2E
    cq = _rms(proj(C_CQ, Q_LORA), gq_ref[...]).astype(BF16)
    for hd in range(HEADS_MLA):
        base = 2 * LANES * hd
        q2 = _dot(cq, wuq_ref[:, base:base + 2 * LANES])
        qp = _rope(q2[:, LANES:], tab_ref, 0, MLA_ROPE // 2)
        mq_ref[:, base:base + LANES] = (q2[:, :LANES] * mla_scale).astype(BF16)
        mq_ref[:, base + LANES:base + 2 * LANES] = (qp * mla_scale).astype(BF16)
    zkv = proj(C_CKV, KV_LORA + LANES)
    ckv = _rms(zkv[:, :KV_LORA], gkv_ref[...]).astype(BF16)
    kpe = _rope(zkv[:, KV_LORA:], tab_ref, 0, MLA_ROPE // 2).astype(BF16)
    kv = _dot(ckv, wukv_ref[...])
    for pr in range(HEADS_MLA // 2):
        mk_ref[:, 2 * LANES * pr:2 * LANES * pr + LANES] = kv[:, LANES * pr:LANES * (pr + 1)].astype(BF16)
        mk_ref[:, 2 * LANES * pr + LANES:2 * LANES * (pr + 1)] = kpe
    mv_ref[...] = kv[:, HEADS_MLA * MLA_NOPE:].astype(BF16)

    head_scale = HEAD_DIM ** -0.5 * LOG2E
    half_h = HEAD_DIM // ROT_FRAC // 2

    def roped(c0, width, t0, half, scale, o_ref):
        step = min(width, 2 * LANES)
        for c in range(0, width, step):
            z = proj(c0 + c, step)
            for off in range(0, step, LANES):
                r = _rope(z[:, off:off + LANES], tab_ref, t0, half)
                if scale is not None:
                    r = r * scale
                o_ref[:, c + off:c + off + LANES] = r.astype(BF16)

    def qkv(c0, nheads, q_ref, k_ref, v_ref):
        width = nheads * HEAD_DIM
        roped(c0, width, 3, half_h, head_scale, q_ref)
        roped(c0 + width, width, 3, half_h, None, k_ref)
        v_ref[...] = proj(c0 + 2 * width, width).astype(BF16)

    qkv(C_DIL, HEADS_DIL, dq_ref, dk_ref, dv_ref)
    qkv(C_DSA, HEADS_DSA, sq_ref, sk_ref, sv_ref)

    half_i = IDX_DIM // ROT_FRAC // 2
    roped(C_QI, IDX_HEADS * IDX_DIM, 6, half_i, None, iq_ref)
    roped(C_KI, IDX_HEADS * IDX_DIM, 6, half_i, None, ik_ref)
    iw_ref[...] = proj(C_WI, LANES) * ((IDX_HEADS * IDX_DIM) ** -0.5)


def _projection(x2, g, w1, gq, wuq, gkv, wukv, tabs, seq):
    T = x2.shape[0]
    tm = TM_ROW
    nsb = seq // tm
    row = lambda i: (i, 0)
    const = lambda i: (0, 0)
    widths = [2 * LANES * HEADS_MLA, 2 * LANES * (HEADS_MLA // 2), HEADS_MLA * HEAD_DIM,
              HEADS_DIL * HEAD_DIM, HEADS_DIL * HEAD_DIM, HEADS_DIL * HEAD_DIM,
              HEADS_DSA * HEAD_DIM, HEADS_DSA * HEAD_DIM, HEADS_DSA * HEAD_DIM,
              IDX_HEADS * IDX_DIM, IDX_HEADS * IDX_DIM]
    out_shape = [jax.ShapeDtypeStruct((T, w), BF16) for w in widths]
    out_shape.append(jax.ShapeDtypeStruct((T, LANES), F32))
    out_specs = [pl.BlockSpec((tm, w), row) for w in widths] + [pl.BlockSpec((tm, LANES), row)]
    return pl.pallas_call(
        _proj_kernel,
        grid=(T // tm,),
        in_specs=[
            pl.BlockSpec((tm, D_MODEL), row),
            pl.BlockSpec((1, D_MODEL), const),
            pl.BlockSpec(w1.shape, const),
            pl.BlockSpec((1, Q_LORA), const),
            pl.BlockSpec(wuq.shape, const),
            pl.BlockSpec((1, KV_LORA), const),
            pl.BlockSpec(wukv.shape, const),
            pl.BlockSpec((9, tm, LANES), lambda i: (0, i % nsb, 0)),
        ],
        out_specs=out_specs,
        out_shape=out_shape,
        compiler_params=_cparams(("parallel",)),
        name="proj",
    )(x2, g, w1, gq, wuq, gkv, wukv, tabs)


def _rep(x, n):
    return x if n == 1 else jnp.concatenate([x] * n, axis=1)


def _flash_tile(q_stack, k, v, nheads, mask_fn, m_ref, l_ref, acc_ref):
    tq = q_stack.shape[0] // nheads
    tk = k.shape[0]
    dv = v.shape[1]
    s_all = _dot_nt(q_stack, k)
    ps = []
    alphas = []
    for h in range(nheads):
        s = mask_fn(s_all[h * tq:(h + 1) * tq])
        m_prev = m_ref[h]
        m_new = jnp.maximum(m_prev, jnp.max(s, axis=1, keepdims=True))
        alpha = jnp.exp2(m_prev - m_new)
        p = jnp.exp2(s - _rep(m_new, tk // LANES))
        l_ref[h] = alpha * l_ref[h] + jnp.sum(p, axis=1, keepdims=True)
        m_ref[h] = m_new
        ps.append(p.astype(BF16))
        alphas.append(alpha)
    pv = _dot(jnp.concatenate(ps, axis=0), v)
    for h in range(nheads):
        acc_ref[h] = _rep(alphas[h], dv // LANES) * acc_ref[h] + pv[h * tq:(h + 1) * tq]


def _flash_init(m_ref, l_ref, acc_ref):
    m_ref[...] = jnp.full(m_ref.shape, NEG, F32)
    l_ref[...] = jnp.zeros(l_ref.shape, F32)
    acc_ref[...] = jnp.zeros(acc_ref.shape, F32)


def _flash_out(nheads, l_ref, acc_ref):
    tq, dv = acc_ref.shape[1:]
    lane = lax.broadcasted_iota(jnp.int32, (tq, dv), 1)
    out = jnp.zeros((tq, dv), F32)
    for h in range(nheads):
        out = jnp.where((lane // HEAD_DIM) == h, acc_ref[h] / _rep(l_ref[h], dv // LANES), out)
    return out


def _slab(x):
    return jnp.broadcast_to(x, (8, x.shape[1]))


def _to_rows(x, width):
    tq = x.shape[1]
    return _rep(jnp.transpose(jnp.broadcast_to(x[:1], (LANES, tq))), width // LANES)


def _flash_tile_t(q_stack, k, v, nheads, mask_fn, m_ref, l_ref, acc_ref):
    tq = q_stack.shape[0] // nheads
    dv = v.shape[1]
    s_all = _dot_nt(k, q_stack)
    ps = []
    alphas = []
    for h in range(nheads):
        s = mask_fn(s_all[:, h * tq:(h + 1) * tq])
        m_prev = m_ref[h]
        m_new = jnp.maximum(m_prev, _slab(jnp.max(s, axis=0, keepdims=True)))
        alpha = jnp.exp2(m_prev - m_new)
        p = jnp.exp2(s - m_new[:1])
        l_ref[h] = alpha * l_ref[h] + _slab(jnp.sum(p, axis=0, keepdims=True))
        m_ref[h] = m_new
        ps.append(p.astype(BF16))
        alphas.append(alpha)
    pv = lax.dot_general(jnp.concatenate(ps, axis=1), v, (((0,), (0,)), ((), ())),
                         preferred_element_type=F32)
    for h in range(nheads):
        acc_ref[h] = _to_rows(alphas[h], dv) * acc_ref[h] + pv[h * tq:(h + 1) * tq]


def _flash_out_t(nheads, l_ref, acc_ref):
    tq, dv = acc_ref.shape[1:]
    lane = lax.broadcasted_iota(jnp.int32, (tq, dv), 1)
    out = jnp.zeros((tq, dv), F32)
    for h in range(nheads):
        out = jnp.where((lane // HEAD_DIM) == h, acc_ref[h] / _to_rows(l_ref[h], dv), out)
    return out


def _head_stack(q, nheads):
    lane = lax.broadcasted_iota(jnp.int32, q.shape, 1)
    zero = jnp.zeros_like(q)
    return jnp.concatenate([jnp.where((lane // HEAD_DIM) == h, q, zero) for h in range(nheads)], axis=0)


def _causal_tile(tq):
    r = lax.broadcasted_iota(jnp.int32, (tq, tq), 0)
    c = lax.broadcasted_iota(jnp.int32, (tq, tq), 1)
    return r >= c


def _mla_kernel(q_ref, k_ref, v_ref, o_ref, m_ref, l_ref, acc_ref):
    tq = q_ref.shape[0]
    i = pl.program_id(2)
    _flash_init(m_ref, l_ref, acc_ref)
    q_stack = jnp.concatenate([q_ref[:, :2 * LANES], q_ref[:, 2 * LANES:]], axis=0)

    def step(j, mask_fn):
        start = pl.multiple_of(j * tq, tq)
        _flash_tile(q_stack, k_ref[pl.ds(start, tq), :], v_ref[pl.ds(start, tq), :], 2,
                    mask_fn, m_ref, l_ref, acc_ref)

    def body(j, carry):
        step(j, lambda s: s)
        return carry

    lax.fori_loop(0, i, body, 0)
    step(i, lambda s: jnp.where(_causal_tile(tq), s, NEG))
    o_ref[...] = _flash_out(2, l_ref, acc_ref).astype(o_ref.dtype)


def _mla_attention(mq, mk, mv, batch, seq):
    T = mq.shape[0]
    tq = TQ
    nq = seq // tq
    npair = HEADS_MLA // 2
    return pl.pallas_call(
        _mla_kernel,
        grid=(batch, npair, nq),
        in_specs=[
            pl.BlockSpec((tq, 4 * LANES), lambda b, p, i: (b * nq + i, p)),
            _resident((seq, 2 * LANES), lambda b, p, i: (b, p)),
            _resident((seq, LANES), lambda b, p, i: (b, p)),
        ],
        out_specs=pl.BlockSpec((tq, LANES), lambda b, p, i: (b * nq + i, p)),
        out_shape=jax.ShapeDtypeStruct((T, HEADS_MLA * HEAD_DIM), BF16),
        scratch_shapes=[pltpu.VMEM((2, tq, LANES), F32)] * 3,
        compiler_params=_cparams(("parallel", "parallel", "arbitrary")),
        name="mla_attn",
    )(mq, mk, mv)


def _dil_bias(tq):
    nband = DIL_BAND // tq + 1
    d = (lax.broadcasted_iota(jnp.int32, (nband, tq, tq), 0) * tq
         + lax.broadcasted_iota(jnp.int32, (nband, tq, tq), 1)
         - lax.broadcasted_iota(jnp.int32, (nband, tq, tq), 2))
    cnt = jnp.zeros(d.shape, F32)
    for window, dil in DIL_PAIRS:
        ok = (d >= 0) & (d % dil == 0) & (d <= (window // dil) * dil)
        cnt = cnt + ok.astype(F32)
    return jnp.where(cnt > 0, jnp.log2(jnp.maximum(cnt, 1.0)), NEG)


def _dil_kernel(q_ref, k_ref, v_ref, bias_ref, o_ref, m_ref, l_ref, acc_ref):
    tq = q_ref.shape[0]
    nband = bias_ref.shape[0]
    i = pl.program_id(2)
    _flash_init(m_ref, l_ref, acc_ref)
    q_stack = _head_stack(q_ref[...], 2)

    def step(delta):
        start = pl.multiple_of((i - delta) * tq, tq)
        bias = bias_ref[delta]
        _flash_tile(q_stack, k_ref[pl.ds(start, tq), :], v_ref[pl.ds(start, tq), :], 2,
                    lambda s: s + bias, m_ref, l_ref, acc_ref)

    step(0)

    def body(delta, carry):
        step(delta)
        return carry

    lax.fori_loop(1, jnp.minimum(i, nband - 1) + 1, body, 0)
    o_ref[...] = _flash_out(2, l_ref, acc_ref).astype(o_ref.dtype)


def _dil_attention(dq, dk, dv, bias, batch, seq):
    T = dq.shape[0]
    tq = TQ
    nq = seq // tq
    npair = HEADS_DIL // 2
    return pl.pallas_call(
        _dil_kernel,
        grid=(batch, npair, nq),
        in_specs=[
            pl.BlockSpec((tq, LANES), lambda b, p, i: (b * nq + i, p)),
            _resident((seq, LANES), lambda b, p, i: (b, p)),
            _resident((seq, LANES), lambda b, p, i: (b, p)),
            _resident(bias.shape, lambda b, p, i: (0, 0, 0)),
        ],
        out_specs=pl.BlockSpec((tq, LANES), lambda b, p, i: (b * nq + i, p)),
        out_shape=jax.ShapeDtypeStruct((T, HEADS_DIL * HEAD_DIM), BF16),
        scratch_shapes=[pltpu.VMEM((2, tq, LANES), F32)] * 3,
        compiler_params=_cparams(("parallel", "parallel", "arbitrary")),
        name="dil_attn",
    )(dq, dk, dv, bias)


_INT_MIN = -(2 ** 31)
_KEY_NEG_INF = _INT_MIN + 0x7FFFFF
INTERP_MIN_WIDTH = 4096
INTERP_MIN_INDEX = 32


def _ordered_bits(x):
    return x ^ ((x >> 31) & 0x7FFFFFFF)


def _key_to_float(key):
    return lax.bitcast_convert_type(_ordered_bits(key), F32)


def _float_to_key(x):
    return _ordered_bits(lax.bitcast_convert_type(x, jnp.int32))


def _dsa_kernel(iq_ref, iwt_ref, ik_ref, q_ref, k_ref, v_ref, o_ref,
                sc_ref, cm_ref, m_ref, l_ref, acc_ref, *, ksel):
    tq = q_ref.shape[0]
    i = pl.program_id(1)
    ntile = i + 1
    key_row = lax.broadcasted_iota(jnp.int32, (tq, tq), 0)
    causal = key_row <= lax.broadcasted_iota(jnp.int32, (tq, tq), 1)
    neg_inf = jnp.float32(-jnp.inf)
    as_f32 = lambda v: lax.bitcast_convert_type(v, F32)
    as_i32 = lambda v: lax.bitcast_convert_type(v, jnp.int32)

    iq_stack = jnp.concatenate(
        [jnp.where((lax.broadcasted_iota(jnp.int32, iq_ref.shape, 1) // IDX_DIM) == h,
                   iq_ref[...], jnp.zeros(iq_ref.shape, BF16)) for h in range(IDX_HEADS)], axis=0)
    wt = iwt_ref[...]
    cm_ref[...] = jnp.full(cm_ref.shape, neg_inf, F32)

    def score_tile(j, diag):
        kk = ik_ref[pl.ds(pl.multiple_of(j * tq, tq), tq), :]
        sc = jnp.zeros((tq, tq), F32)
        for half in range(2):
            nh = IDX_HEADS // 2
            d = _dot_nt(kk, iq_stack[half * nh * tq:(half + 1) * nh * tq])
            for hh in range(nh):
                h = half * nh + hh
                sc = sc + jnp.maximum(d[:, hh * tq:(hh + 1) * tq], 0.0) * wt[h:h + 1, :]
        if diag:
            sc = jnp.where(causal, sc, neg_inf)
        sc_ref[j] = as_i32(sc)
        cm_ref[...] = jnp.maximum(cm_ref[...], sc)

    def score_body(j, carry):
        score_tile(j, False)
        return carry

    lax.fori_loop(0, i, score_body, 0)
    score_tile(i, True)

    def count_cols(hit, thr):
        def body(j, cnt):
            words = sc_ref[j]
            for g in range(tq // 8):
                cnt = cnt + jnp.where(hit(words[8 * g:8 * (g + 1)], thr), 1, 0)
            return cnt

        cnt = lax.fori_loop(0, ntile, body, jnp.zeros((8, tq), jnp.int32))
        return _slab(jnp.sum(cnt, axis=0, keepdims=True))

    def count_ge(thr):
        return count_cols(lambda words, t: as_f32(words) >= t, thr)

    cm = cm_ref[...]
    lo = jnp.maximum(_float_to_key(_slab(jnp.min(cm, axis=0, keepdims=True))), _KEY_NEG_INF)
    hi = _float_to_key(_slab(jnp.max(cm, axis=0, keepdims=True))) + 1
    clo = jnp.full((8, tq), 2 ** 30, jnp.int32)
    chi = jnp.zeros((8, tq), jnp.int32)
    key_zero = 0
    key_tiny = 0x00800000
    c_zero = count_ge(jnp.zeros((8, tq), F32))
    c_tiny = count_ge(jnp.full((8, tq), jnp.finfo(F32).tiny, F32))
    in_zero = (c_tiny < ksel) & (c_zero >= ksel)
    above = (c_tiny >= ksel) & (lo < key_tiny)
    below = (c_zero < ksel) & (hi > key_zero)
    lo = jnp.where(in_zero, key_zero, jnp.where(above, key_tiny, lo))
    clo = jnp.where(in_zero, c_zero, jnp.where(above, c_tiny, clo))
    hi = jnp.where(in_zero, key_tiny, jnp.where(below, key_zero, hi))
    chi = jnp.where(in_zero, c_tiny, jnp.where(below, c_zero, chi))
    closed = jnp.where(in_zero, 1, 0)

    def open_rows(lo, hi, clo, closed):
        return (closed == 0) & (clo != ksel) & (hi - 1 > lo)

    def any_rows(mask):
        return jnp.max(jnp.where(mask, 1, 0))

    def probe_key(lo, hi, clo, chi):
        mid = (lo >> 1) + (hi >> 1) + (lo & hi & 1)
        width = hi - lo
        lo_f = _key_to_float(lo)
        frac = (clo - ksel).astype(F32) / (clo - chi).astype(F32)
        guess = _float_to_key(lo_f + (_key_to_float(hi) - lo_f) * frac)
        margin = width >> 3
        guess = jnp.minimum(jnp.maximum(guess, lo + 1 + margin), hi - 1 - margin)
        use = (clo < 2 ** 30) & (width > INTERP_MIN_WIDTH) & (lo > _KEY_NEG_INF)
        return jnp.where(use, guess, mid)

    def sel_body(st):
        _, lo, hi, clo, chi = st
        mid = probe_key(lo, hi, clo, chi)
        cnt = count_ge(_key_to_float(mid))
        ge = cnt >= ksel
        act = open_rows(lo, hi, clo, closed)
        up = act & ge
        dn = act & jnp.logical_not(ge)
        lo = jnp.where(up, mid, lo)
        clo = jnp.where(up, cnt, clo)
        hi = jnp.where(dn, mid, hi)
        chi = jnp.where(dn, cnt, chi)
        return any_rows(open_rows(lo, hi, clo, closed)), lo, hi, clo, chi

    _, lo, hi, clo, chi = lax.while_loop(
        lambda st: st[0] > 0, sel_body, (any_rows(open_rows(lo, hi, clo, closed)), lo, hi, clo, chi))
    exact = clo == ksel
    upper = _key_to_float(jnp.where(exact, lo, hi))[:1]
    lower = _key_to_float(lo)[:1]

    big = jnp.int32(2 ** 30)

    def rank_body(j, carry):
        sc = as_f32(sc_ref[j])
        sc_ref[j] = jnp.where(sc >= upper, -1, jnp.where(sc >= lower, key_row + j * tq, big))
        return carry

    lax.fori_loop(0, ntile, rank_body, 0)

    ja = jnp.where(exact, -1, 0)
    jb = jnp.where(exact, 0, ntile * tq)
    ca = jnp.where(exact, ksel, chi)
    cb = clo

    def idx_body(st):
        _, ja, jb, ca, cb = st
        width = jb - ja
        frac = (ksel - ca).astype(F32) / (cb - ca).astype(F32)
        guess = ja + (width.astype(F32) * frac).astype(jnp.int32)
        margin = width >> 3
        guess = jnp.minimum(jnp.maximum(guess, ja + 1 + margin), jb - 1 - margin)
        mid = jnp.where((cb < 2 ** 30) & (width > INTERP_MIN_INDEX), guess, (ja + jb) >> 1)
        cnt = count_cols(lambda words, t: words < t, mid)
        act = width > 1
        hit = act & (cnt == ksel)
        dn = act & (cnt >= ksel)
        up = act & (cnt < ksel)
        jb = jnp.where(dn, mid, jb)
        cb = jnp.where(dn, cnt, cb)
        ja = jnp.where(hit, mid - 1, jnp.where(up, mid, ja))
        ca = jnp.where(up, cnt, ca)
        return any_rows(jb - ja > 1), ja, jb, ca, cb

    _, _, jb, _, _ = lax.while_loop(lambda st: st[0] > 0, idx_body,
                                   (any_rows(jb - ja > 1), ja, jb, ca, cb))
    bound = jb[:1]

    _flash_init(m_ref, l_ref, acc_ref)
    q_stack = _head_stack(q_ref[...], HEADS_DSA)

    def attn_tile(j, diag):
        start = pl.multiple_of(j * tq, tq)
        sel = sc_ref[j] < bound
        if diag:
            sel = sel & causal
        cap = jnp.where(sel, -NEG, NEG)
        _flash_tile_t(q_stack, k_ref[pl.ds(start, tq), :], v_ref[pl.ds(start, tq), :], HEADS_DSA,
                      lambda s: jnp.minimum(s, cap), m_ref, l_ref, acc_ref)

    def attn_body(j, carry):
        attn_tile(j, False)
        return carry

    lax.fori_loop(0, i, attn_body, 0)
    attn_tile(i, True)
    o_ref[...] = _flash_out_t(HEADS_DSA, l_ref, acc_ref).astype(o_ref.dtype)


def _dsa_attention(iq, iw, ik, sq, sk, sv, batch, seq):
    T = sq.shape[0]
    tq = TQ
    nq = seq // tq
    ksel = min(DSA_TOPK, seq // 4)
    assert tq >= ksel
    width = HEADS_DSA * HEAD_DIM
    qrow = lambda b, i: (b * nq + i, 0)
    full = lambda b, i: (b, 0)
    return pl.pallas_call(
        functools.partial(_dsa_kernel, ksel=ksel),
        grid=(batch, nq),
        in_specs=[
            pl.BlockSpec((tq, IDX_HEADS * IDX_DIM), qrow),
            pl.BlockSpec((IDX_HEADS, tq), lambda b, i: (0, b * nq + i)),
            _resident((seq, IDX_HEADS * IDX_DIM), full),
            pl.BlockSpec((tq, width), qrow),
            _resident((seq, width), full),
            _resident((seq, width), full),
        ],
        out_specs=pl.BlockSpec((tq, width), qrow),
        out_shape=jax.ShapeDtypeStruct((T, width), BF16),
        scratch_shapes=[
            pltpu.VMEM((nq, tq, tq), jnp.int32),
            pltpu.VMEM((tq, tq), F32),
            pltpu.VMEM((HEADS_DSA, 8, tq), F32),
            pltpu.VMEM((HEADS_DSA, 8, tq), F32),
            pltpu.VMEM((HEADS_DSA, tq, width), F32),
        ],
        compiler_params=_cparams(("parallel", "arbitrary")),
        name="dsa_attn",
    )(iq, jnp.transpose(iw[:, :IDX_HEADS]), ik, sq, sk, sv)


def _oproj_kernel(x_ref, oa_ref, ob_ref, oc_ref, wo_ref, g_ref, x1_ref, h_ref):
    na = HEADS_MLA * HEAD_DIM
    nb = HEADS_DIL * HEAD_DIM
    x1 = (x_ref[...] + _dot(oa_ref[...], wo_ref[:na, :])
          + _dot(ob_ref[...], wo_ref[na:na + nb, :])
          + _dot(oc_ref[...], wo_ref[na + nb:, :]))
    x1_ref[...] = x1
    h_ref[...] = _rms(x1, g_ref[...]).astype(BF16)


def _out_projection(x2, oa, ob, oc, wo, g):
    T = x2.shape[0]
    tm = TM_FFN
    row = lambda i: (i, 0)
    const = lambda i: (0, 0)
    return pl.pallas_call(
        _oproj_kernel,
        grid=(T // tm,),
        in_specs=[
            pl.BlockSpec((tm, D_MODEL), row),
            pl.BlockSpec((tm, oa.shape[1]), row),
            pl.BlockSpec((tm, ob.shape[1]), row),
            pl.BlockSpec((tm, oc.shape[1]), row),
            pl.BlockSpec(wo.shape, const),
            pl.BlockSpec((1, D_MODEL), const),
        ],
        out_specs=[pl.BlockSpec((tm, D_MODEL), row), pl.BlockSpec((tm, D_MODEL), row)],
        out_shape=[jax.ShapeDtypeStruct((T, D_MODEL), F32), jax.ShapeDtypeStruct((T, D_MODEL), BF16)],
        compiler_params=_cparams(("parallel",)),
        name="oproj",
    )(x2, oa, ob, oc, wo, g)


def _ffn_kernel(h_ref, halo_ref, x1_ref, wg_ref, wu_ref, cg_ref, cu_ref, wd_ref, o_ref,
                uga_ref, uua_ref, ugb_ref, uub_ref, acc_ref, *, blocks_per_seq):
    tm = h_ref.shape[0]
    hal = halo_ref.shape[0]
    nchunk = wg_ref.shape[0]
    first = (pl.program_id(0) % blocks_per_seq) == 0
    keep = jnp.where(first, 0.0, 1.0).astype(F32)
    h = h_ref[...]
    hh = halo_ref[...]
    acc_ref[...] = jnp.zeros(acc_ref.shape, F32)

    def conv(u_ref, w):
        y = w[CONV_WIDTH:CONV_WIDTH + 1, :]
        for tap in range(CONV_WIDTH):
            off = hal - (CONV_WIDTH - 1) + tap
            y = y + u_ref[pl.ds(off, tm), :] * w[tap:tap + 1, :]
        return y

    def up_proj(j, slot):
        ug_ref, uu_ref = slot
        wg = wg_ref[j]
        wu = wu_ref[j]
        ug_ref[:hal, :] = _dot(hh, wg) * keep
        ug_ref[hal:, :] = _dot(h, wg)
        uu_ref[:hal, :] = _dot(hh, wu) * keep
        uu_ref[hal:, :] = _dot(h, wu)

    def down_proj(j, slot):
        ug_ref, uu_ref = slot
        gate = conv(ug_ref, cg_ref[j])
        up = conv(uu_ref, cu_ref[j])
        act = gate * jax.nn.sigmoid(gate) * up
        acc_ref[...] += _dot(act.astype(BF16), wd_ref[j])

    slot_a = (uga_ref, uua_ref)
    slot_b = (ugb_ref, uub_ref)
    up_proj(0, slot_a)

    def chunk_pair(jj, carry):
        j = 2 * jj
        up_proj(j + 1, slot_b)
        down_proj(j, slot_a)
        up_proj(j + 2, slot_a)
        down_proj(j + 1, slot_b)
        return carry

    lax.fori_loop(0, (nchunk - 1) // 2, chunk_pair, 0)
    if nchunk % 2 == 0:
        up_proj(nchunk - 1, slot_b)
        down_proj(nchunk - 2, slot_a)
        down_proj(nchunk - 1, slot_b)
    else:
        down_proj(nchunk - 1, slot_a)
    o_ref[...] = x1_ref[...] + acc_ref[...]


def _ffn(h2, x1, wg, wu, cg, cu, wd, seq):
    T = x1.shape[0]
    tm = TM_FFN
    hal = BF16_SUBLANES
    tn = wg.shape[2]
    row = lambda i: (i, 0)
    c3 = lambda i: (0, 0, 0)
    return pl.pallas_call(
        functools.partial(_ffn_kernel, blocks_per_seq=seq // tm),
        grid=(T // tm,),
        in_specs=[
            pl.BlockSpec((tm, D_MODEL), row),
            pl.BlockSpec((hal, D_MODEL), lambda i: (jnp.maximum(i * (tm // hal) - 1, 0), 0)),
            pl.BlockSpec((tm, D_MODEL), row),
            _resident(wg.shape, c3),
            _resident(wu.shape, c3),
            _resident(cg.shape, c3),
            _resident(cu.shape, c3),
            _resident(wd.shape, c3),
        ],
        out_specs=pl.BlockSpec((tm, D_MODEL), row),
        out_shape=jax.ShapeDtypeStruct((T, D_MODEL), F32),
        scratch_shapes=[
            pltpu.VMEM((hal + tm, tn), F32),
            pltpu.VMEM((hal + tm, tn), F32),
            pltpu.VMEM((hal + tm, tn), F32),
            pltpu.VMEM((hal + tm, tn), F32),
            pltpu.VMEM((tm, D_MODEL), F32),
        ],
        compiler_params=_cparams(("parallel",)),
        name="ffn",
    )(h2, h2, x1, wg, wu, cg, cu, wd)


def _final_norm_kernel(x_ref, g_ref, o_ref):
    o_ref[...] = _rms(x_ref[...], g_ref[...])


def _final_norm(x2, g):
    T = x2.shape[0]
    tm = TM_FFN
    return pl.pallas_call(
        _final_norm_kernel,
        grid=(T // tm,),
        in_specs=[pl.BlockSpec((tm, D_MODEL), lambda i: (i, 0)),
                  pl.BlockSpec((1, D_MODEL), lambda i: (0, 0))],
        out_specs=pl.BlockSpec((tm, D_MODEL), lambda i: (i, 0)),
        out_shape=jax.ShapeDtypeStruct((T, D_MODEL), F32),
        compiler_params=_cparams(("parallel",)),
        name="final_norm",
    )(x2, g)


def _rope_tables(seq):
    def base(dim):
        inv = jnp.power(jnp.float32(ROPE_THETA), -jnp.arange(0, dim, 2, dtype=F32) / dim)
        ang = jnp.arange(seq, dtype=F32)[:, None] * inv[None, :]
        return jnp.cos(ang), jnp.sin(ang)

    def tables(dim, period):
        cos, sin = base(dim)
        half = dim // 2
        pad = period - dim
        c = jnp.concatenate([cos, cos, jnp.ones((seq, pad), F32)], axis=1)
        sn = jnp.concatenate([-sin, jnp.zeros((seq, half + pad), F32)], axis=1)
        sp = jnp.concatenate([jnp.zeros((seq, half), F32), sin, jnp.zeros((seq, pad), F32)], axis=1)
        reps = LANES // period
        return [jnp.tile(t, (1, reps)) for t in (c, sn, sp)]

    tabs = (tables(MLA_ROPE, LANES) + tables(HEAD_DIM // ROT_FRAC, HEAD_DIM)
            + tables(IDX_DIM // ROT_FRAC, IDX_DIM))
    return jnp.stack(tabs)


def _pad_cols(w, width):
    return jnp.pad(w, ((0, 0), (0, width - w.shape[1])))


def _layer_weights(w_in, w_uq, w_ukv, w_up, conv_w, conv_b, w_down):
    o = 0
    parts = {}
    for name, width in (("cq", Q_LORA), ("ckv", KV_LORA), ("kpe", MLA_ROPE),
                        ("dil", 3 * HEADS_DIL * HEAD_DIM), ("dsa", 3 * HEADS_DSA * HEAD_DIM),
                        ("qi", IDX_HEADS * IDX_DIM), ("ki", IDX_DIM), ("wi", IDX_HEADS)):
        parts[name] = w_in[:, o:o + width]
        o += width
    w1 = jnp.concatenate([
        parts["cq"], parts["ckv"], _pad_cols(parts["kpe"], LANES), parts["dil"], parts["dsa"],
        parts["qi"], jnp.tile(parts["ki"], (1, IDX_HEADS)), _pad_cols(parts["wi"], LANES)],
        axis=1).astype(BF16)
    assert w1.shape[1] == N_PROJ

    dq = MLA_NOPE + MLA_ROPE
    cols = []
    for hd in range(HEADS_MLA):
        nope = w_uq[:, dq * hd:dq * hd + MLA_NOPE]
        pe = w_uq[:, dq * hd + MLA_NOPE:dq * (hd + 1)]
        z = jnp.zeros((Q_LORA, MLA_NOPE), w_uq.dtype)
        slot = [nope, z] if hd % 2 == 0 else [z, nope]
        cols += slot + [_pad_cols(pe, LANES)]
    wuq = jnp.concatenate(cols, axis=1).astype(BF16)

    dkv = MLA_NOPE + HEAD_DIM
    kn = [w_ukv[:, dkv * hd:dkv * hd + MLA_NOPE] for hd in range(HEADS_MLA)]
    vv = [w_ukv[:, dkv * hd + MLA_NOPE:dkv * (hd + 1)] for hd in range(HEADS_MLA)]
    wukv = jnp.concatenate(kn + vv, axis=1).astype(BF16)

    nchunk = D_FF // TN_FFN
    def chunks(w):
        return w.reshape(w.shape[0], nchunk, TN_FFN).transpose(1, 0, 2)
    wg = chunks(w_up[:, :D_FF]).astype(BF16)
    wu = chunks(w_up[:, D_FF:]).astype(BF16)
    cw = jnp.concatenate([conv_w, conv_b[None, :],
                          jnp.zeros((8 - CONV_WIDTH - 1, 2 * D_FF), F32)], axis=0)
    cg = chunks(cw[:, :D_FF])
    cu = chunks(cw[:, D_FF:])
    wd = w_down.reshape(nchunk, TN_FFN, D_MODEL).astype(BF16)
    return w1, wuq, wukv, wg, wu, cg, cu, wd


def kernel(x, g_attn, w_in, g_q_lat, w_uq, g_kv_lat, w_ukv, w_o, g_ffn, w_up, conv_w, conv_b,
           w_down, g_final):
    batch, seq, _ = x.shape
    depth = w_in.shape[0]
    assert seq % DIL_BAND == 0 and seq % TM_FFN == 0 and seq % TQ == 0
    assert D_FF % TN_FFN == 0
    T = batch * seq
    tabs = _rope_tables(seq)
    bias = _dil_bias(TQ)
    xf = x.reshape(T, D_MODEL)
    for l in range(depth):
        w1, wuq, wukv, wg, wu, cg, cu, wd = _layer_weights(
            w_in[l], w_uq[l], w_ukv[l], w_up[l], conv_w[l], conv_b[l], w_down[l])
        (mq, mk, mv, dq, dk, dv, sq, sk, sv, iq, ik, iw) = _projection(
            xf, g_attn[l][None, :], w1, g_q_lat[l][None, :], wuq, g_kv_lat[l][None, :], wukv,
            tabs, seq)
        oa = _mla_attention(mq, mk, mv, batch, seq)
        ob = _dil_attention(dq, dk, dv, bias, batch, seq)
        oc = _dsa_attention(iq, iw, ik, sq, sk, sv, batch, seq)
        x1, h2 = _out_projection(xf, oa, ob, oc, w_o[l].astype(BF16), g_ffn[l][None, :])
        xf = _ffn(h2, x1, wg, wu, cg, cu, wd, seq)
    return _final_norm(xf, g_final[None, :]).reshape(batch, seq, D_MODEL)
```

```python
import functools
import math

import jax
import jax.numpy as jnp
from jax import lax
from jax.experimental import pallas as pl
from jax.experimental.pallas import tpu as pltpu

D_MODEL = 1024
HEAD_DIM = 64
HEADS_MLA = 4
HEADS_DIL = 8
HEADS_DSA = 4
Q_LORA = 256
KV_LORA = 128
MLA_NOPE = 64
MLA_ROPE = 32
DIL_PAIRS = ((128, 1), (512, 4), (2048, 16))
IDX_HEADS = 8
IDX_DIM = 32
DSA_TOPK = 256
D_FF = 2816
CONV_WIDTH = 3
ROPE_THETA = 500000.0
ROT_FRAC = 4
NORM_EPS = 1e-6

LANES = 128
BF16_SUBLANES = 16
VMEM_LIMIT = 56 * 1024 * 1024
NEG = -1e30
LOG2E = math.log2(math.e)
F32 = jnp.float32
BF16 = jnp.bfloat16

TQ = 512
TM_ROW = 512
TM_FFN = 1024
TN_FFN = 256
DIL_BAND = max(w for w, _ in DIL_PAIRS)

C_CQ = 0
C_CKV = C_CQ + Q_LORA
C_KPE = C_CKV + KV_LORA
C_DIL = C_KPE + LANES
C_DSA = C_DIL + 3 * HEADS_DIL * HEAD_DIM
C_QI = C_DSA + 3 * HEADS_DSA * HEAD_DIM
C_KI = C_QI + IDX_HEADS * IDX_DIM
C_WI = C_KI + IDX_HEADS * IDX_DIM
N_PROJ = C_WI + LANES


def _cparams(sem):
    return pltpu.CompilerParams(dimension_semantics=sem, vmem_limit_bytes=VMEM_LIMIT)


def _resident(shape, index_map):
    return pl.BlockSpec(shape, index_map, pipeline_mode=pl.Buffered(1))


def _rms(x, g):
    return x * lax.rsqrt(jnp.mean(x * x, axis=-1, keepdims=True) + NORM_EPS) * g


def _dot(a, b):
    return jnp.dot(a, b, preferred_element_type=F32)


def _dot_nt(a, b):
    return lax.dot_general(a, b, (((1,), (1,)), ((), ())), preferred_element_type=F32)


def _rope(x, tab_ref, t0, half):
    c = tab_ref[t0]
    sn = tab_ref[t0 + 1]
    sp = tab_ref[t0 + 2]
    return (x * c + pltpu.roll(x, LANES - half, axis=1) * sn
            + pltpu.roll(x, half, axis=1) * sp)


def _proj_kernel(x_ref, g_ref, w1_ref, gq_ref, wuq_ref, gkv_ref, wukv_ref, tab_ref,
                 mq_ref, mk_ref, mv_ref, dq_ref, dk_ref, dv_ref,
                 sq_ref, sk_ref, sv_ref, iq_ref, ik_ref, iw_ref):
    h = _rms(x_ref[...], g_ref[...]).astype(BF16)

    def proj(lo, width):
        return _dot(h, w1_ref[:, lo:lo + width])

    mla_scale = (MLA_NOPE + MLA_ROPE) ** -0.5 * LOG2E
    cq = _rms(proj(C_CQ, Q_LORA), gq_ref[...]).astype(BF16)
    for hd in range(HEADS_MLA):
        base = 2 * LANES * hd
        q2 = _dot(cq, wuq_ref[:, base:base + 2 * LANES])
        qp = _rope(q2[:, LANES:], tab_ref, 0, MLA_ROPE // 2)
        mq_ref[:, base:base + LANES] = (q2[:, :LANES] * mla_scale).astype(BF16)
        mq_ref[:, base + LANES:base + 2 * LANES] = (qp * mla_scale).astype(BF16)
    zkv = proj(C_CKV, KV_LORA + LANES)
    ckv = _rms(zkv[:, :KV_LORA], gkv_ref[...]).astype(BF16)
    kpe = _rope(zkv[:, KV_LORA:], tab_ref, 0, MLA_ROPE // 2).astype(BF16)
    kv = _dot(ckv, wukv_ref[...])
    for pr in range(HEADS_MLA // 2):
        mk_ref[:, 2 * LANES * pr:2 * LANES * pr + LANES] = kv[:, LANES * pr:LANES * (pr + 1)].astype(BF16)
        mk_ref[:, 2 * LANES * pr + LANES:2 * LANES * (pr + 1)] = kpe
    mv_ref[...] = kv[:, HEADS_MLA * MLA_NOPE:].astype(BF16)

    head_scale = HEAD_DIM ** -0.5 * LOG2E
    half_h = HEAD_DIM // ROT_FRAC // 2

    def roped(c0, width, t0, half, scale, o_ref):
        step = min(width, 2 * LANES)
        for c in range(0, width, step):
            z = proj(c0 + c, step)
            for off in range(0, step, LANES):
                r = _rope(z[:, off:off + LANES], tab_ref, t0, half)
                if scale is not None:
                    r = r * scale
                o_ref[:, c + off:c + off + LANES] = r.astype(BF16)

    def qkv(c0, nheads, q_ref, k_ref, v_ref):
        width = nheads * HEAD_DIM
        roped(c0, width, 3, half_h, head_scale, q_ref)
        roped(c0 + width, width, 3, half_h, None, k_ref)
        v_ref[...] = proj(c0 + 2 * width, width).astype(BF16)

    qkv(C_DIL, HEADS_DIL, dq_ref, dk_ref, dv_ref)
    qkv(C_DSA, HEADS_DSA, sq_ref, sk_ref, sv_ref)

    half_i = IDX_DIM // ROT_FRAC // 2
    roped(C_QI, IDX_HEADS * IDX_DIM, 6, half_i, None, iq_ref)
    roped(C_KI, IDX_HEADS * IDX_DIM, 6, half_i, None, ik_ref)
    iw_ref[...] = proj(C_WI, LANES) * ((IDX_HEADS * IDX_DIM) ** -0.5)


def _projection(x2, g, w1, gq, wuq, gkv, wukv, tabs, seq):
    T = x2.shape[0]
    tm = TM_ROW
    nsb = seq // tm
    row = lambda i: (i, 0)
    const = lambda i: (0, 0)
    widths = [2 * LANES * HEADS_MLA, 2 * LANES * (HEADS_MLA // 2), HEADS_MLA * HEAD_DIM,
              HEADS_DIL * HEAD_DIM, HEADS_DIL * HEAD_DIM, HEADS_DIL * HEAD_DIM,
              HEADS_DSA * HEAD_DIM, HEADS_DSA * HEAD_DIM, HEADS_DSA * HEAD_DIM,
              IDX_HEADS * IDX_DIM, IDX_HEADS * IDX_DIM]
    out_shape = [jax.ShapeDtypeStruct((T, w), BF16) for w in widths]
    out_shape.append(jax.ShapeDtypeStruct((T, LANES), F32))
    out_specs = [pl.BlockSpec((tm, w), row) for w in widths] + [pl.BlockSpec((tm, LANES), row)]
    return pl.pallas_call(
        _proj_kernel,
        grid=(T // tm,),
        in_specs=[
            pl.BlockSpec((tm, D_MODEL), row),
            pl.BlockSpec((1, D_MODEL), const),
            pl.BlockSpec(w1.shape, const),
            pl.BlockSpec((1, Q_LORA), const),
            pl.BlockSpec(wuq.shape, const),
            pl.BlockSpec((1, KV_LORA), const),
            pl.BlockSpec(wukv.shape, const),
            pl.BlockSpec((9, tm, LANES), lambda i: (0, i % nsb, 0)),
        ],
        out_specs=out_specs,
        out_shape=out_shape,
        compiler_params=_cparams(("parallel",)),
        name="proj",
    )(x2, g, w1, gq, wuq, gkv, wukv, tabs)


def _rep(x, n):
    return x if n == 1 else jnp.concatenate([x] * n, axis=1)


def _flash_tile(q_stack, k, v, nheads, mask_fn, m_ref, l_ref, acc_ref):
    tq = q_stack.shape[0] // nheads
    tk = k.shape[0]
    dv = v.shape[1]
    s_all = _dot_nt(q_stack, k)
    ps = []
    alphas = []
    for h in range(nheads):
        s = mask_fn(s_all[h * tq:(h + 1) * tq])
        m_prev = m_ref[h]
        m_new = jnp.maximum(m_prev, jnp.max(s, axis=1, keepdims=True))
        alpha = jnp.exp2(m_prev - m_new)
        p = jnp.exp2(s - _rep(m_new, tk // LANES))
        l_ref[h] = alpha * l_ref[h] + jnp.sum(p, axis=1, keepdims=True)
        m_ref[h] = m_new
        ps.append(p.astype(BF16))
        alphas.append(alpha)
    pv = _dot(jnp.concatenate(ps, axis=0), v)
    for h in range(nheads):
        acc_ref[h] = _rep(alphas[h], dv // LANES) * acc_ref[h] + pv[h * tq:(h + 1) * tq]


def _flash_init(m_ref, l_ref, acc_ref):
    m_ref[...] = jnp.full(m_ref.shape, NEG, F32)
    l_ref[...] = jnp.zeros(l_ref.shape, F32)
    acc_ref[...] = jnp.zeros(acc_ref.shape, F32)


def _flash_out(nheads, l_ref, acc_ref):
    tq, dv = acc_ref.shape[1:]
    lane = lax.broadcasted_iota(jnp.int32, (tq, dv), 1)
    out = jnp.zeros((tq, dv), F32)
    for h in range(nheads):
        out = jnp.where((lane // HEAD_DIM) == h, acc_ref[h] / _rep(l_ref[h], dv // LANES), out)
    return out


def _slab(x):
    return jnp.broadcast_to(x, (8, x.shape[1]))


def _to_rows(x, width):
    tq = x.shape[1]
    return _rep(jnp.transpose(jnp.broadcast_to(x[:1], (LANES, tq))), width // LANES)


def _flash_tile_t(q_stack, k, v, nheads, mask_fn, m_ref, l_ref, acc_ref):
    tq = q_stack.shape[0] // nheads
    dv = v.shape[1]
    s_all = _dot_nt(k, q_stack)
    ps = []
    alphas = []
    for h in range(nheads):
        s = mask_fn(s_all[:, h * tq:(h + 1) * tq])
        m_prev = m_ref[h]
        m_new = jnp.maximum(m_prev, _slab(jnp.max(s, axis=0, keepdims=True)))
        alpha = jnp.exp2(m_prev - m_new)
        p = jnp.exp2(s - m_new[:1])
        l_ref[h] = alpha * l_ref[h] + _slab(jnp.sum(p, axis=0, keepdims=True))
        m_ref[h] = m_new
        ps.append(p.astype(BF16))
        alphas.append(alpha)
    pv = lax.dot_general(jnp.concatenate(ps, axis=1), v, (((0,), (0,)), ((), ())),
                         preferred_element_type=F32)
    for h in range(nheads):
        acc_ref[h] = _to_rows(alphas[h], dv) * acc_ref[h] + pv[h * tq:(h + 1) * tq]


def _flash_out_t(nheads, l_ref, acc_ref):
    tq, dv = acc_ref.shape[1:]
    lane = lax.broadcasted_iota(jnp.int32, (tq, dv), 1)
    out = jnp.zeros((tq, dv), F32)
    for h in range(nheads):
        out = jnp.where((lane // HEAD_DIM) == h, acc_ref[h] / _to_rows(l_ref[h], dv), out)
    return out


def _head_stack(q, nheads):
    lane = lax.broadcasted_iota(jnp.int32, q.shape, 1)
    zero = jnp.zeros_like(q)
    return jnp.concatenate([jnp.where((lane // HEAD_DIM) == h, q, zero) for h in range(nheads)], axis=0)


def _causal_tile(tq):
    r = lax.broadcasted_iota(jnp.int32, (tq, tq), 0)
    c = lax.broadcasted_iota(jnp.int32, (tq, tq), 1)
    return r >= c


def _mla_kernel(q_ref, k_ref, v_ref, o_ref, m_ref, l_ref, acc_ref):
    tq = q_ref.shape[0]
    i = pl.program_id(2)
    _flash_init(m_ref, l_ref, acc_ref)
    q_stack = jnp.concatenate([q_ref[:, :2 * LANES], q_ref[:, 2 * LANES:]], axis=0)

    def step(j, mask_fn):
        start = pl.multiple_of(j * tq, tq)
        _flash_tile(q_stack, k_ref[pl.ds(start, tq), :], v_ref[pl.ds(start, tq), :], 2,
                    mask_fn, m_ref, l_ref, acc_ref)

    def body(j, carry):
        step(j, lambda s: s)
        return carry

    lax.fori_loop(0, i, body, 0)
    step(i, lambda s: jnp.where(_causal_tile(tq), s, NEG))
    o_ref[...] = _flash_out(2, l_ref, acc_ref).astype(o_ref.dtype)


def _mla_attention(mq, mk, mv, batch, seq):
    T = mq.shape[0]
    tq = TQ
    nq = seq // tq
    npair = HEADS_MLA // 2
    return pl.pallas_call(
        _mla_kernel,
        grid=(batch, npair, nq),
        in_specs=[
            pl.BlockSpec((tq, 4 * LANES), lambda b, p, i: (b * nq + i, p)),
            _resident((seq, 2 * LANES), lambda b, p, i: (b, p)),
            _resident((seq, LANES), lambda b, p, i: (b, p)),
        ],
        out_specs=pl.BlockSpec((tq, LANES), lambda b, p, i: (b * nq + i, p)),
        out_shape=jax.ShapeDtypeStruct((T, HEADS_MLA * HEAD_DIM), BF16),
        scratch_shapes=[pltpu.VMEM((2, tq, LANES), F32)] * 3,
        compiler_params=_cparams(("parallel", "parallel", "arbitrary")),
        name="mla_attn",
    )(mq, mk, mv)


def _dil_bias(tq):
    nband = DIL_BAND // tq + 1
    d = (lax.broadcasted_iota(jnp.int32, (nband, tq, tq), 0) * tq
         + lax.broadcasted_iota(jnp.int32, (nband, tq, tq), 1)
         - lax.broadcasted_iota(jnp.int32, (nband, tq, tq), 2))
    cnt = jnp.zeros(d.shape, F32)
    for window, dil in DIL_PAIRS:
        ok = (d >= 0) & (d % dil == 0) & (d <= (window // dil) * dil)
        cnt = cnt + ok.astype(F32)
    return jnp.where(cnt > 0, jnp.log2(jnp.maximum(cnt, 1.0)), NEG)


def _dil_kernel(q_ref, k_ref, v_ref, bias_ref, o_ref, m_ref, l_ref, acc_ref):
    tq = q_ref.shape[0]
    nband = bias_ref.shape[0]
    i = pl.program_id(2)
    _flash_init(m_ref, l_ref, acc_ref)
    q_stack = _head_stack(q_ref[...], 2)

    def step(delta):
        start = pl.multiple_of((i - delta) * tq, tq)
        bias = bias_ref[delta]
        _flash_tile(q_stack, k_ref[pl.ds(start, tq), :], v_ref[pl.ds(start, tq), :], 2,
                    lambda s: s + bias, m_ref, l_ref, acc_ref)

    step(0)

    def body(delta, carry):
        step(delta)
        return carry

    lax.fori_loop(1, jnp.minimum(i, nband - 1) + 1, body, 0)
    o_ref[...] = _flash_out(2, l_ref, acc_ref).astype(o_ref.dtype)


def _dil_attention(dq, dk, dv, bias, batch, seq):
    T = dq.shape[0]
    tq = TQ
    nq = seq // tq
    npair = HEADS_DIL // 2
    return pl.pallas_call(
        _dil_kernel,
        grid=(batch, npair, nq),
        in_specs=[
            pl.BlockSpec((tq, LANES), lambda b, p, i: (b * nq + i, p)),
            _resident((seq, LANES), lambda b, p, i: (b, p)),
            _resident((seq, LANES), lambda b, p, i: (b, p)),
            _resident(bias.shape, lambda b, p, i: (0, 0, 0)),
        ],
        out_specs=pl.BlockSpec((tq, LANES), lambda b, p, i: (b * nq + i, p)),
        out_shape=jax.ShapeDtypeStruct((T, HEADS_DIL * HEAD_DIM), BF16),
        scratch_shapes=[pltpu.VMEM((2, tq, LANES), F32)] * 3,
        compiler_params=_cparams(("parallel", "parallel", "arbitrary")),
        name="dil_attn",
    )(dq, dk, dv, bias)


_INT_MIN = -(2 ** 31)
_KEY_NEG_INF = _INT_MIN + 0x7FFFFF
INTERP_MIN_WIDTH = 4096
INTERP_MIN_INDEX = 32


def _ordered_bits(x):
    return x ^ ((x >> 31) & 0x7FFFFFFF)


def _key_to_float(key):
    return lax.bitcast_convert_type(_ordered_bits(key), F32)


def _float_to_key(x):
    return _ordered_bits(lax.bitcast_convert_type(x, jnp.int32))


def _dsa_kernel(iq_ref, iwt_ref, ik_ref, q_ref, k_ref, v_ref, o_ref,
                sc_ref, cm_ref, m_ref, l_ref, acc_ref, *, ksel):
    tq = q_ref.shape[0]
    i = pl.program_id(1)
    ntile = i + 1
    key_row = lax.broadcasted_iota(jnp.int32, (tq, tq), 0)
    causal = key_row <= lax.broadcasted_iota(jnp.int32, (tq, tq), 1)
    neg_inf = jnp.float32(-jnp.inf)
    as_f32 = lambda v: lax.bitcast_convert_type(v, F32)
    as_i32 = lambda v: lax.bitcast_convert_type(v, jnp.int32)

    iq_stack = jnp.concatenate(
        [jnp.where((lax.broadcasted_iota(jnp.int32, iq_ref.shape, 1) // IDX_DIM) == h,
                   iq_ref[...], jnp.zeros(iq_ref.shape, BF16)) for h in range(IDX_HEADS)], axis=0)
    wt = iwt_ref[...]
    cm_ref[...] = jnp.full(cm_ref.shape, neg_inf, F32)

    def score_tile(j, diag):
        kk = ik_ref[pl.ds(pl.multiple_of(j * tq, tq), tq), :]
        sc = jnp.zeros((tq, tq), F32)
        for half in range(2):
            nh = IDX_HEADS // 2
            d = _dot_nt(kk, iq_stack[half * nh * tq:(half + 1) * nh * tq])
            for hh in range(nh):
                h = half * nh + hh
                sc = sc + jnp.maximum(d[:, hh * tq:(hh + 1) * tq], 0.0) * wt[h:h + 1, :]
        if diag:
            sc = jnp.where(causal, sc, neg_inf)
        sc_ref[j] = as_i32(sc)
        cm_ref[...] = jnp.maximum(cm_ref[...], sc)

    def score_body(j, carry):
        score_tile(j, False)
        return carry

    lax.fori_loop(0, i, score_body, 0)
    score_tile(i, True)

    def count_cols(hit, thr):
        def body(j, cnt):
            words = sc_ref[j]
            for g in range(tq // 8):
                cnt = cnt + jnp.where(hit(words[8 * g:8 * (g + 1)], thr), 1, 0)
            return cnt

        cnt = lax.fori_loop(0, ntile, body, jnp.zeros((8, tq), jnp.int32))
        return _slab(jnp.sum(cnt, axis=0, keepdims=True))

    def count_ge(thr):
        return count_cols(lambda words, t: as_f32(words) >= t, thr)

    cm = cm_ref[...]
    lo = jnp.maximum(_float_to_key(_slab(jnp.min(cm, axis=0, keepdims=True))), _KEY_NEG_INF)
    hi = _float_to_key(_slab(jnp.max(cm, axis=0, keepdims=True))) + 1
    clo = jnp.full((8, tq), 2 ** 30, jnp.int32)
    chi = jnp.zeros((8, tq), jnp.int32)
    key_zero = 0
    key_tiny = 0x00800000
    hi = jnp.where((hi > key_zero) & (hi < key_tiny), key_tiny, hi)

    def open_rows(lo, hi, clo):
        zero_class = (lo == key_zero) & (hi == key_tiny)
        return (clo != ksel) & (hi - 1 > lo) & jnp.logical_not(zero_class)

    def any_rows(mask):
        return jnp.max(jnp.where(mask, 1, 0))

    def probe_key(lo, hi, clo, chi):
        mid = (lo >> 1) + (hi >> 1) + (lo & hi & 1)
        straddle = (lo < key_zero) & (hi > key_zero)
        width = hi - lo
        lo_f = _key_to_float(lo)
        frac = (clo - ksel).astype(F32) / (clo - chi).astype(F32)
        guess = _float_to_key(lo_f + (_key_to_float(hi) - lo_f) * frac)
        margin = width >> 3
        guess = jnp.minimum(jnp.maximum(guess, lo + 1 + margin), hi - 1 - margin)
        use = ((clo < 2 ** 30) & (width > INTERP_MIN_WIDTH) & (lo > _KEY_NEG_INF)
               & jnp.logical_not(straddle))
        probe = jnp.where(use, guess, mid)
        probe = jnp.where(straddle, key_zero, probe)
        return jnp.where((lo == key_zero) & (hi > key_tiny), key_tiny, probe)

    def sel_body(st):
        _, lo, hi, clo, chi = st
        mid = probe_key(lo, hi, clo, chi)
        cnt = count_ge(_key_to_float(mid))
        ge = cnt >= ksel
        act = open_rows(lo, hi, clo)
        up = act & ge
        dn = act & jnp.logical_not(ge)
        lo = jnp.where(up, mid, lo)
        clo = jnp.where(up, cnt, clo)
        hi = jnp.where(dn, mid, hi)
        chi = jnp.where(dn, cnt, chi)
        return any_rows(open_rows(lo, hi, clo)), lo, hi, clo, chi

    _, lo, hi, clo, chi = lax.while_loop(
        lambda st: st[0] > 0, sel_body, (any_rows(open_rows(lo, hi, clo)), lo, hi, clo, chi))
    exact = clo == ksel
    upper = _key_to_float(jnp.where(exact, lo, hi))[:1]
    lower = _key_to_float(lo)[:1]

    big = jnp.int32(2 ** 30)

    def rank_body(j, carry):
        sc = as_f32(sc_ref[j])
        sc_ref[j] = jnp.where(sc >= upper, -1, jnp.where(sc >= lower, key_row + j * tq, big))
        return carry

    lax.fori_loop(0, ntile, rank_body, 0)

    ja = jnp.where(exact, -1, 0)
    jb = jnp.where(exact, 0, ntile * tq)
    ca = jnp.where(exact, ksel, chi)
    cb = clo

    def idx_body(st):
        _, ja, jb, ca, cb = st
        width = jb - ja
        frac = (ksel - ca).astype(F32) / (cb - ca).astype(F32)
        guess = ja + (width.astype(F32) * frac).astype(jnp.int32)
        margin = width >> 3
        guess = jnp.minimum(jnp.maximum(guess, ja + 1 + margin), jb - 1 - margin)
        mid = jnp.where((cb < 2 ** 30) & (width > INTERP_MIN_INDEX), guess, (ja + jb) >> 1)
        cnt = count_cols(lambda words, t: words < t, mid)
        act = width > 1
        hit = act & (cnt == ksel)
        dn = act & (cnt >= ksel)
        up = act & (cnt < ksel)
        jb = jnp.where(dn, mid, jb)
        cb = jnp.where(dn, cnt, cb)
        ja = jnp.where(hit, mid - 1, jnp.where(up, mid, ja))
        ca = jnp.where(up, cnt, ca)
        return any_rows(jb - ja > 1), ja, jb, ca, cb

    _, _, jb, _, _ = lax.while_loop(lambda st: st[0] > 0, idx_body,
                                   (any_rows(jb - ja > 1), ja, jb, ca, cb))
    bound = jb[:1]

    _flash_init(m_ref, l_ref, acc_ref)
    q_stack = _head_stack(q_ref[...], HEADS_DSA)

    def attn_tile(j, diag):
        start = pl.multiple_of(j * tq, tq)
        sel = sc_ref[j] < bound
        if diag:
            sel = sel & causal
        cap = jnp.where(sel, -NEG, NEG)
        _flash_tile_t(q_stack, k_ref[pl.ds(start, tq), :], v_ref[pl.ds(start, tq), :], HEADS_DSA,
                      lambda s: jnp.minimum(s, cap), m_ref, l_ref, acc_ref)

    def attn_body(j, carry):
        attn_tile(j, False)
        return carry

    lax.fori_loop(0, i, attn_body, 0)
    attn_tile(i, True)
    o_ref[...] = _flash_out_t(HEADS_DSA, l_ref, acc_ref).astype(o_ref.dtype)


def _dsa_attention(iq, iw, ik, sq, sk, sv, batch, seq):
    T = sq.shape[0]
    tq = TQ
    nq = seq // tq
    ksel = min(DSA_TOPK, seq // 4)
    assert tq >= ksel
    width = HEADS_DSA * HEAD_DIM
    qrow = lambda b, i: (b * nq + i, 0)
    full = lambda b, i: (b, 0)
    return pl.pallas_call(
        functools.partial(_dsa_kernel, ksel=ksel),
        grid=(batch, nq),
        in_specs=[
            pl.BlockSpec((tq, IDX_HEADS * IDX_DIM), qrow),
            pl.BlockSpec((IDX_HEADS, tq), lambda b, i: (0, b * nq + i)),
            _resident((seq, IDX_HEADS * IDX_DIM), full),
            pl.BlockSpec((tq, width), qrow),
            _resident((seq, width), full),
            _resident((seq, width), full),
        ],
        out_specs=pl.BlockSpec((tq, width), qrow),
        out_shape=jax.ShapeDtypeStruct((T, width), BF16),
        scratch_shapes=[
            pltpu.VMEM((nq, tq, tq), jnp.int32),
            pltpu.VMEM((tq, tq), F32),
            pltpu.VMEM((HEADS_DSA, 8, tq), F32),
            pltpu.VMEM((HEADS_DSA, 8, tq), F32),
            pltpu.VMEM((HEADS_DSA, tq, width), F32),
        ],
        compiler_params=_cparams(("parallel", "arbitrary")),
        name="dsa_attn",
    )(iq, jnp.transpose(iw[:, :IDX_HEADS]), ik, sq, sk, sv)


def _oproj_kernel(x_ref, oa_ref, ob_ref, oc_ref, wo_ref, g_ref, x1_ref, h_ref):
    na = HEADS_MLA * HEAD_DIM
    nb = HEADS_DIL * HEAD_DIM
    x1 = (x_ref[...] + _dot(oa_ref[...], wo_ref[:na, :])
          + _dot(ob_ref[...], wo_ref[na:na + nb, :])
          + _dot(oc_ref[...], wo_ref[na + nb:, :]))
    x1_ref[...] = x1
    h_ref[...] = _rms(x1, g_ref[...]).astype(BF16)


def _out_projection(x2, oa, ob, oc, wo, g):
    T = x2.shape[0]
    tm = TM_FFN
    row = lambda i: (i, 0)
    const = lambda i: (0, 0)
    return pl.pallas_call(
        _oproj_kernel,
        grid=(T // tm,),
        in_specs=[
            pl.BlockSpec((tm, D_MODEL), row),
            pl.BlockSpec((tm, oa.shape[1]), row),
            pl.BlockSpec((tm, ob.shape[1]), row),
            pl.BlockSpec((tm, oc.shape[1]), row),
            pl.BlockSpec(wo.shape, const),
            pl.BlockSpec((1, D_MODEL), const),
        ],
        out_specs=[pl.BlockSpec((tm, D_MODEL), row), pl.BlockSpec((tm, D_MODEL), row)],
        out_shape=[jax.ShapeDtypeStruct((T, D_MODEL), F32), jax.ShapeDtypeStruct((T, D_MODEL), BF16)],
        compiler_params=_cparams(("parallel",)),
        name="oproj",
    )(x2, oa, ob, oc, wo, g)


def _ffn_kernel(h_ref, halo_ref, x1_ref, wg_ref, wu_ref, cg_ref, cu_ref, wd_ref, o_ref,
                uga_ref, uua_ref, ugb_ref, uub_ref, acc_ref, *, blocks_per_seq):
    tm = h_ref.shape[0]
    hal = halo_ref.shape[0]
    nchunk = wg_ref.shape[0]
    first = (pl.program_id(0) % blocks_per_seq) == 0
    keep = jnp.where(first, 0.0, 1.0).astype(F32)
    h = h_ref[...]
    hh = halo_ref[...]
    acc_ref[...] = jnp.zeros(acc_ref.shape, F32)

    def conv(u_ref, w):
        y = w[CONV_WIDTH:CONV_WIDTH + 1, :]
        for tap in range(CONV_WIDTH):
            off = hal - (CONV_WIDTH - 1) + tap
            y = y + u_ref[pl.ds(off, tm), :] * w[tap:tap + 1, :]
        return y

    def up_proj(j, slot):
        ug_ref, uu_ref = slot
        wg = wg_ref[j]
        wu = wu_ref[j]
        ug_ref[:hal, :] = _dot(hh, wg) * keep
        ug_ref[hal:, :] = _dot(h, wg)
        uu_ref[:hal, :] = _dot(hh, wu) * keep
        uu_ref[hal:, :] = _dot(h, wu)

    def down_proj(j, slot):
        ug_ref, uu_ref = slot
        gate = conv(ug_ref, cg_ref[j])
        up = conv(uu_ref, cu_ref[j])
        act = gate * jax.nn.sigmoid(gate) * up
        acc_ref[...] += _dot(act.astype(BF16), wd_ref[j])

    slot_a = (uga_ref, uua_ref)
    slot_b = (ugb_ref, uub_ref)
    up_proj(0, slot_a)

    def chunk_pair(jj, carry):
        j = 2 * jj
        up_proj(j + 1, slot_b)
        down_proj(j, slot_a)
        up_proj(j + 2, slot_a)
        down_proj(j + 1, slot_b)
        return carry

    lax.fori_loop(0, (nchunk - 1) // 2, chunk_pair, 0)
    if nchunk % 2 == 0:
        up_proj(nchunk - 1, slot_b)
        down_proj(nchunk - 2, slot_a)
        down_proj(nchunk - 1, slot_b)
    else:
        down_proj(nchunk - 1, slot_a)
    o_ref[...] = x1_ref[...] + acc_ref[...]


def _ffn(h2, x1, wg, wu, cg, cu, wd, seq):
    T = x1.shape[0]
    tm = TM_FFN
    hal = BF16_SUBLANES
    tn = wg.shape[2]
    row = lambda i: (i, 0)
    c3 = lambda i: (0, 0, 0)
    return pl.pallas_call(
        functools.partial(_ffn_kernel, blocks_per_seq=seq // tm),
        grid=(T // tm,),
        in_specs=[
            pl.BlockSpec((tm, D_MODEL), row),
            pl.BlockSpec((hal, D_MODEL), lambda i: (jnp.maximum(i * (tm // hal) - 1, 0), 0)),
            pl.BlockSpec((tm, D_MODEL), row),
            _resident(wg.shape, c3),
            _resident(wu.shape, c3),
            _resident(cg.shape, c3),
            _resident(cu.shape, c3),
            _resident(wd.shape, c3),
        ],
        out_specs=pl.BlockSpec((tm, D_MODEL), row),
        out_shape=jax.ShapeDtypeStruct((T, D_MODEL), F32),
        scratch_shapes=[
            pltpu.VMEM((hal + tm, tn), F32),
            pltpu.VMEM((hal + tm, tn), F32),
            pltpu.VMEM((hal + tm, tn), F32),
            pltpu.VMEM((hal + tm, tn), F32),
            pltpu.VMEM((tm, D_MODEL), F32),
        ],
        compiler_params=_cparams(("parallel",)),
        name="ffn",
    )(h2, h2, x1, wg, wu, cg, cu, wd)


def _final_norm_kernel(x_ref, g_ref, o_ref):
    o_ref[...] = _rms(x_ref[...], g_ref[...])


def _final_norm(x2, g):
    T = x2.shape[0]
    tm = TM_FFN
    return pl.pallas_call(
        _final_norm_kernel,
        grid=(T // tm,),
        in_specs=[pl.BlockSpec((tm, D_MODEL), lambda i: (i, 0)),
                  pl.BlockSpec((1, D_MODEL), lambda i: (0, 0))],
        out_specs=pl.BlockSpec((tm, D_MODEL), lambda i: (i, 0)),
        out_shape=jax.ShapeDtypeStruct((T, D_MODEL), F32),
        compiler_params=_cparams(("parallel",)),
        name="final_norm",
    )(x2, g)


def _rope_tables(seq):
    def base(dim):
        inv = jnp.power(jnp.float32(ROPE_THETA), -jnp.arange(0, dim, 2, dtype=F32) / dim)
        ang = jnp.arange(seq, dtype=F32)[:, None] * inv[None, :]
        return jnp.cos(ang), jnp.sin(ang)

    def tables(dim, period):
        cos, sin = base(dim)
        half = dim // 2
        pad = period - dim
        c = jnp.concatenate([cos, cos, jnp.ones((seq, pad), F32)], axis=1)
        sn = jnp.concatenate([-sin, jnp.zeros((seq, half + pad), F32)], axis=1)
        sp = jnp.concatenate([jnp.zeros((seq, half), F32), sin, jnp.zeros((seq, pad), F32)], axis=1)
        reps = LANES // period
        return [jnp.tile(t, (1, reps)) for t in (c, sn, sp)]

    tabs = (tables(MLA_ROPE, LANES) + tables(HEAD_DIM // ROT_FRAC, HEAD_DIM)
            + tables(IDX_DIM // ROT_FRAC, IDX_DIM))
    return jnp.stack(tabs)


def _pad_cols(w, width):
    return jnp.pad(w, ((0, 0), (0, width - w.shape[1])))


def _layer_weights(w_in, w_uq, w_ukv, w_up, conv_w, conv_b, w_down):
    o = 0
    parts = {}
    for name, width in (("cq", Q_LORA), ("ckv", KV_LORA), ("kpe", MLA_ROPE),
                        ("dil", 3 * HEADS_DIL * HEAD_DIM), ("dsa", 3 * HEADS_DSA * HEAD_DIM),
                        ("qi", IDX_HEADS * IDX_DIM), ("ki", IDX_DIM), ("wi", IDX_HEADS)):
        parts[name] = w_in[:, o:o + width]
        o += width
    w1 = jnp.concatenate([
        parts["cq"], parts["ckv"], _pad_cols(parts["kpe"], LANES), parts["dil"], parts["dsa"],
        parts["qi"], jnp.tile(parts["ki"], (1, IDX_HEADS)), _pad_cols(parts["wi"], LANES)],
        axis=1).astype(BF16)
    assert w1.shape[1] == N_PROJ

    dq = MLA_NOPE + MLA_ROPE
    cols = []
    for hd in range(HEADS_MLA):
        nope = w_uq[:, dq * hd:dq * hd + MLA_NOPE]
        pe = w_uq[:, dq * hd + MLA_NOPE:dq * (hd + 1)]
        z = jnp.zeros((Q_LORA, MLA_NOPE), w_uq.dtype)
        slot = [nope, z] if hd % 2 == 0 else [z, nope]
        cols += slot + [_pad_cols(pe, LANES)]
    wuq = jnp.concatenate(cols, axis=1).astype(BF16)

    dkv = MLA_NOPE + HEAD_DIM
    kn = [w_ukv[:, dkv * hd:dkv * hd + MLA_NOPE] for hd in range(HEADS_MLA)]
    vv = [w_ukv[:, dkv * hd + MLA_NOPE:dkv * (hd + 1)] for hd in range(HEADS_MLA)]
    wukv = jnp.concatenate(kn + vv, axis=1).astype(BF16)

    nchunk = D_FF // TN_FFN
    def chunks(w):
        return w.reshape(w.shape[0], nchunk, TN_FFN).transpose(1, 0, 2)
    wg = chunks(w_up[:, :D_FF]).astype(BF16)
    wu = chunks(w_up[:, D_FF:]).astype(BF16)
    cw = jnp.concatenate([conv_w, conv_b[None, :],
                          jnp.zeros((8 - CONV_WIDTH - 1, 2 * D_FF), F32)], axis=0)
    cg = chunks(cw[:, :D_FF])
    cu = chunks(cw[:, D_FF:])
    wd = w_down.reshape(nchunk, TN_FFN, D_MODEL).astype(BF16)
    return w1, wuq, wukv, wg, wu, cg, cu, wd


def kernel(x, g_attn, w_in, g_q_lat, w_uq, g_kv_lat, w_ukv, w_o, g_ffn, w_up, conv_w, conv_b,
           w_down, g_final):
    batch, seq, _ = x.shape
    depth = w_in.shape[0]
    assert seq % DIL_BAND == 0 and seq % TM_FFN == 0 and seq % TQ == 0
    assert D_FF % TN_FFN == 0
    T = batch * seq
    tabs = _rope_tables(seq)
    bias = _dil_bias(TQ)
    xf = x.reshape(T, D_MODEL)
    for l in range(depth):
        w1, wuq, wukv, wg, wu, cg, cu, wd = _layer_weights(
            w_in[l], w_uq[l], w_ukv[l], w_up[l], conv_w[l], conv_b[l], w_down[l])
        (mq, mk, mv, dq, dk, dv, sq, sk, sv, iq, ik, iw) = _projection(
            xf, g_attn[l][None, :], w1, g_q_lat[l][None, :], wuq, g_kv_lat[l][None, :], wukv,
            tabs, seq)
        oa = _mla_attention(mq, mk, mv, batch, seq)
        ob = _dil_attention(dq, dk, dv, bias, batch, seq)
        oc = _dsa_attention(iq, iw, ik, sq, sk, sv, batch, seq)
        x1, h2 = _out_projection(xf, oa, ob, oc, w_o[l].astype(BF16), g_ffn[l][None, :])
        xf = _ffn(h2, x1, wg, wu, cg, cu, wd, seq)
    return _final_norm(xf, g_final[None, :]).reshape(batch, seq, D_MODEL)
```

```python
import functools
import math

import jax
import jax.numpy as jnp
from jax import lax
from jax.experimental import pallas as pl
from jax.experimental.pallas import tpu as pltpu

D_MODEL = 1024
HEAD_DIM = 64
HEADS_MLA = 4
HEADS_DIL = 8
HEADS_DSA = 4
Q_LORA = 256
KV_LORA = 128
MLA_NOPE = 64
MLA_ROPE = 32
DIL_PAIRS = ((128, 1), (512, 4), (2048, 16))
IDX_HEADS = 8
IDX_DIM = 32
DSA_TOPK = 256
D_FF = 2816
CONV_WIDTH = 3
ROPE_THETA = 500000.0
ROT_FRAC = 4
NORM_EPS = 1e-6

LANES = 128
BF16_SUBLANES = 16
VMEM_LIMIT = 56 * 1024 * 1024
NEG = -1e30
LOG2E = math.log2(math.e)
F32 = jnp.float32
BF16 = jnp.bfloat16

TQ = 512
TM_ROW = 512
TM_FFN = 1024
TN_FFN = 256
DIL_BAND = max(w for w, _ in DIL_PAIRS)

C_CQ = 0
C_CKV = C_CQ + Q_LORA
C_KPE = C_CKV + KV_LORA
C_DIL = C_KPE + LANES
C_DSA = C_DIL + 3 * HEADS_DIL * HEAD_DIM
C_QI = C_DSA + 3 * HEADS_DSA * HEAD_DIM
C_KI = C_QI + IDX_HEADS * IDX_DIM
C_WI = C_KI + IDX_HEADS * IDX_DIM
N_PROJ = C_WI + LANES


def _cparams(sem):
    return pltpu.CompilerParams(dimension_semantics=sem, vmem_limit_bytes=VMEM_LIMIT)


def _resident(shape, index_map):
    return pl.BlockSpec(shape, index_map, pipeline_mode=pl.Buffered(1))


def _rms(x, g):
    return x * lax.rsqrt(jnp.mean(x * x, axis=-1, keepdims=True) + NORM_EPS) * g


def _dot(a, b):
    return jnp.dot(a, b, preferred_element_type=F32)


def _dot_nt(a, b):
    return lax.dot_general(a, b, (((1,), (1,)), ((), ())), preferred_element_type=F32)


def _rope(x, tab_ref, t0, half):
    c = tab_ref[t0]
    sn = tab_ref[t0 + 1]
    sp = tab_ref[t0 + 2]
    return (x * c + pltpu.roll(x, LANES - half, axis=1) * sn
            + pltpu.roll(x, half, axis=1) * sp)


def _proj_kernel(x_ref, g_ref, w1_ref, gq_ref, wuq_ref, gkv_ref, wukv_ref, tab_ref,
                 mq_ref, mk_ref, mv_ref, dq_ref, dk_ref, dv_ref,
                 sq_ref, sk_ref, sv_ref, iq_ref, ik_ref, iw_ref):
    h = _rms(x_ref[...], g_ref[...]).astype(BF16)

    def proj(lo, width):
        return _dot(h, w1_ref[:, lo:lo + width])

    mla_scale = (MLA_NOPE + MLA_ROPE) ** -0.5 * LOG2E
    cq = _rms(proj(C_CQ, Q_LORA), gq_ref[...]).astype(BF16)
    for hd in range(HEADS_MLA):
        base = 2 * LANES * hd
        q2 = _dot(cq, wuq_ref[:, base:base + 2 * LANES])
        qp = _rope(q2[:, LANES:], tab_ref, 0, MLA_ROPE // 2)
        mq_ref[:, base:base + LANES] = (q2[:, :LANES] * mla_scale).astype(BF16)
        mq_ref[:, base + LANES:base + 2 * LANES] = (qp * mla_scale).astype(BF16)
    zkv = proj(C_CKV, KV_LORA + LANES)
    ckv = _rms(zkv[:, :KV_LORA], gkv_ref[...]).astype(BF16)
    kpe = _rope(zkv[:, KV_LORA:], tab_ref, 0, MLA_ROPE // 2).astype(BF16)
    kv = _dot(ckv, wukv_ref[...])
    for pr in range(HEADS_MLA // 2):
        mk_ref[:, 2 * LANES * pr:2 * LANES * pr + LANES] = kv[:, LANES * pr:LANES * (pr + 1)].astype(BF16)
        mk_ref[:, 2 * LANES * pr + LANES:2 * LANES * (pr + 1)] = kpe
    mv_ref[...] = kv[:, HEADS_MLA * MLA_NOPE:].astype(BF16)

    head_scale = HEAD_DIM ** -0.5 * LOG2E
    half_h = HEAD_DIM // ROT_FRAC // 2

    def roped(c0, width, t0, half, scale, o_ref):
        step = min(width, 2 * LANES)
        for c in range(0, width, step):
            z = proj(c0 + c, step)
            for off in range(0, step, LANES):
                r = _rope(z[:, off:off + LANES], tab_ref, t0, half)
                if scale is not None:
                    r = r * scale
                o_ref[:, c + off:c + off + LANES] = r.astype(BF16)

    def qkv(c0, nheads, q_ref, k_ref, v_ref):
        width = nheads * HEAD_DIM
        roped(c0, width, 3, half_h, head_scale, q_ref)
        roped(c0 + width, width, 3, half_h, None, k_ref)
        v_ref[...] = proj(c0 + 2 * width, width).astype(BF16)

    qkv(C_DIL, HEADS_DIL, dq_ref, dk_ref, dv_ref)
    qkv(C_DSA, HEADS_DSA, sq_ref, sk_ref, sv_ref)

    half_i = IDX_DIM // ROT_FRAC // 2
    roped(C_QI, IDX_HEADS * IDX_DIM, 6, half_i, None, iq_ref)
    roped(C_KI, IDX_HEADS * IDX_DIM, 6, half_i, None, ik_ref)
    iw_ref[...] = proj(C_WI, LANES) * ((IDX_HEADS * IDX_DIM) ** -0.5)


def _projection(x2, g, w1, gq, wuq, gkv, wukv, tabs, seq):
    T = x2.shape[0]
    tm = TM_ROW
    nsb = seq // tm
    row = lambda i: (i, 0)
    const = lambda i: (0, 0)
    widths = [2 * LANES * HEADS_MLA, 2 * LANES * (HEADS_MLA // 2), HEADS_MLA * HEAD_DIM,
              HEADS_DIL * HEAD_DIM, HEADS_DIL * HEAD_DIM, HEADS_DIL * HEAD_DIM,
              HEADS_DSA * HEAD_DIM, HEADS_DSA * HEAD_DIM, HEADS_DSA * HEAD_DIM,
              IDX_HEADS * IDX_DIM, IDX_HEADS * IDX_DIM]
    out_shape = [jax.ShapeDtypeStruct((T, w), BF16) for w in widths]
    out_shape.append(jax.ShapeDtypeStruct((T, LANES), F32))
    out_specs = [pl.BlockSpec((tm, w), row) for w in widths] + [pl.BlockSpec((tm, LANES), row)]
    return pl.pallas_call(
        _proj_kernel,
        grid=(T // tm,),
        in_specs=[
            pl.BlockSpec((tm, D_MODEL), row),
            pl.BlockSpec((1, D_MODEL), const),
            pl.BlockSpec(w1.shape, const),
            pl.BlockSpec((1, Q_LORA), const),
            pl.BlockSpec(wuq.shape, const),
            pl.BlockSpec((1, KV_LORA), const),
            pl.BlockSpec(wukv.shape, const),
            pl.BlockSpec((9, tm, LANES), lambda i: (0, i % nsb, 0)),
        ],
        out_specs=out_specs,
        out_shape=out_shape,
        compiler_params=_cparams(("parallel",)),
        name="proj",
    )(x2, g, w1, gq, wuq, gkv, wukv, tabs)


def _rep(x, n):
    return x if n == 1 else jnp.concatenate([x] * n, axis=1)


def _flash_tile(q_stack, k, v, nheads, mask_fn, m_ref, l_ref, acc_ref):
    tq = q_stack.shape[0] // nheads
    tk = k.shape[0]
    dv = v.shape[1]
    s_all = _dot_nt(q_stack, k)
    ps = []
    alphas = []
    for h in range(nheads):
        s = mask_fn(s_all[h * tq:(h + 1) * tq])
        m_prev = m_ref[h]
        m_new = jnp.maximum(m_prev, jnp.max(s, axis=1, keepdims=True))
        alpha = jnp.exp2(m_prev - m_new)
        p = jnp.exp2(s - _rep(m_new, tk // LANES))
        l_ref[h] = alpha * l_ref[h] + jnp.sum(p, axis=1, keepdims=True)
        m_ref[h] = m_new
        ps.append(p.astype(BF16))
        alphas.append(alpha)
    pv = _dot(jnp.concatenate(ps, axis=0), v)
    for h in range(nheads):
        acc_ref[h] = _rep(alphas[h], dv // LANES) * acc_ref[h] + pv[h * tq:(h + 1) * tq]


def _flash_init(m_ref, l_ref, acc_ref):
    m_ref[...] = jnp.full(m_ref.shape, NEG, F32)
    l_ref[...] = jnp.zeros(l_ref.shape, F32)
    acc_ref[...] = jnp.zeros(acc_ref.shape, F32)


def _flash_out(nheads, l_ref, acc_ref):
    tq, dv = acc_ref.shape[1:]
    lane = lax.broadcasted_iota(jnp.int32, (tq, dv), 1)
    out = jnp.zeros((tq, dv), F32)
    for h in range(nheads):
        out = jnp.where((lane // HEAD_DIM) == h, acc_ref[h] / _rep(l_ref[h], dv // LANES), out)
    return out


def _slab(x):
    return jnp.broadcast_to(x, (8, x.shape[1]))


def _to_rows(x, width):
    tq = x.shape[1]
    return _rep(jnp.transpose(jnp.broadcast_to(x[:1], (LANES, tq))), width // LANES)


def _flash_tile_t(q_stack, k, v, nheads, mask_fn, m_ref, l_ref, acc_ref):
    tq = q_stack.shape[0] // nheads
    dv = v.shape[1]
    s_all = _dot_nt(k, q_stack)
    ps = []
    alphas = []
    for h in range(nheads):
        s = mask_fn(s_all[:, h * tq:(h + 1) * tq])
        m_prev = m_ref[h]
        m_new = jnp.maximum(m_prev, _slab(jnp.max(s, axis=0, keepdims=True)))
        alpha = jnp.exp2(m_prev - m_new)
        p = jnp.exp2(s - m_new[:1])
        l_ref[h] = alpha * l_ref[h] + _slab(jnp.sum(p, axis=0, keepdims=True))
        m_ref[h] = m_new
        ps.append(p.astype(BF16))
        alphas.append(alpha)
    pv = lax.dot_general(jnp.concatenate(ps, axis=1), v, (((0,), (0,)), ((), ())),
                         preferred_element_type=F32)
    for h in range(nheads):
        acc_ref[h] = _to_rows(alphas[h], dv) * acc_ref[h] + pv[h * tq:(h + 1) * tq]


def _flash_out_t(nheads, l_ref, acc_ref):
    tq, dv = acc_ref.shape[1:]
    lane = lax.broadcasted_iota(jnp.int32, (tq, dv), 1)
    out = jnp.zeros((tq, dv), F32)
    for h in range(nheads):
        out = jnp.where((lane // HEAD_DIM) == h, acc_ref[h] / _to_rows(l_ref[h], dv), out)
    return out


def _head_stack(q, nheads):
    lane = lax.broadcasted_iota(jnp.int32, q.shape, 1)
    zero = jnp.zeros_like(q)
    return jnp.concatenate([jnp.where((lane // HEAD_DIM) == h, q, zero) for h in range(nheads)], axis=0)


def _causal_tile(tq):
    r = lax.broadcasted_iota(jnp.int32, (tq, tq), 0)
    c = lax.broadcasted_iota(jnp.int32, (tq, tq), 1)
    return r >= c


def _mla_kernel(q_ref, k_ref, v_ref, o_ref, m_ref, l_ref, acc_ref):
    tq = q_ref.shape[0]
    i = pl.program_id(2)
    _flash_init(m_ref, l_ref, acc_ref)
    q_stack = jnp.concatenate([q_ref[:, :2 * LANES], q_ref[:, 2 * LANES:]], axis=0)

    def step(j, mask_fn):
        start = pl.multiple_of(j * tq, tq)
        _flash_tile(q_stack, k_ref[pl.ds(start, tq), :], v_ref[pl.ds(start, tq), :], 2,
                    mask_fn, m_ref, l_ref, acc_ref)

    def body(j, carry):
        step(j, lambda s: s)
        return carry

    lax.fori_loop(0, i, body, 0)
    step(i, lambda s: jnp.where(_causal_tile(tq), s, NEG))
    o_ref[...] = _flash_out(2, l_ref, acc_ref).astype(o_ref.dtype)


def _mla_attention(mq, mk, mv, batch, seq):
    T = mq.shape[0]
    tq = TQ
    nq = seq // tq
    npair = HEADS_MLA // 2
    return pl.pallas_call(
        _mla_kernel,
        grid=(batch, npair, nq),
        in_specs=[
            pl.BlockSpec((tq, 4 * LANES), lambda b, p, i: (b * nq + i, p)),
            _resident((seq, 2 * LANES), lambda b, p, i: (b, p)),
            _resident((seq, LANES), lambda b, p, i: (b, p)),
        ],
        out_specs=pl.BlockSpec((tq, LANES), lambda b, p, i: (b * nq + i, p)),
        out_shape=jax.ShapeDtypeStruct((T, HEADS_MLA * HEAD_DIM), BF16),
        scratch_shapes=[pltpu.VMEM((2, tq, LANES), F32)] * 3,
        compiler_params=_cparams(("parallel", "parallel", "arbitrary")),
        name="mla_attn",
    )(mq, mk, mv)


def _dil_bias(tq):
    nband = DIL_BAND // tq + 1
    d = (lax.broadcasted_iota(jnp.int32, (nband, tq, tq), 0) * tq
         + lax.broadcasted_iota(jnp.int32, (nband, tq, tq), 1)
         - lax.broadcasted_iota(jnp.int32, (nband, tq, tq), 2))
    cnt = jnp.zeros(d.shape, F32)
    for window, dil in DIL_PAIRS:
        ok = (d >= 0) & (d % dil == 0) & (d <= (window // dil) * dil)
        cnt = cnt + ok.astype(F32)
    return jnp.where(cnt > 0, jnp.log2(jnp.maximum(cnt, 1.0)), NEG)


def _dil_kernel(q_ref, k_ref, v_ref, bias_ref, o_ref, m_ref, l_ref, acc_ref):
    tq = q_ref.shape[0]
    nband = bias_ref.shape[0]
    i = pl.program_id(2)
    _flash_init(m_ref, l_ref, acc_ref)
    q_stack = _head_stack(q_ref[...], 2)

    def step(delta):
        start = pl.multiple_of((i - delta) * tq, tq)
        bias = bias_ref[delta]
        _flash_tile(q_stack, k_ref[pl.ds(start, tq), :], v_ref[pl.ds(start, tq), :], 2,
                    lambda s: s + bias, m_ref, l_ref, acc_ref)

    step(0)

    def body(delta, carry):
        step(delta)
        return carry

    lax.fori_loop(1, jnp.minimum(i, nband - 1) + 1, body, 0)
    o_ref[...] = _flash_out(2, l_ref, acc_ref).astype(o_ref.dtype)


def _dil_attention(dq, dk, dv, bias, batch, seq):
    T = dq.shape[0]
    tq = TQ
    nq = seq // tq
    npair = HEADS_DIL // 2
    return pl.pallas_call(
        _dil_kernel,
        grid=(batch, npair, nq),
        in_specs=[
            pl.BlockSpec((tq, LANES), lambda b, p, i: (b * nq + i, p)),
            _resident((seq, LANES), lambda b, p, i: (b, p)),
            _resident((seq, LANES), lambda b, p, i: (b, p)),
            _resident(bias.shape, lambda b, p, i: (0, 0, 0)),
        ],
        out_specs=pl.BlockSpec((tq, LANES), lambda b, p, i: (b * nq + i, p)),
        out_shape=jax.ShapeDtypeStruct((T, HEADS_DIL * HEAD_DIM), BF16),
        scratch_shapes=[pltpu.VMEM((2, tq, LANES), F32)] * 3,
        compiler_params=_cparams(("parallel", "parallel", "arbitrary")),
        name="dil_attn",
    )(dq, dk, dv, bias)


_INT_MIN = -(2 ** 31)
_KEY_NEG_INF = _INT_MIN + 0x7FFFFF
INTERP_MIN_WIDTH = 4096
INTERP_MIN_INDEX = 32


def _ordered_bits(x):
    return x ^ ((x >> 31) & 0x7FFFFFFF)


def _key_to_float(key):
    return lax.bitcast_convert_type(_ordered_bits(key), F32)


def _float_to_key(x):
    return _ordered_bits(lax.bitcast_convert_type(x, jnp.int32))


def _dsa_kernel(iq_ref, iwt_ref, ik_ref, q_ref, k_ref, v_ref, o_ref,
                sc_ref, cm_ref, m_ref, l_ref, acc_ref, *, ksel):
    tq = q_ref.shape[0]
    i = pl.program_id(1)
    ntile = i + 1
    key_row = lax.broadcasted_iota(jnp.int32, (tq, tq), 0)
    causal = key_row <= lax.broadcasted_iota(jnp.int32, (tq, tq), 1)
    neg_inf = jnp.float32(-jnp.inf)
    as_f32 = lambda v: lax.bitcast_convert_type(v, F32)
    as_i32 = lambda v: lax.bitcast_convert_type(v, jnp.int32)

    iq_stack = jnp.concatenate(
        [jnp.where((lax.broadcasted_iota(jnp.int32, iq_ref.shape, 1) // IDX_DIM) == h,
                   iq_ref[...], jnp.zeros(iq_ref.shape, BF16)) for h in range(IDX_HEADS)], axis=0)
    wt = iwt_ref[...]
    cm_ref[...] = jnp.full(cm_ref.shape, neg_inf, F32)

    def score_tile(j, diag):
        kk = ik_ref[pl.ds(pl.multiple_of(j * tq, tq), tq), :]
        sc = jnp.zeros((tq, tq), F32)
        for half in range(2):
            nh = IDX_HEADS // 2
            d = _dot_nt(kk, iq_stack[half * nh * tq:(half + 1) * nh * tq])
            for hh in range(nh):
                h = half * nh + hh
                sc = sc + jnp.maximum(d[:, hh * tq:(hh + 1) * tq], 0.0) * wt[h:h + 1, :]
        if diag:
            sc = jnp.where(causal, sc, neg_inf)
        sc_ref[j] = as_i32(sc)
        cm_ref[...] = jnp.maximum(cm_ref[...], sc)

    def score_body(j, carry):
        score_tile(j, False)
        return carry

    lax.fori_loop(0, i, score_body, 0)
    score_tile(i, True)

    def count_cols(hit, thr):
        def body(j, cnt):
            words = sc_ref[j]
            for g in range(tq // 8):
                cnt = cnt + jnp.where(hit(words[8 * g:8 * (g + 1)], thr), 1, 0)
            return cnt

        cnt = lax.fori_loop(0, ntile, body, jnp.zeros((8, tq), jnp.int32))
        return _slab(jnp.sum(cnt, axis=0, keepdims=True))

    def count_ge(thr):
        return count_cols(lambda words, t: as_f32(words) >= t, thr)

    cm = cm_ref[...]
    lo = jnp.maximum(_float_to_key(_slab(jnp.min(cm, axis=0, keepdims=True))), _KEY_NEG_INF)
    hi = _float_to_key(_slab(jnp.max(cm, axis=0, keepdims=True))) + 1
    clo = jnp.full((8, tq), 2 ** 30, jnp.int32)
    chi = jnp.zeros((8, tq), jnp.int32)
    key_zero = 0
    key_tiny = 0x00800000
    hi = jnp.where((hi > key_zero) & (hi < key_tiny), key_tiny, hi)

    def open_rows(lo, hi, clo):
        zero_class = (lo == key_zero) & (hi == key_tiny)
        return (clo != ksel) & (hi - 1 > lo) & jnp.logical_not(zero_class)

    def any_rows(mask):
        return jnp.max(jnp.where(mask, 1, 0))

    def probe_key(lo, hi, clo, chi):
        mid = (lo >> 1) + (hi >> 1) + (lo & hi & 1)
        straddle = (lo < key_zero) & (hi > key_zero)
        width = hi - lo
        lo_f = _key_to_float(lo)
        frac = (clo - ksel).astype(F32) / (clo - chi).astype(F32)
        guess = _float_to_key(lo_f + (_key_to_float(hi) - lo_f) * frac)
        margin = width >> 3
        guess = jnp.minimum(jnp.maximum(guess, lo + 1 + margin), hi - 1 - margin)
        use = ((clo < 2 ** 30) & (width > INTERP_MIN_WIDTH) & (lo > _KEY_NEG_INF)
               & jnp.logical_not(straddle))
        probe = jnp.where(use, guess, mid)
        probe = jnp.where(straddle, key_zero, probe)
        return jnp.where((lo == key_zero) & (hi > key_tiny), key_tiny, probe)

    def sel_body(st):
        _, lo, hi, clo, chi = st
        mid = probe_key(lo, hi, clo, chi)
        cnt = count_ge(_key_to_float(mid))
        ge = cnt >= ksel
        act = open_rows(lo, hi, clo)
        up = act & ge
        dn = act & jnp.logical_not(ge)
        lo = jnp.where(up, mid, lo)
        clo = jnp.where(up, cnt, clo)
        hi = jnp.where(dn, mid, hi)
        chi = jnp.where(dn, cnt, chi)
        return any_rows(open_rows(lo, hi, clo)), lo, hi, clo, chi

    _, lo, hi, clo, chi = lax.while_loop(
        lambda st: st[0] > 0, sel_body, (any_rows(open_rows(lo, hi, clo)), lo, hi, clo, chi))
    exact = clo == ksel
    upper = _key_to_float(jnp.where(exact, lo, hi))[:1]
    lower = _key_to_float(lo)[:1]

    big = jnp.int32(2 ** 30)

    def rank_body(j, carry):
        sc = as_f32(sc_ref[j])
        sc_ref[j] = jnp.where(sc >= upper, -1, jnp.where(sc >= lower, key_row + j * tq, big))
        return carry

    lax.fori_loop(0, ntile, rank_body, 0)

    ja = jnp.where(exact, -1, 0)
    jb = jnp.where(exact, 0, ntile * tq)
    ca = jnp.where(exact, ksel, chi)
    cb = clo

    def idx_body(st):
        _, ja, jb, ca, cb = st
        width = jb - ja
        frac = (ksel - ca).astype(F32) / (cb - ca).astype(F32)
        guess = ja + (width.astype(F32) * frac).astype(jnp.int32)
        margin = width >> 3
        guess = jnp.minimum(jnp.maximum(guess, ja + 1 + margin), jb - 1 - margin)
        mid = jnp.where((cb < 2 ** 30) & (width > INTERP_MIN_INDEX), guess, (ja + jb) >> 1)
        cnt = count_cols(lambda words, t: words < t, mid)
        act = width > 1
        hit = act & (cnt == ksel)
        dn = act & (cnt >= ksel)
        up = act & (cnt < ksel)
        jb = jnp.where(dn, mid, jb)
        cb = jnp.where(dn, cnt, cb)
        ja = jnp.where(hit, mid - 1, jnp.where(up, mid, ja))
        ca = jnp.where(up, cnt, ca)
        return any_rows(jb - ja > 1), ja, jb, ca, cb

    _, _, jb, _, _ = lax.while_loop(lambda st: st[0] > 0, idx_body,
                                   (any_rows(jb - ja > 1), ja, jb, ca, cb))
    bound = jb[:1]

    _flash_init(m_ref, l_ref, acc_ref)
    q_stack = _head_stack(q_ref[...], HEADS_DSA)

    def attn_tile(j, diag):
        start = pl.multiple_of(j * tq, tq)
        sel = sc_ref[j] < bound
        if diag:
            sel = sel & causal
        cap = jnp.where(sel, -NEG, NEG)
        _flash_tile_t(q_stack, k_ref[pl.ds(start, tq), :], v_ref[pl.ds(start, tq), :], HEADS_DSA,
                      lambda s: jnp.minimum(s, cap), m_ref, l_ref, acc_ref)

    def attn_body(j, carry):
        attn_tile(j, False)
        return carry

    lax.fori_loop(0, i, attn_body, 0)
    attn_tile(i, True)
    o_ref[...] = _flash_out_t(HEADS_DSA, l_ref, acc_ref).astype(o_ref.dtype)


def _dsa_attention(iq, iw, ik, sq, sk, sv, batch, seq):
    T = sq.shape[0]
    tq = TQ
    nq = seq // tq
    ksel = min(DSA_TOPK, seq // 4)
    assert tq >= ksel
    width = HEADS_DSA * HEAD_DIM
    qrow = lambda b, i: (b * nq + i, 0)
    full = lambda b, i: (b, 0)
    return pl.pallas_call(
        functools.partial(_dsa_kernel, ksel=ksel),
        grid=(batch, nq),
        in_specs=[
            pl.BlockSpec((tq, IDX_HEADS * IDX_DIM), qrow),
            pl.BlockSpec((IDX_HEADS, tq), lambda b, i: (0, b * nq + i)),
            _resident((seq, IDX_HEADS * IDX_DIM), full),
            pl.BlockSpec((tq, width), qrow),
            _resident((seq, width), full),
            _resident((seq, width), full),
        ],
        out_specs=pl.BlockSpec((tq, width), qrow),
        out_shape=jax.ShapeDtypeStruct((T, width), BF16),
        scratch_shapes=[
            pltpu.VMEM((nq, tq, tq), jnp.int32),
            pltpu.VMEM((tq, tq), F32),
            pltpu.VMEM((HEADS_DSA, 8, tq), F32),
            pltpu.VMEM((HEADS_DSA, 8, tq), F32),
            pltpu.VMEM((HEADS_DSA, tq, width), F32),
        ],
        compiler_params=_cparams(("parallel", "arbitrary")),
        name="dsa_attn",
    )(iq, jnp.transpose(iw[:, :IDX_HEADS]), ik, sq, sk, sv)


def _oproj_kernel(x_ref, oa_ref, ob_ref, oc_ref, wo_ref, g_ref, x1_ref, h_ref):
    na = HEADS_MLA * HEAD_DIM
    nb = HEADS_DIL * HEAD_DIM
    x1 = (x_ref[...] + _dot(oa_ref[...], wo_ref[:na, :])
          + _dot(ob_ref[...], wo_ref[na:na + nb, :])
          + _dot(oc_ref[...], wo_ref[na + nb:, :]))
    x1_ref[...] = x1
    h_ref[...] = _rms(x1, g_ref[...]).astype(BF16)


def _out_projection(x2, oa, ob, oc, wo, g):
    T = x2.shape[0]
    tm = TM_FFN
    row = lambda i: (i, 0)
    const = lambda i: (0, 0)
    return pl.pallas_call(
        _oproj_kernel,
        grid=(T // tm,),
        in_specs=[
            pl.BlockSpec((tm, D_MODEL), row),
            pl.BlockSpec((tm, oa.shape[1]), row),
            pl.BlockSpec((tm, ob.shape[1]), row),
            pl.BlockSpec((tm, oc.shape[1]), row),
            pl.BlockSpec(wo.shape, const),
            pl.BlockSpec((1, D_MODEL), const),
        ],
        out_specs=[pl.BlockSpec((tm, D_MODEL), row), pl.BlockSpec((tm, D_MODEL), row)],
        out_shape=[jax.ShapeDtypeStruct((T, D_MODEL), F32), jax.ShapeDtypeStruct((T, D_MODEL), BF16)],
        compiler_params=_cparams(("parallel",)),
        name="oproj",
    )(x2, oa, ob, oc, wo, g)


def _ffn_kernel(h_ref, halo_ref, x1_ref, wg_ref, wu_ref, cg_ref, cu_ref, wd_ref, *rest,
                blocks_per_seq, final_norm):
    gf_ref = rest[0] if final_norm else None
    o_ref, uga_ref, uua_ref, ugb_ref, uub_ref, acc_ref = rest[1:] if final_norm else rest
    tm = h_ref.shape[0]
    hal = halo_ref.shape[0]
    nchunk = wg_ref.shape[0]
    first = (pl.program_id(0) % blocks_per_seq) == 0
    keep = jnp.where(first, 0.0, 1.0).astype(F32)
    h = h_ref[...]
    hh = halo_ref[...]
    acc_ref[...] = jnp.zeros(acc_ref.shape, F32)

    def conv(u_ref, w):
        y = w[CONV_WIDTH:CONV_WIDTH + 1, :]
        for tap in range(CONV_WIDTH):
            off = hal - (CONV_WIDTH - 1) + tap
            y = y + u_ref[pl.ds(off, tm), :] * w[tap:tap + 1, :]
        return y

    def up_proj(j, slot):
        ug_ref, uu_ref = slot
        wg = wg_ref[j]
        wu = wu_ref[j]
        ug_ref[:hal, :] = _dot(hh, wg) * keep
        ug_ref[hal:, :] = _dot(h, wg)
        uu_ref[:hal, :] = _dot(hh, wu) * keep
        uu_ref[hal:, :] = _dot(h, wu)

    def down_proj(j, slot):
        ug_ref, uu_ref = slot
        gate = conv(ug_ref, cg_ref[j])
        up = conv(uu_ref, cu_ref[j])
        act = gate * jax.nn.sigmoid(gate) * up
        acc_ref[...] += _dot(act.astype(BF16), wd_ref[j])

    slot_a = (uga_ref, uua_ref)
    slot_b = (ugb_ref, uub_ref)
    up_proj(0, slot_a)

    def chunk_pair(jj, carry):
        j = 2 * jj
        up_proj(j + 1, slot_b)
        down_proj(j, slot_a)
        up_proj(j + 2, slot_a)
        down_proj(j + 1, slot_b)
        return carry

    lax.fori_loop(0, (nchunk - 1) // 2, chunk_pair, 0)
    if nchunk % 2 == 0:
        up_proj(nchunk - 1, slot_b)
        down_proj(nchunk - 2, slot_a)
        down_proj(nchunk - 1, slot_b)
    else:
        down_proj(nchunk - 1, slot_a)
    out = x1_ref[...] + acc_ref[...]
    o_ref[...] = out if gf_ref is None else _rms(out, gf_ref[...])


def _ffn(h2, x1, wg, wu, cg, cu, wd, seq, g_final=None):
    T = x1.shape[0]
    tm = TM_FFN
    hal = BF16_SUBLANES
    tn = wg.shape[2]
    row = lambda i: (i, 0)
    c3 = lambda i: (0, 0, 0)
    final_norm = g_final is not None
    in_specs = [
        pl.BlockSpec((tm, D_MODEL), row),
        pl.BlockSpec((hal, D_MODEL), lambda i: (jnp.maximum(i * (tm // hal) - 1, 0), 0)),
        pl.BlockSpec((tm, D_MODEL), row),
        _resident(wg.shape, c3),
        _resident(wu.shape, c3),
        _resident(cg.shape, c3),
        _resident(cu.shape, c3),
        _resident(wd.shape, c3),
    ]
    args = [h2, h2, x1, wg, wu, cg, cu, wd]
    if final_norm:
        in_specs.append(pl.BlockSpec((1, D_MODEL), lambda i: (0, 0)))
        args.append(g_final)
    return pl.pallas_call(
        functools.partial(_ffn_kernel, blocks_per_seq=seq // tm, final_norm=final_norm),
        grid=(T // tm,),
        in_specs=in_specs,
        out_specs=pl.BlockSpec((tm, D_MODEL), row),
        out_shape=jax.ShapeDtypeStruct((T, D_MODEL), F32),
        scratch_shapes=[pltpu.VMEM((hal + tm, tn), F32)] * 4 + [pltpu.VMEM((tm, D_MODEL), F32)],
        compiler_params=_cparams(("parallel",)),
        name="ffn",
    )(*args)


def _rope_tables(seq):
    def base(dim):
        inv = jnp.power(jnp.float32(ROPE_THETA), -jnp.arange(0, dim, 2, dtype=F32) / dim)
        ang = jnp.arange(seq, dtype=F32)[:, None] * inv[None, :]
        return jnp.cos(ang), jnp.sin(ang)

    def tables(dim, period):
        cos, sin = base(dim)
        half = dim // 2
        pad = period - dim
        c = jnp.concatenate([cos, cos, jnp.ones((seq, pad), F32)], axis=1)
        sn = jnp.concatenate([-sin, jnp.zeros((seq, half + pad), F32)], axis=1)
        sp = jnp.concatenate([jnp.zeros((seq, half), F32), sin, jnp.zeros((seq, pad), F32)], axis=1)
        reps = LANES // period
        return [jnp.tile(t, (1, reps)) for t in (c, sn, sp)]

    tabs = (tables(MLA_ROPE, LANES) + tables(HEAD_DIM // ROT_FRAC, HEAD_DIM)
            + tables(IDX_DIM // ROT_FRAC, IDX_DIM))
    return jnp.stack(tabs)


def _pad_cols(w, width):
    return jnp.pad(w, ((0, 0), (0, width - w.shape[1])))


def _layer_weights(w_in, w_uq, w_ukv, w_up, conv_w, conv_b, w_down):
    o = 0
    parts = {}
    for name, width in (("cq", Q_LORA), ("ckv", KV_LORA), ("kpe", MLA_ROPE),
                        ("dil", 3 * HEADS_DIL * HEAD_DIM), ("dsa", 3 * HEADS_DSA * HEAD_DIM),
                        ("qi", IDX_HEADS * IDX_DIM), ("ki", IDX_DIM), ("wi", IDX_HEADS)):
        parts[name] = w_in[:, o:o + width]
        o += width
    w1 = jnp.concatenate([
        parts["cq"], parts["ckv"], _pad_cols(parts["kpe"], LANES), parts["dil"], parts["dsa"],
        parts["qi"], jnp.tile(parts["ki"], (1, IDX_HEADS)), _pad_cols(parts["wi"], LANES)],
        axis=1).astype(BF16)
    assert w1.shape[1] == N_PROJ

    dq = MLA_NOPE + MLA_ROPE
    cols = []
    for hd in range(HEADS_MLA):
        nope = w_uq[:, dq * hd:dq * hd + MLA_NOPE]
        pe = w_uq[:, dq * hd + MLA_NOPE:dq * (hd + 1)]
        z = jnp.zeros((Q_LORA, MLA_NOPE), w_uq.dtype)
        slot = [nope, z] if hd % 2 == 0 else [z, nope]
        cols += slot + [_pad_cols(pe, LANES)]
    wuq = jnp.concatenate(cols, axis=1).astype(BF16)

    dkv = MLA_NOPE + HEAD_DIM
    kn = [w_ukv[:, dkv * hd:dkv * hd + MLA_NOPE] for hd in range(HEADS_MLA)]
    vv = [w_ukv[:, dkv * hd + MLA_NOPE:dkv * (hd + 1)] for hd in range(HEADS_MLA)]
    wukv = jnp.concatenate(kn + vv, axis=1).astype(BF16)

    nchunk = D_FF // TN_FFN
    def chunks(w):
        return w.reshape(w.shape[0], nchunk, TN_FFN).transpose(1, 0, 2)
    wg = chunks(w_up[:, :D_FF]).astype(BF16)
    wu = chunks(w_up[:, D_FF:]).astype(BF16)
    cw = jnp.concatenate([conv_w, conv_b[None, :],
                          jnp.zeros((8 - CONV_WIDTH - 1, 2 * D_FF), F32)], axis=0)
    cg = chunks(cw[:, :D_FF])
    cu = chunks(cw[:, D_FF:])
    wd = w_down.reshape(nchunk, TN_FFN, D_MODEL).astype(BF16)
    return w1, wuq, wukv, wg, wu, cg, cu, wd


def kernel(x, g_attn, w_in, g_q_lat, w_uq, g_kv_lat, w_ukv, w_o, g_ffn, w_up, conv_w, conv_b,
           w_down, g_final):
    batch, seq, _ = x.shape
    depth = w_in.shape[0]
    assert seq % DIL_BAND == 0 and seq % TM_FFN == 0 and seq % TQ == 0
    assert D_FF % TN_FFN == 0
    T = batch * seq
    tabs = _rope_tables(seq)
    bias = _dil_bias(TQ)
    xf = x.reshape(T, D_MODEL)
    for l in range(depth):
        w1, wuq, wukv, wg, wu, cg, cu, wd = _layer_weights(
            w_in[l], w_uq[l], w_ukv[l], w_up[l], conv_w[l], conv_b[l], w_down[l])
        (mq, mk, mv, dq, dk, dv, sq, sk, sv, iq, ik, iw) = _projection(
            xf, g_attn[l][None, :], w1, g_q_lat[l][None, :], wuq, g_kv_lat[l][None, :], wukv,
            tabs, seq)
        oa = _mla_attention(mq, mk, mv, batch, seq)
        ob = _dil_attention(dq, dk, dv, bias, batch, seq)
        oc = _dsa_attention(iq, iw, ik, sq, sk, sv, batch, seq)
        x1, h2 = _out_projection(xf, oa, ob, oc, w_o[l].astype(BF16), g_ffn[l][None, :])
        xf = _ffn(h2, x1, wg, wu, cg, cu, wd, seq,
                  g_final=g_final[None, :] if l == depth - 1 else None)
    return xf.reshape(batch, seq, D_MODEL)
```

```python
import functools
import math

import jax
import jax.numpy as jnp
from jax import lax
from jax.experimental import pallas as pl
from jax.experimental.pallas import tpu as pltpu

D_MODEL = 1024
HEAD_DIM = 64
HEADS_MLA = 4
HEADS_DIL = 8
HEADS_DSA = 4
Q_LORA = 256
KV_LORA = 128
MLA_NOPE = 64
MLA_ROPE = 32
DIL_PAIRS = ((128, 1), (512, 4), (2048, 16))
IDX_HEADS = 8
IDX_DIM = 32
DSA_TOPK = 256
D_FF = 2816
CONV_WIDTH = 3
ROPE_THETA = 500000.0
ROT_FRAC = 4
NORM_EPS = 1e-6

LANES = 128
BF16_SUBLANES = 16
VMEM_LIMIT = 56 * 1024 * 1024
NEG = -1e30
LOG2E = math.log2(math.e)
F32 = jnp.float32
BF16 = jnp.bfloat16

TQ = 512
TM_ROW = 512
TM_FFN = 1024
TN_FFN = 256
MLA_UNROLL = 4
DIL_BAND = max(w for w, _ in DIL_PAIRS)

C_CQ = 0
C_CKV = C_CQ + Q_LORA
C_KPE = C_CKV + KV_LORA
C_DIL = C_KPE + LANES
C_DSA = C_DIL + 3 * HEADS_DIL * HEAD_DIM
C_QI = C_DSA + 3 * HEADS_DSA * HEAD_DIM
C_KI = C_QI + IDX_HEADS * IDX_DIM
C_WI = C_KI + IDX_HEADS * IDX_DIM
N_PROJ = C_WI + LANES


def _cparams(sem):
    return pltpu.CompilerParams(dimension_semantics=sem, vmem_limit_bytes=VMEM_LIMIT)


def _resident(shape, index_map):
    return pl.BlockSpec(shape, index_map, pipeline_mode=pl.Buffered(1))


def _rms(x, g):
    return x * lax.rsqrt(jnp.mean(x * x, axis=-1, keepdims=True) + NORM_EPS) * g


def _dot(a, b):
    return jnp.dot(a, b, preferred_element_type=F32)


def _dot_nt(a, b):
    return lax.dot_general(a, b, (((1,), (1,)), ((), ())), preferred_element_type=F32)


def _rope(x, tab_ref, t0, half):
    c = tab_ref[t0]
    sn = tab_ref[t0 + 1]
    sp = tab_ref[t0 + 2]
    return (x * c + pltpu.roll(x, LANES - half, axis=1) * sn
            + pltpu.roll(x, half, axis=1) * sp)


def _proj_kernel(x_ref, g_ref, w1_ref, gq_ref, wuq_ref, gkv_ref, wukv_ref, tab_ref,
                 mq_ref, mk_ref, mv_ref, dq_ref, dk_ref, dv_ref,
                 sq_ref, sk_ref, sv_ref, iq_ref, ik_ref, iw_ref):
    h = _rms(x_ref[...], g_ref[...]).astype(BF16)

    def proj(lo, width):
        return _dot(h, w1_ref[:, lo:lo + width])

    mla_scale = (MLA_NOPE + MLA_ROPE) ** -0.5 * LOG2E
    cq = _rms(proj(C_CQ, Q_LORA), gq_ref[...]).astype(BF16)
    for hd in range(HEADS_MLA):
        base = 2 * LANES * hd
        q2 = _dot(cq, wuq_ref[:, base:base + 2 * LANES])
        qp = _rope(q2[:, LANES:], tab_ref, 0, MLA_ROPE // 2)
        mq_ref[:, base:base + LANES] = (q2[:, :LANES] * mla_scale).astype(BF16)
        mq_ref[:, base + LANES:base + 2 * LANES] = (qp * mla_scale).astype(BF16)
    zkv = proj(C_CKV, KV_LORA + LANES)
    ckv = _rms(zkv[:, :KV_LORA], gkv_ref[...]).astype(BF16)
    kpe = _rope(zkv[:, KV_LORA:], tab_ref, 0, MLA_ROPE // 2).astype(BF16)
    kv = _dot(ckv, wukv_ref[...])
    for pr in range(HEADS_MLA // 2):
        mk_ref[:, 2 * LANES * pr:2 * LANES * pr + LANES] = kv[:, LANES * pr:LANES * (pr + 1)].astype(BF16)
        mk_ref[:, 2 * LANES * pr + LANES:2 * LANES * (pr + 1)] = kpe
    mv_ref[...] = kv[:, HEADS_MLA * MLA_NOPE:].astype(BF16)

    head_scale = HEAD_DIM ** -0.5 * LOG2E
    half_h = HEAD_DIM // ROT_FRAC // 2

    def roped(c0, width, t0, half, scale, o_ref):
        step = min(width, 2 * LANES)
        for c in range(0, width, step):
            z = proj(c0 + c, step)
            for off in range(0, step, LANES):
                r = _rope(z[:, off:off + LANES], tab_ref, t0, half)
                if scale is not None:
                    r = r * scale
                o_ref[:, c + off:c + off + LANES] = r.astype(BF16)

    def qkv(c0, nheads, q_ref, k_ref, v_ref):
        width = nheads * HEAD_DIM
        roped(c0, width, 3, half_h, head_scale, q_ref)
        roped(c0 + width, width, 3, half_h, None, k_ref)
        v_ref[...] = proj(c0 + 2 * width, width).astype(BF16)

    qkv(C_DIL, HEADS_DIL, dq_ref, dk_ref, dv_ref)
    qkv(C_DSA, HEADS_DSA, sq_ref, sk_ref, sv_ref)

    half_i = IDX_DIM // ROT_FRAC // 2
    roped(C_QI, IDX_HEADS * IDX_DIM, 6, half_i, None, iq_ref)
    roped(C_KI, IDX_HEADS * IDX_DIM, 6, half_i, None, ik_ref)
    iw_ref[...] = proj(C_WI, LANES) * ((IDX_HEADS * IDX_DIM) ** -0.5)


def _projection(x2, g, w1, gq, wuq, gkv, wukv, tabs, seq):
    T = x2.shape[0]
    tm = TM_ROW
    nsb = seq // tm
    row = lambda i: (i, 0)
    const = lambda i: (0, 0)
    widths = [2 * LANES * HEADS_MLA, 2 * LANES * (HEADS_MLA // 2), HEADS_MLA * HEAD_DIM,
              HEADS_DIL * HEAD_DIM, HEADS_DIL * HEAD_DIM, HEADS_DIL * HEAD_DIM,
              HEADS_DSA * HEAD_DIM, HEADS_DSA * HEAD_DIM, HEADS_DSA * HEAD_DIM,
              IDX_HEADS * IDX_DIM, IDX_HEADS * IDX_DIM]
    out_shape = [jax.ShapeDtypeStruct((T, w), BF16) for w in widths]
    out_shape.append(jax.ShapeDtypeStruct((T, LANES), F32))
    out_specs = [pl.BlockSpec((tm, w), row) for w in widths] + [pl.BlockSpec((tm, LANES), row)]
    return pl.pallas_call(
        _proj_kernel,
        grid=(T // tm,),
        in_specs=[
            pl.BlockSpec((tm, D_MODEL), row),
            pl.BlockSpec((1, D_MODEL), const),
            pl.BlockSpec(w1.shape, const),
            pl.BlockSpec((1, Q_LORA), const),
            pl.BlockSpec(wuq.shape, const),
            pl.BlockSpec((1, KV_LORA), const),
            pl.BlockSpec(wukv.shape, const),
            pl.BlockSpec((9, tm, LANES), lambda i: (0, i % nsb, 0)),
        ],
        out_specs=out_specs,
        out_shape=out_shape,
        compiler_params=_cparams(("parallel",)),
        name="proj",
    )(x2, g, w1, gq, wuq, gkv, wukv, tabs)


def _rep(x, n):
    return x if n == 1 else jnp.concatenate([x] * n, axis=1)


def _flash_tile(q_stack, k, v, nheads, mask_fn, m_ref, l_ref, acc_ref):
    tq = q_stack.shape[0] // nheads
    tk = k.shape[0]
    dv = v.shape[1]
    s_all = _dot_nt(q_stack, k)
    ps = []
    alphas = []
    for h in range(nheads):
        s = mask_fn(s_all[h * tq:(h + 1) * tq])
        m_prev = m_ref[h]
        m_new = jnp.maximum(m_prev, jnp.max(s, axis=1, keepdims=True))
        alpha = jnp.exp2(m_prev - m_new)
        p = jnp.exp2(s - _rep(m_new, tk // LANES))
        l_ref[h] = alpha * l_ref[h] + jnp.sum(p, axis=1, keepdims=True)
        m_ref[h] = m_new
        ps.append(p.astype(BF16))
        alphas.append(alpha)
    pv = _dot(jnp.concatenate(ps, axis=0), v)
    for h in range(nheads):
        acc_ref[h] = _rep(alphas[h], dv // LANES) * acc_ref[h] + pv[h * tq:(h + 1) * tq]


def _flash_init(m_ref, l_ref, acc_ref):
    m_ref[...] = jnp.full(m_ref.shape, NEG, F32)
    l_ref[...] = jnp.zeros(l_ref.shape, F32)
    acc_ref[...] = jnp.zeros(acc_ref.shape, F32)


def _flash_out(nheads, l_ref, acc_ref):
    tq, dv = acc_ref.shape[1:]
    lane = lax.broadcasted_iota(jnp.int32, (tq, dv), 1)
    out = jnp.zeros((tq, dv), F32)
    for h in range(nheads):
        out = jnp.where((lane // HEAD_DIM) == h, acc_ref[h] / _rep(l_ref[h], dv // LANES), out)
    return out


def _slab(x):
    return jnp.broadcast_to(x, (8, x.shape[1]))


def _to_rows(x, width):
    tq = x.shape[1]
    return _rep(jnp.transpose(jnp.broadcast_to(x[:1], (LANES, tq))), width // LANES)


def _flash_tile_t(q_stack, k, v, nheads, mask_fn, m_ref, l_ref, acc_ref):
    tq = q_stack.shape[0] // nheads
    dv = v.shape[1]
    s_all = _dot_nt(k, q_stack)
    ps = []
    alphas = []
    for h in range(nheads):
        s = mask_fn(s_all[:, h * tq:(h + 1) * tq])
        m_prev = m_ref[h]
        m_new = jnp.maximum(m_prev, _slab(jnp.max(s, axis=0, keepdims=True)))
        alpha = jnp.exp2(m_prev - m_new)
        p = jnp.exp2(s - m_new[:1])
        l_ref[h] = alpha * l_ref[h] + _slab(jnp.sum(p, axis=0, keepdims=True))
        m_ref[h] = m_new
        ps.append(p.astype(BF16))
        alphas.append(alpha)
    pv = lax.dot_general(jnp.concatenate(ps, axis=1), v, (((0,), (0,)), ((), ())),
                         preferred_element_type=F32)
    for h in range(nheads):
        acc_ref[h] = _to_rows(alphas[h], dv) * acc_ref[h] + pv[h * tq:(h + 1) * tq]


def _flash_out_t(nheads, l_ref, acc_ref):
    tq, dv = acc_ref.shape[1:]
    lane = lax.broadcasted_iota(jnp.int32, (tq, dv), 1)
    out = jnp.zeros((tq, dv), F32)
    for h in range(nheads):
        out = jnp.where((lane // HEAD_DIM) == h, acc_ref[h] / _to_rows(l_ref[h], dv), out)
    return out


def _head_stack(q, nheads):
    lane = lax.broadcasted_iota(jnp.int32, q.shape, 1)
    zero = jnp.zeros_like(q)
    return jnp.concatenate([jnp.where((lane // HEAD_DIM) == h, q, zero) for h in range(nheads)], axis=0)


def _causal_tile(tq):
    r = lax.broadcasted_iota(jnp.int32, (tq, tq), 0)
    c = lax.broadcasted_iota(jnp.int32, (tq, tq), 1)
    return r >= c


def _mla_kernel(q_ref, k_ref, v_ref, o_ref, m_ref, l_ref, acc_ref):
    tq = q_ref.shape[0]
    i = pl.program_id(2)
    _flash_init(m_ref, l_ref, acc_ref)
    q_stack = jnp.concatenate([q_ref[:, :2 * LANES], q_ref[:, 2 * LANES:]], axis=0)

    def step(j, mask_fn):
        start = pl.multiple_of(j * tq, tq)
        _flash_tile(q_stack, k_ref[pl.ds(start, tq), :], v_ref[pl.ds(start, tq), :], 2,
                    mask_fn, m_ref, l_ref, acc_ref)

    def group(g, carry):
        for u in range(MLA_UNROLL):
            step(g * MLA_UNROLL + u, lambda s: s)
        return carry

    def single(j, carry):
        step(j, lambda s: s)
        return carry

    ngroup = i // MLA_UNROLL
    lax.fori_loop(0, ngroup, group, 0)
    lax.fori_loop(ngroup * MLA_UNROLL, i, single, 0)
    step(i, lambda s: jnp.where(_causal_tile(tq), s, NEG))
    o_ref[...] = _flash_out(2, l_ref, acc_ref).astype(o_ref.dtype)


def _mla_attention(mq, mk, mv, batch, seq):
    T = mq.shape[0]
    tq = TQ
    nq = seq // tq
    npair = HEADS_MLA // 2
    return pl.pallas_call(
        _mla_kernel,
        grid=(batch, npair, nq),
        in_specs=[
            pl.BlockSpec((tq, 4 * LANES), lambda b, p, i: (b * nq + i, p)),
            _resident((seq, 2 * LANES), lambda b, p, i: (b, p)),
            _resident((seq, LANES), lambda b, p, i: (b, p)),
        ],
        out_specs=pl.BlockSpec((tq, LANES), lambda b, p, i: (b * nq + i, p)),
        out_shape=jax.ShapeDtypeStruct((T, HEADS_MLA * HEAD_DIM), BF16),
        scratch_shapes=[pltpu.VMEM((2, tq, LANES), F32)] * 3,
        compiler_params=_cparams(("parallel", "parallel", "arbitrary")),
        name="mla_attn",
    )(mq, mk, mv)


def _dil_bias(tq):
    nband = DIL_BAND // tq + 1
    d = (lax.broadcasted_iota(jnp.int32, (nband, tq, tq), 0) * tq
         + lax.broadcasted_iota(jnp.int32, (nband, tq, tq), 1)
         - lax.broadcasted_iota(jnp.int32, (nband, tq, tq), 2))
    cnt = jnp.zeros(d.shape, F32)
    for window, dil in DIL_PAIRS:
        ok = (d >= 0) & (d % dil == 0) & (d <= (window // dil) * dil)
        cnt = cnt + ok.astype(F32)
    return jnp.where(cnt > 0, jnp.log2(jnp.maximum(cnt, 1.0)), NEG)


def _dil_kernel(q_ref, k_ref, v_ref, bias_ref, o_ref, m_ref, l_ref, acc_ref):
    tq = q_ref.shape[0]
    nband = bias_ref.shape[0]
    i = pl.program_id(2)
    _flash_init(m_ref, l_ref, acc_ref)
    q_stack = _head_stack(q_ref[...], 2)

    def step(delta):
        start = pl.multiple_of((i - delta) * tq, tq)
        bias = bias_ref[delta]
        _flash_tile(q_stack, k_ref[pl.ds(start, tq), :], v_ref[pl.ds(start, tq), :], 2,
                    lambda s: s + bias, m_ref, l_ref, acc_ref)

    @pl.when(i >= nband - 1)
    def _():
        for delta in range(nband):
            step(delta)

    @pl.when(i < nband - 1)
    def _():
        step(0)

        def body(delta, carry):
            step(delta)
            return carry

        lax.fori_loop(1, i + 1, body, 0)

    o_ref[...] = _flash_out(2, l_ref, acc_ref).astype(o_ref.dtype)


def _dil_attention(dq, dk, dv, bias, batch, seq):
    T = dq.shape[0]
    tq = TQ
    nq = seq // tq
    npair = HEADS_DIL // 2
    return pl.pallas_call(
        _dil_kernel,
        grid=(batch, npair, nq),
        in_specs=[
            pl.BlockSpec((tq, LANES), lambda b, p, i: (b * nq + i, p)),
            _resident((seq, LANES), lambda b, p, i: (b, p)),
            _resident((seq, LANES), lambda b, p, i: (b, p)),
            _resident(bias.shape, lambda b, p, i: (0, 0, 0)),
        ],
        out_specs=pl.BlockSpec((tq, LANES), lambda b, p, i: (b * nq + i, p)),
        out_shape=jax.ShapeDtypeStruct((T, HEADS_DIL * HEAD_DIM), BF16),
        scratch_shapes=[pltpu.VMEM((2, tq, LANES), F32)] * 3,
        compiler_params=_cparams(("parallel", "parallel", "arbitrary")),
        name="dil_attn",
    )(dq, dk, dv, bias)


_INT_MIN = -(2 ** 31)
_KEY_NEG_INF = _INT_MIN + 0x7FFFFF
INTERP_MIN_WIDTH = 4096
INTERP_MIN_INDEX = 32


def _ordered_bits(x):
    return x ^ ((x >> 31) & 0x7FFFFFFF)


def _key_to_float(key):
    return lax.bitcast_convert_type(_ordered_bits(key), F32)


def _float_to_key(x):
    return _ordered_bits(lax.bitcast_convert_type(x, jnp.int32))


def _dsa_kernel(iq_ref, iwt_ref, ik_ref, q_ref, k_ref, v_ref, o_ref,
                sc_ref, cm_ref, m_ref, l_ref, acc_ref, *, ksel):
    tq = q_ref.shape[0]
    i = pl.program_id(1)
    ntile = i + 1
    key_row = lax.broadcasted_iota(jnp.int32, (tq, tq), 0)
    causal = key_row <= lax.broadcasted_iota(jnp.int32, (tq, tq), 1)
    neg_inf = jnp.float32(-jnp.inf)
    as_f32 = lambda v: lax.bitcast_convert_type(v, F32)
    as_i32 = lambda v: lax.bitcast_convert_type(v, jnp.int32)

    iq_stack = jnp.concatenate(
        [jnp.where((lax.broadcasted_iota(jnp.int32, iq_ref.shape, 1) // IDX_DIM) == h,
                   iq_ref[...], jnp.zeros(iq_ref.shape, BF16)) for h in range(IDX_HEADS)], axis=0)
    wt = iwt_ref[...]
    cm_ref[...] = jnp.full(cm_ref.shape, neg_inf, F32)

    def score_tile(j, diag):
        kk = ik_ref[pl.ds(pl.multiple_of(j * tq, tq), tq), :]
        sc = jnp.zeros((tq, tq), F32)
        for half in range(2):
            nh = IDX_HEADS // 2
            d = _dot_nt(kk, iq_stack[half * nh * tq:(half + 1) * nh * tq])
            for hh in range(nh):
                h = half * nh + hh
                sc = sc + jnp.maximum(d[:, hh * tq:(hh + 1) * tq], 0.0) * wt[h:h + 1, :]
        if diag:
            sc = jnp.where(causal, sc, neg_inf)
        sc_ref[j] = as_i32(sc)
        cm_ref[...] = jnp.maximum(cm_ref[...], sc)

    def score_body(j, carry):
        score_tile(j, False)
        return carry

    lax.fori_loop(0, i, score_body, 0)
    score_tile(i, True)

    def count_cols(hit, thr):
        def body(j, cnt):
            words = sc_ref[j]
            for g in range(tq // 8):
                cnt = cnt + jnp.where(hit(words[8 * g:8 * (g + 1)], thr), 1, 0)
            return cnt

        cnt = lax.fori_loop(0, ntile, body, jnp.zeros((8, tq), jnp.int32))
        return _slab(jnp.sum(cnt, axis=0, keepdims=True))

    def count_ge(thr):
        return count_cols(lambda words, t: as_f32(words) >= t, thr)

    cm = cm_ref[...]
    lo = jnp.maximum(_float_to_key(_slab(jnp.min(cm, axis=0, keepdims=True))), _KEY_NEG_INF)
    hi = _float_to_key(_slab(jnp.max(cm, axis=0, keepdims=True))) + 1
    clo = jnp.full((8, tq), 2 ** 30, jnp.int32)
    chi = jnp.zeros((8, tq), jnp.int32)
    key_zero = 0
    key_tiny = 0x00800000
    hi = jnp.where((hi > key_zero) & (hi < key_tiny), key_tiny, hi)

    def open_rows(lo, hi, clo):
        zero_class = (lo == key_zero) & (hi == key_tiny)
        return (clo != ksel) & (hi - 1 > lo) & jnp.logical_not(zero_class)

    def any_rows(mask):
        return jnp.max(jnp.where(mask, 1, 0))

    def probe_key(lo, hi, clo, chi):
        mid = (lo >> 1) + (hi >> 1) + (lo & hi & 1)
        straddle = (lo < key_zero) & (hi > key_zero)
        width = hi - lo
        lo_f = _key_to_float(lo)
        frac = (clo - ksel).astype(F32) / (clo - chi).astype(F32)
        guess = _float_to_key(lo_f + (_key_to_float(hi) - lo_f) * frac)
        margin = width >> 3
        guess = jnp.minimum(jnp.maximum(guess, lo + 1 + margin), hi - 1 - margin)
        use = ((clo < 2 ** 30) & (width > INTERP_MIN_WIDTH) & (lo > _KEY_NEG_INF)
               & jnp.logical_not(straddle))
        probe = jnp.where(use, guess, mid)
        probe = jnp.where(straddle, key_zero, probe)
        return jnp.where((lo == key_zero) & (hi > key_tiny), key_tiny, probe)

    def sel_body(st):
        _, lo, hi, clo, chi = st
        mid = probe_key(lo, hi, clo, chi)
        cnt = count_ge(_key_to_float(mid))
        ge = cnt >= ksel
        act = open_rows(lo, hi, clo)
        up = act & ge
        dn = act & jnp.logical_not(ge)
        lo = jnp.where(up, mid, lo)
        clo = jnp.where(up, cnt, clo)
        hi = jnp.where(dn, mid, hi)
        chi = jnp.where(dn, cnt, chi)
        return any_rows(open_rows(lo, hi, clo)), lo, hi, clo, chi

    _, lo, hi, clo, chi = lax.while_loop(
        lambda st: st[0] > 0, sel_body, (any_rows(open_rows(lo, hi, clo)), lo, hi, clo, chi))
    exact = clo == ksel
    upper = _key_to_float(jnp.where(exact, lo, hi))[:1]
    lower = _key_to_float(lo)[:1]

    big = jnp.int32(2 ** 30)

    def rank_body(j, carry):
        sc = as_f32(sc_ref[j])
        sc_ref[j] = jnp.where(sc >= upper, -1, jnp.where(sc >= lower, key_row + j * tq, big))
        return carry

    lax.fori_loop(0, ntile, rank_body, 0)

    ja = jnp.where(exact, -1, 0)
    jb = jnp.where(exact, 0, ntile * tq)
    ca = jnp.where(exact, ksel, chi)
    cb = clo

    def idx_body(st):
        _, ja, jb, ca, cb = st
        width = jb - ja
        frac = (ksel - ca).astype(F32) / (cb - ca).astype(F32)
        guess = ja + (width.astype(F32) * frac).astype(jnp.int32)
        margin = width >> 3
        guess = jnp.minimum(jnp.maximum(guess, ja + 1 + margin), jb - 1 - margin)
        mid = jnp.where((cb < 2 ** 30) & (width > INTERP_MIN_INDEX), guess, (ja + jb) >> 1)
        cnt = count_cols(lambda words, t: words < t, mid)
        act = width > 1
        hit = act & (cnt == ksel)
        dn = act & (cnt >= ksel)
        up = act & (cnt < ksel)
        jb = jnp.where(dn, mid, jb)
        cb = jnp.where(dn, cnt, cb)
        ja = jnp.where(hit, mid - 1, jnp.where(up, mid, ja))
        ca = jnp.where(up, cnt, ca)
        return any_rows(jb - ja > 1), ja, jb, ca, cb

    _, _, jb, _, _ = lax.while_loop(lambda st: st[0] > 0, idx_body,
                                   (any_rows(jb - ja > 1), ja, jb, ca, cb))
    bound = jb[:1]

    _flash_init(m_ref, l_ref, acc_ref)
    q_stack = _head_stack(q_ref[...], HEADS_DSA)

    def attn_tile(j, diag):
        start = pl.multiple_of(j * tq, tq)
        sel = sc_ref[j] < bound
        if diag:
            sel = sel & causal
        cap = jnp.where(sel, -NEG, NEG)
        _flash_tile_t(q_stack, k_ref[pl.ds(start, tq), :], v_ref[pl.ds(start, tq), :], HEADS_DSA,
                      lambda s: jnp.minimum(s, cap), m_ref, l_ref, acc_ref)

    def attn_body(j, carry):
        attn_tile(j, False)
        return carry

    lax.fori_loop(0, i, attn_body, 0)
    attn_tile(i, True)
    o_ref[...] = _flash_out_t(HEADS_DSA, l_ref, acc_ref).astype(o_ref.dtype)


def _dsa_attention(iq, iw, ik, sq, sk, sv, batch, seq):
    T = sq.shape[0]
    tq = TQ
    nq = seq // tq
    ksel = min(DSA_TOPK, seq // 4)
    assert tq >= ksel
    width = HEADS_DSA * HEAD_DIM
    qrow = lambda b, i: (b * nq + i, 0)
    full = lambda b, i: (b, 0)
    return pl.pallas_call(
        functools.partial(_dsa_kernel, ksel=ksel),
        grid=(batch, nq),
        in_specs=[
            pl.BlockSpec((tq, IDX_HEADS * IDX_DIM), qrow),
            pl.BlockSpec((IDX_HEADS, tq), lambda b, i: (0, b * nq + i)),
            _resident((seq, IDX_HEADS * IDX_DIM), full),
            pl.BlockSpec((tq, width), qrow),
            _resident((seq, width), full),
            _resident((seq, width), full),
        ],
        out_specs=pl.BlockSpec((tq, width), qrow),
        out_shape=jax.ShapeDtypeStruct((T, width), BF16),
        scratch_shapes=[
            pltpu.VMEM((nq, tq, tq), jnp.int32),
            pltpu.VMEM((tq, tq), F32),
            pltpu.VMEM((HEADS_DSA, 8, tq), F32),
            pltpu.VMEM((HEADS_DSA, 8, tq), F32),
            pltpu.VMEM((HEADS_DSA, tq, width), F32),
        ],
        compiler_params=_cparams(("parallel", "arbitrary")),
        name="dsa_attn",
    )(iq, jnp.transpose(iw[:, :IDX_HEADS]), ik, sq, sk, sv)


def _oproj_kernel(x_ref, oa_ref, ob_ref, oc_ref, wo_ref, g_ref, x1_ref, h_ref):
    na = HEADS_MLA * HEAD_DIM
    nb = HEADS_DIL * HEAD_DIM
    x1 = (x_ref[...] + _dot(oa_ref[...], wo_ref[:na, :])
          + _dot(ob_ref[...], wo_ref[na:na + nb, :])
          + _dot(oc_ref[...], wo_ref[na + nb:, :]))
    x1_ref[...] = x1
    h_ref[...] = _rms(x1, g_ref[...]).astype(BF16)


def _out_projection(x2, oa, ob, oc, wo, g):
    T = x2.shape[0]
    tm = TM_FFN
    row = lambda i: (i, 0)
    const = lambda i: (0, 0)
    return pl.pallas_call(
        _oproj_kernel,
        grid=(T // tm,),
        in_specs=[
            pl.BlockSpec((tm, D_MODEL), row),
            pl.BlockSpec((tm, oa.shape[1]), row),
            pl.BlockSpec((tm, ob.shape[1]), row),
            pl.BlockSpec((tm, oc.shape[1]), row),
            pl.BlockSpec(wo.shape, const),
            pl.BlockSpec((1, D_MODEL), const),
        ],
        out_specs=[pl.BlockSpec((tm, D_MODEL), row), pl.BlockSpec((tm, D_MODEL), row)],
        out_shape=[jax.ShapeDtypeStruct((T, D_MODEL), F32), jax.ShapeDtypeStruct((T, D_MODEL), BF16)],
        compiler_params=_cparams(("parallel",)),
        name="oproj",
    )(x2, oa, ob, oc, wo, g)


def _ffn_kernel(h_ref, halo_ref, x1_ref, wg_ref, wu_ref, cg_ref, cu_ref, wd_ref, *rest,
                blocks_per_seq, final_norm):
    gf_ref = rest[0] if final_norm else None
    o_ref, uga_ref, uua_ref, ugb_ref, uub_ref, acc_ref = rest[1:] if final_norm else rest
    tm = h_ref.shape[0]
    hal = halo_ref.shape[0]
    nchunk = wg_ref.shape[0]
    first = (pl.program_id(0) % blocks_per_seq) == 0
    keep = jnp.where(first, 0.0, 1.0).astype(F32)
    h = h_ref[...]
    hh = halo_ref[...]
    acc_ref[...] = jnp.zeros(acc_ref.shape, F32)

    def conv(u_ref, w):
        y = w[CONV_WIDTH:CONV_WIDTH + 1, :]
        for tap in range(CONV_WIDTH):
            off = hal - (CONV_WIDTH - 1) + tap
            y = y + u_ref[pl.ds(off, tm), :] * w[tap:tap + 1, :]
        return y

    def up_proj(j, slot):
        ug_ref, uu_ref = slot
        wg = wg_ref[j]
        wu = wu_ref[j]
        ug_ref[:hal, :] = _dot(hh, wg) * keep
        ug_ref[hal:, :] = _dot(h, wg)
        uu_ref[:hal, :] = _dot(hh, wu) * keep
        uu_ref[hal:, :] = _dot(h, wu)

    def down_proj(j, slot):
        ug_ref, uu_ref = slot
        gate = conv(ug_ref, cg_ref[j])
        up = conv(uu_ref, cu_ref[j])
        act = gate * jax.nn.sigmoid(gate) * up
        acc_ref[...] += _dot(act.astype(BF16), wd_ref[j])

    slot_a = (uga_ref, uua_ref)
    slot_b = (ugb_ref, uub_ref)
    up_proj(0, slot_a)

    def chunk_pair(jj, carry):
        j = 2 * jj
        up_proj(j + 1, slot_b)
        down_proj(j, slot_a)
        up_proj(j + 2, slot_a)
        down_proj(j + 1, slot_b)
        return carry

    lax.fori_loop(0, (nchunk - 1) // 2, chunk_pair, 0)
    if nchunk % 2 == 0:
        up_proj(nchunk - 1, slot_b)
        down_proj(nchunk - 2, slot_a)
        down_proj(nchunk - 1, slot_b)
    else:
        down_proj(nchunk - 1, slot_a)
    out = x1_ref[...] + acc_ref[...]
    o_ref[...] = out if gf_ref is None else _rms(out, gf_ref[...])


def _ffn(h2, x1, wg, wu, cg, cu, wd, seq, g_final=None):
    T = x1.shape[0]
    tm = TM_FFN
    hal = BF16_SUBLANES
    tn = wg.shape[2]
    row = lambda i: (i, 0)
    c3 = lambda i: (0, 0, 0)
    final_norm = g_final is not None
    in_specs = [
        pl.BlockSpec((tm, D_MODEL), row),
        pl.BlockSpec((hal, D_MODEL), lambda i: (jnp.maximum(i * (tm // hal) - 1, 0), 0)),
        pl.BlockSpec((tm, D_MODEL), row),
        _resident(wg.shape, c3),
        _resident(wu.shape, c3),
        _resident(cg.shape, c3),
        _resident(cu.shape, c3),
        _resident(wd.shape, c3),
    ]
    args = [h2, h2, x1, wg, wu, cg, cu, wd]
    if final_norm:
        in_specs.append(pl.BlockSpec((1, D_MODEL), lambda i: (0, 0)))
        args.append(g_final)
    return pl.pallas_call(
        functools.partial(_ffn_kernel, blocks_per_seq=seq // tm, final_norm=final_norm),
        grid=(T // tm,),
        in_specs=in_specs,
        out_specs=pl.BlockSpec((tm, D_MODEL), row),
        out_shape=jax.ShapeDtypeStruct((T, D_MODEL), F32),
        scratch_shapes=[pltpu.VMEM((hal + tm, tn), F32)] * 4 + [pltpu.VMEM((tm, D_MODEL), F32)],
        compiler_params=_cparams(("parallel",)),
        name="ffn",
    )(*args)


def _rope_tables(seq):
    def base(dim):
        inv = jnp.power(jnp.float32(ROPE_THETA), -jnp.arange(0, dim, 2, dtype=F32) / dim)
        ang = jnp.arange(seq, dtype=F32)[:, None] * inv[None, :]
        return jnp.cos(ang), jnp.sin(ang)

    def tables(dim, period):
        cos, sin = base(dim)
        half = dim // 2
        pad = period - dim
        c = jnp.concatenate([cos, cos, jnp.ones((seq, pad), F32)], axis=1)
        sn = jnp.concatenate([-sin, jnp.zeros((seq, half + pad), F32)], axis=1)
        sp = jnp.concatenate([jnp.zeros((seq, half), F32), sin, jnp.zeros((seq, pad), F32)], axis=1)
        reps = LANES // period
        return [jnp.tile(t, (1, reps)) for t in (c, sn, sp)]

    tabs = (tables(MLA_ROPE, LANES) + tables(HEAD_DIM // ROT_FRAC, HEAD_DIM)
            + tables(IDX_DIM // ROT_FRAC, IDX_DIM))
    return jnp.stack(tabs)


def _pad_cols(w, width):
    return jnp.pad(w, ((0, 0), (0, width - w.shape[1])))


def _layer_weights(w_in, w_uq, w_ukv, w_up, conv_w, conv_b, w_down):
    o = 0
    parts = {}
    for name, width in (("cq", Q_LORA), ("ckv", KV_LORA), ("kpe", MLA_ROPE),
                        ("dil", 3 * HEADS_DIL * HEAD_DIM), ("dsa", 3 * HEADS_DSA * HEAD_DIM),
                        ("qi", IDX_HEADS * IDX_DIM), ("ki", IDX_DIM), ("wi", IDX_HEADS)):
        parts[name] = w_in[:, o:o + width]
        o += width
    w1 = jnp.concatenate([
        parts["cq"], parts["ckv"], _pad_cols(parts["kpe"], LANES), parts["dil"], parts["dsa"],
        parts["qi"], jnp.tile(parts["ki"], (1, IDX_HEADS)), _pad_cols(parts["wi"], LANES)],
        axis=1).astype(BF16)
    assert w1.shape[1] == N_PROJ

    dq = MLA_NOPE + MLA_ROPE
    cols = []
    for hd in range(HEADS_MLA):
        nope = w_uq[:, dq * hd:dq * hd + MLA_NOPE]
        pe = w_uq[:, dq * hd + MLA_NOPE:dq * (hd + 1)]
        z = jnp.zeros((Q_LORA, MLA_NOPE), w_uq.dtype)
        slot = [nope, z] if hd % 2 == 0 else [z, nope]
        cols += slot + [_pad_cols(pe, LANES)]
    wuq = jnp.concatenate(cols, axis=1).astype(BF16)

    dkv = MLA_NOPE + HEAD_DIM
    kn = [w_ukv[:, dkv * hd:dkv * hd + MLA_NOPE] for hd in range(HEADS_MLA)]
    vv = [w_ukv[:, dkv * hd + MLA_NOPE:dkv * (hd + 1)] for hd in range(HEADS_MLA)]
    wukv = jnp.concatenate(kn + vv, axis=1).astype(BF16)

    nchunk = D_FF // TN_FFN
    def chunks(w):
        return w.reshape(w.shape[0], nchunk, TN_FFN).transpose(1, 0, 2)
    wg = chunks(w_up[:, :D_FF]).astype(BF16)
    wu = chunks(w_up[:, D_FF:]).astype(BF16)
    cw = jnp.concatenate([conv_w, conv_b[None, :],
                          jnp.zeros((8 - CONV_WIDTH - 1, 2 * D_FF), F32)], axis=0)
    cg = chunks(cw[:, :D_FF])
    cu = chunks(cw[:, D_FF:])
    wd = w_down.reshape(nchunk, TN_FFN, D_MODEL).astype(BF16)
    return w1, wuq, wukv, wg, wu, cg, cu, wd


def kernel(x, g_attn, w_in, g_q_lat, w_uq, g_kv_lat, w_ukv, w_o, g_ffn, w_up, conv_w, conv_b,
           w_down, g_final):
    batch, seq, _ = x.shape
    depth = w_in.shape[0]
    assert seq % DIL_BAND == 0 and seq % TM_FFN == 0 and seq % TQ == 0
    assert D_FF % TN_FFN == 0
    T = batch * seq
    tabs = _rope_tables(seq)
    bias = _dil_bias(TQ)
    xf = x.reshape(T, D_MODEL)
    for l in range(depth):
        w1, wuq, wukv, wg, wu, cg, cu, wd = _layer_weights(
            w_in[l], w_uq[l], w_ukv[l], w_up[l], conv_w[l], conv_b[l], w_down[l])
        (mq, mk, mv, dq, dk, dv, sq, sk, sv, iq, ik, iw) = _projection(
            xf, g_attn[l][None, :], w1, g_q_lat[l][None, :], wuq, g_kv_lat[l][None, :], wukv,
            tabs, seq)
        oa = _mla_attention(mq, mk, mv, batch, seq)
        ob = _dil_attention(dq, dk, dv, bias, batch, seq)
        oc = _dsa_attention(iq, iw, ik, sq, sk, sv, batch, seq)
        x1, h2 = _out_projection(xf, oa, ob, oc, w_o[l].astype(BF16), g_ffn[l][None, :])
        xf = _ffn(h2, x1, wg, wu, cg, cu, wd, seq,
                  g_final=g_final[None, :] if l == depth - 1 else None)
    return xf.reshape(batch, seq, D_MODEL)
```

```python
import functools
import math

import jax
import jax.numpy as jnp
from jax import lax
from jax.experimental import pallas as pl
from jax.experimental.pallas import tpu as pltpu

D_MODEL = 1024
HEAD_DIM = 64
HEADS_MLA = 4
HEADS_DIL = 8
HEADS_DSA = 4
Q_LORA = 256
KV_LORA = 128
MLA_NOPE = 64
MLA_ROPE = 32
DIL_PAIRS = ((128, 1), (512, 4), (2048, 16))
IDX_HEADS = 8
IDX_DIM = 32
DSA_TOPK = 256
D_FF = 2816
CONV_WIDTH = 3
ROPE_THETA = 500000.0
ROT_FRAC = 4
NORM_EPS = 1e-6

LANES = 128
BF16_SUBLANES = 16
VMEM_LIMIT = 56 * 1024 * 1024
NEG = -1e30
LOG2E = math.log2(math.e)
F32 = jnp.float32
BF16 = jnp.bfloat16

TQ = 512
TM_ROW = 512
TM_FFN = 1024
TN_FFN = 256
MLA_UNROLL = 4
SCORE_UNROLL = 4
DIL_BAND = max(w for w, _ in DIL_PAIRS)

C_CQ = 0
C_CKV = C_CQ + Q_LORA
C_KPE = C_CKV + KV_LORA
C_DIL = C_KPE + LANES
C_DSA = C_DIL + 3 * HEADS_DIL * HEAD_DIM
C_QI = C_DSA + 3 * HEADS_DSA * HEAD_DIM
C_KI = C_QI + IDX_HEADS * IDX_DIM
C_WI = C_KI + IDX_HEADS * IDX_DIM
N_PROJ = C_WI + LANES


def _cparams(sem):
    return pltpu.CompilerParams(dimension_semantics=sem, vmem_limit_bytes=VMEM_LIMIT)


def _resident(shape, index_map):
    return pl.BlockSpec(shape, index_map, pipeline_mode=pl.Buffered(1))


def _rms(x, g):
    return x * lax.rsqrt(jnp.mean(x * x, axis=-1, keepdims=True) + NORM_EPS) * g


def _dot(a, b):
    return jnp.dot(a, b, preferred_element_type=F32)


def _dot_nt(a, b):
    return lax.dot_general(a, b, (((1,), (1,)), ((), ())), preferred_element_type=F32)


def _rope(x, tab_ref, t0, half):
    c = tab_ref[t0]
    sn = tab_ref[t0 + 1]
    sp = tab_ref[t0 + 2]
    return (x * c + pltpu.roll(x, LANES - half, axis=1) * sn
            + pltpu.roll(x, half, axis=1) * sp)


def _proj_kernel(x_ref, g_ref, w1_ref, gq_ref, wuq_ref, gkv_ref, wukv_ref, tab_ref,
                 mq_ref, mk_ref, mv_ref, dq_ref, dk_ref, dv_ref,
                 sq_ref, sk_ref, sv_ref, iq_ref, ik_ref, iw_ref):
    h = _rms(x_ref[...], g_ref[...]).astype(BF16)

    def proj(lo, width):
        return _dot(h, w1_ref[:, lo:lo + width])

    mla_scale = (MLA_NOPE + MLA_ROPE) ** -0.5 * LOG2E
    cq = _rms(proj(C_CQ, Q_LORA), gq_ref[...]).astype(BF16)
    for hd in range(HEADS_MLA):
        base = 2 * LANES * hd
        q2 = _dot(cq, wuq_ref[:, base:base + 2 * LANES])
        qp = _rope(q2[:, LANES:], tab_ref, 0, MLA_ROPE // 2)
        mq_ref[:, base:base + LANES] = (q2[:, :LANES] * mla_scale).astype(BF16)
        mq_ref[:, base + LANES:base + 2 * LANES] = (qp * mla_scale).astype(BF16)
    zkv = proj(C_CKV, KV_LORA + LANES)
    ckv = _rms(zkv[:, :KV_LORA], gkv_ref[...]).astype(BF16)
    kpe = _rope(zkv[:, KV_LORA:], tab_ref, 0, MLA_ROPE // 2).astype(BF16)
    kv = _dot(ckv, wukv_ref[...])
    for pr in range(HEADS_MLA // 2):
        mk_ref[:, 2 * LANES * pr:2 * LANES * pr + LANES] = kv[:, LANES * pr:LANES * (pr + 1)].astype(BF16)
        mk_ref[:, 2 * LANES * pr + LANES:2 * LANES * (pr + 1)] = kpe
    mv_ref[...] = kv[:, HEADS_MLA * MLA_NOPE:].astype(BF16)

    head_scale = HEAD_DIM ** -0.5 * LOG2E
    half_h = HEAD_DIM // ROT_FRAC // 2

    def roped(c0, width, t0, half, scale, o_ref):
        step = min(width, 2 * LANES)
        for c in range(0, width, step):
            z = proj(c0 + c, step)
            for off in range(0, step, LANES):
                r = _rope(z[:, off:off + LANES], tab_ref, t0, half)
                if scale is not None:
                    r = r * scale
                o_ref[:, c + off:c + off + LANES] = r.astype(BF16)

    def qkv(c0, nheads, q_ref, k_ref, v_ref):
        width = nheads * HEAD_DIM
        roped(c0, width, 3, half_h, head_scale, q_ref)
        roped(c0 + width, width, 3, half_h, None, k_ref)
        v_ref[...] = proj(c0 + 2 * width, width).astype(BF16)

    qkv(C_DIL, HEADS_DIL, dq_ref, dk_ref, dv_ref)
    qkv(C_DSA, HEADS_DSA, sq_ref, sk_ref, sv_ref)

    half_i = IDX_DIM // ROT_FRAC // 2
    roped(C_QI, IDX_HEADS * IDX_DIM, 6, half_i, None, iq_ref)
    roped(C_KI, IDX_HEADS * IDX_DIM, 6, half_i, None, ik_ref)
    iw_ref[...] = proj(C_WI, LANES) * ((IDX_HEADS * IDX_DIM) ** -0.5)


def _projection(x2, g, w1, gq, wuq, gkv, wukv, tabs, seq):
    T = x2.shape[0]
    tm = TM_ROW
    nsb = seq // tm
    row = lambda i: (i, 0)
    const = lambda i: (0, 0)
    widths = [2 * LANES * HEADS_MLA, 2 * LANES * (HEADS_MLA // 2), HEADS_MLA * HEAD_DIM,
              HEADS_DIL * HEAD_DIM, HEADS_DIL * HEAD_DIM, HEADS_DIL * HEAD_DIM,
              HEADS_DSA * HEAD_DIM, HEADS_DSA * HEAD_DIM, HEADS_DSA * HEAD_DIM,
              IDX_HEADS * IDX_DIM, IDX_HEADS * IDX_DIM]
    out_shape = [jax.ShapeDtypeStruct((T, w), BF16) for w in widths]
    out_shape.append(jax.ShapeDtypeStruct((T, LANES), F32))
    out_specs = [pl.BlockSpec((tm, w), row) for w in widths] + [pl.BlockSpec((tm, LANES), row)]
    return pl.pallas_call(
        _proj_kernel,
        grid=(T // tm,),
        in_specs=[
            pl.BlockSpec((tm, D_MODEL), row),
            pl.BlockSpec((1, D_MODEL), const),
            pl.BlockSpec(w1.shape, const),
            pl.BlockSpec((1, Q_LORA), const),
            pl.BlockSpec(wuq.shape, const),
            pl.BlockSpec((1, KV_LORA), const),
            pl.BlockSpec(wukv.shape, const),
            pl.BlockSpec((9, tm, LANES), lambda i: (0, i % nsb, 0)),
        ],
        out_specs=out_specs,
        out_shape=out_shape,
        compiler_params=_cparams(("parallel",)),
        name="proj",
    )(x2, g, w1, gq, wuq, gkv, wukv, tabs)


def _rep(x, n):
    return x if n == 1 else jnp.concatenate([x] * n, axis=1)


def _flash_tile(q_stack, k, v, nheads, mask_fn, m_ref, l_ref, acc_ref):
    tq = q_stack.shape[0] // nheads
    tk = k.shape[0]
    dv = v.shape[1]
    s_all = _dot_nt(q_stack, k)
    ps = []
    alphas = []
    for h in range(nheads):
        s = mask_fn(s_all[h * tq:(h + 1) * tq])
        m_prev = m_ref[h]
        m_new = jnp.maximum(m_prev, jnp.max(s, axis=1, keepdims=True))
        alpha = jnp.exp2(m_prev - m_new)
        p = jnp.exp2(s - _rep(m_new, tk // LANES))
        l_ref[h] = alpha * l_ref[h] + jnp.sum(p, axis=1, keepdims=True)
        m_ref[h] = m_new
        ps.append(p.astype(BF16))
        alphas.append(alpha)
    pv = _dot(jnp.concatenate(ps, axis=0), v)
    for h in range(nheads):
        acc_ref[h] = _rep(alphas[h], dv // LANES) * acc_ref[h] + pv[h * tq:(h + 1) * tq]


def _flash_init(m_ref, l_ref, acc_ref):
    m_ref[...] = jnp.full(m_ref.shape, NEG, F32)
    l_ref[...] = jnp.zeros(l_ref.shape, F32)
    acc_ref[...] = jnp.zeros(acc_ref.shape, F32)


def _flash_out(nheads, l_ref, acc_ref):
    tq, dv = acc_ref.shape[1:]
    lane = lax.broadcasted_iota(jnp.int32, (tq, dv), 1)
    out = jnp.zeros((tq, dv), F32)
    for h in range(nheads):
        out = jnp.where((lane // HEAD_DIM) == h, acc_ref[h] / _rep(l_ref[h], dv // LANES), out)
    return out


def _slab(x):
    return jnp.broadcast_to(x, (8, x.shape[1]))


def _to_rows(x, width):
    tq = x.shape[1]
    return _rep(jnp.transpose(jnp.broadcast_to(x[:1], (LANES, tq))), width // LANES)


def _flash_tile_t(q_stack, k, v, nheads, mask_fn, m_ref, l_ref, acc_ref):
    tq = q_stack.shape[0] // nheads
    dv = v.shape[1]
    s_all = _dot_nt(k, q_stack)
    ps = []
    alphas = []
    for h in range(nheads):
        s = mask_fn(s_all[:, h * tq:(h + 1) * tq])
        m_prev = m_ref[h]
        m_new = jnp.maximum(m_prev, _slab(jnp.max(s, axis=0, keepdims=True)))
        alpha = jnp.exp2(m_prev - m_new)
        p = jnp.exp2(s - m_new[:1])
        l_ref[h] = alpha * l_ref[h] + _slab(jnp.sum(p, axis=0, keepdims=True))
        m_ref[h] = m_new
        ps.append(p.astype(BF16))
        alphas.append(alpha)
    pv = lax.dot_general(jnp.concatenate(ps, axis=1), v, (((0,), (0,)), ((), ())),
                         preferred_element_type=F32)
    for h in range(nheads):
        acc_ref[h] = _to_rows(alphas[h], dv) * acc_ref[h] + pv[h * tq:(h + 1) * tq]


def _flash_out_t(nheads, l_ref, acc_ref):
    tq, dv = acc_ref.shape[1:]
    lane = lax.broadcasted_iota(jnp.int32, (tq, dv), 1)
    out = jnp.zeros((tq, dv), F32)
    for h in range(nheads):
        out = jnp.where((lane // HEAD_DIM) == h, acc_ref[h] / _to_rows(l_ref[h], dv), out)
    return out


def _head_stack(q, nheads):
    lane = lax.broadcasted_iota(jnp.int32, q.shape, 1)
    zero = jnp.zeros_like(q)
    return jnp.concatenate([jnp.where((lane // HEAD_DIM) == h, q, zero) for h in range(nheads)], axis=0)


def _causal_tile(tq):
    r = lax.broadcasted_iota(jnp.int32, (tq, tq), 0)
    c = lax.broadcasted_iota(jnp.int32, (tq, tq), 1)
    return r >= c


def _mla_kernel(q_ref, k_ref, v_ref, o_ref, m_ref, l_ref, acc_ref):
    tq = q_ref.shape[0]
    i = pl.program_id(2)
    _flash_init(m_ref, l_ref, acc_ref)
    q_stack = jnp.concatenate([q_ref[:, :2 * LANES], q_ref[:, 2 * LANES:]], axis=0)

    def step(j, mask_fn):
        start = pl.multiple_of(j * tq, tq)
        _flash_tile(q_stack, k_ref[pl.ds(start, tq), :], v_ref[pl.ds(start, tq), :], 2,
                    mask_fn, m_ref, l_ref, acc_ref)

    def group(g, carry):
        for u in range(MLA_UNROLL):
            step(g * MLA_UNROLL + u, lambda s: s)
        return carry

    def single(j, carry):
        step(j, lambda s: s)
        return carry

    ngroup = i // MLA_UNROLL
    lax.fori_loop(0, ngroup, group, 0)
    lax.fori_loop(ngroup * MLA_UNROLL, i, single, 0)
    step(i, lambda s: jnp.where(_causal_tile(tq), s, NEG))
    o_ref[...] = _flash_out(2, l_ref, acc_ref).astype(o_ref.dtype)


def _mla_attention(mq, mk, mv, batch, seq):
    T = mq.shape[0]
    tq = TQ
    nq = seq // tq
    npair = HEADS_MLA // 2
    return pl.pallas_call(
        _mla_kernel,
        grid=(batch, npair, nq),
        in_specs=[
            pl.BlockSpec((tq, 4 * LANES), lambda b, p, i: (b * nq + i, p)),
            _resident((seq, 2 * LANES), lambda b, p, i: (b, p)),
            _resident((seq, LANES), lambda b, p, i: (b, p)),
        ],
        out_specs=pl.BlockSpec((tq, LANES), lambda b, p, i: (b * nq + i, p)),
        out_shape=jax.ShapeDtypeStruct((T, HEADS_MLA * HEAD_DIM), BF16),
        scratch_shapes=[pltpu.VMEM((2, tq, LANES), F32)] * 3,
        compiler_params=_cparams(("parallel", "parallel", "arbitrary")),
        name="mla_attn",
    )(mq, mk, mv)


def _dil_bias(tq):
    nband = DIL_BAND // tq + 1
    d = (lax.broadcasted_iota(jnp.int32, (nband, tq, tq), 0) * tq
         + lax.broadcasted_iota(jnp.int32, (nband, tq, tq), 1)
         - lax.broadcasted_iota(jnp.int32, (nband, tq, tq), 2))
    cnt = jnp.zeros(d.shape, F32)
    for window, dil in DIL_PAIRS:
        ok = (d >= 0) & (d % dil == 0) & (d <= (window // dil) * dil)
        cnt = cnt + ok.astype(F32)
    return jnp.where(cnt > 0, jnp.log2(jnp.maximum(cnt, 1.0)), NEG)


def _dil_kernel(q_ref, k_ref, v_ref, bias_ref, o_ref, m_ref, l_ref, acc_ref):
    tq = q_ref.shape[0]
    nband = bias_ref.shape[0]
    i = pl.program_id(2)
    _flash_init(m_ref, l_ref, acc_ref)
    q_stack = _head_stack(q_ref[...], 2)

    def step(delta):
        start = pl.multiple_of((i - delta) * tq, tq)
        bias = bias_ref[delta]
        _flash_tile(q_stack, k_ref[pl.ds(start, tq), :], v_ref[pl.ds(start, tq), :], 2,
                    lambda s: s + bias, m_ref, l_ref, acc_ref)

    @pl.when(i >= nband - 1)
    def _():
        for delta in range(nband):
            step(delta)

    @pl.when(i < nband - 1)
    def _():
        step(0)

        def body(delta, carry):
            step(delta)
            return carry

        lax.fori_loop(1, i + 1, body, 0)

    o_ref[...] = _flash_out(2, l_ref, acc_ref).astype(o_ref.dtype)


def _dil_attention(dq, dk, dv, bias, batch, seq):
    T = dq.shape[0]
    tq = TQ
    nq = seq // tq
    npair = HEADS_DIL // 2
    return pl.pallas_call(
        _dil_kernel,
        grid=(batch, npair, nq),
        in_specs=[
            pl.BlockSpec((tq, LANES), lambda b, p, i: (b * nq + i, p)),
            _resident((seq, LANES), lambda b, p, i: (b, p)),
            _resident((seq, LANES), lambda b, p, i: (b, p)),
            _resident(bias.shape, lambda b, p, i: (0, 0, 0)),
        ],
        out_specs=pl.BlockSpec((tq, LANES), lambda b, p, i: (b * nq + i, p)),
        out_shape=jax.ShapeDtypeStruct((T, HEADS_DIL * HEAD_DIM), BF16),
        scratch_shapes=[pltpu.VMEM((2, tq, LANES), F32)] * 3,
        compiler_params=_cparams(("parallel", "parallel", "arbitrary")),
        name="dil_attn",
    )(dq, dk, dv, bias)


_INT_MIN = -(2 ** 31)
_KEY_NEG_INF = _INT_MIN + 0x7FFFFF
INTERP_MIN_WIDTH = 4096
INTERP_MIN_INDEX = 32


def _ordered_bits(x):
    return x ^ ((x >> 31) & 0x7FFFFFFF)


def _key_to_float(key):
    return lax.bitcast_convert_type(_ordered_bits(key), F32)


def _float_to_key(x):
    return _ordered_bits(lax.bitcast_convert_type(x, jnp.int32))


def _dsa_kernel(iq_ref, iwt_ref, ik_ref, q_ref, k_ref, v_ref, o_ref,
                sc_ref, cm_ref, m_ref, l_ref, acc_ref, *, ksel):
    tq = q_ref.shape[0]
    i = pl.program_id(1)
    ntile = i + 1
    key_row = lax.broadcasted_iota(jnp.int32, (tq, tq), 0)
    causal = key_row <= lax.broadcasted_iota(jnp.int32, (tq, tq), 1)
    neg_inf = jnp.float32(-jnp.inf)
    as_f32 = lambda v: lax.bitcast_convert_type(v, F32)
    as_i32 = lambda v: lax.bitcast_convert_type(v, jnp.int32)

    iq_stack = jnp.concatenate(
        [jnp.where((lax.broadcasted_iota(jnp.int32, iq_ref.shape, 1) // IDX_DIM) == h,
                   iq_ref[...], jnp.zeros(iq_ref.shape, BF16)) for h in range(IDX_HEADS)], axis=0)
    wt = iwt_ref[...]
    cm_ref[...] = jnp.full(cm_ref.shape, neg_inf, F32)

    def score_tile(j, diag):
        kk = ik_ref[pl.ds(pl.multiple_of(j * tq, tq), tq), :]
        sc = jnp.zeros((tq, tq), F32)
        for half in range(2):
            nh = IDX_HEADS // 2
            d = _dot_nt(kk, iq_stack[half * nh * tq:(half + 1) * nh * tq])
            for hh in range(nh):
                h = half * nh + hh
                sc = sc + jnp.maximum(d[:, hh * tq:(hh + 1) * tq], 0.0) * wt[h:h + 1, :]
        if diag:
            sc = jnp.where(causal, sc, neg_inf)
        sc_ref[j] = as_i32(sc)
        cm_ref[...] = jnp.maximum(cm_ref[...], sc)

    def score_group(g, carry):
        for u in range(SCORE_UNROLL):
            score_tile(g * SCORE_UNROLL + u, False)
        return carry

    def score_body(j, carry):
        score_tile(j, False)
        return carry

    ngroup = i // SCORE_UNROLL
    lax.fori_loop(0, ngroup, score_group, 0)
    lax.fori_loop(ngroup * SCORE_UNROLL, i, score_body, 0)
    score_tile(i, True)

    def count_cols(hit, thr):
        def body(j, cnt):
            words = sc_ref[j]
            for g in range(tq // 8):
                cnt = cnt + jnp.where(hit(words[8 * g:8 * (g + 1)], thr), 1, 0)
            return cnt

        cnt = lax.fori_loop(0, ntile, body, jnp.zeros((8, tq), jnp.int32))
        return _slab(jnp.sum(cnt, axis=0, keepdims=True))

    def count_ge(thr):
        return count_cols(lambda words, t: as_f32(words) >= t, thr)

    cm = cm_ref[...]
    lo = jnp.maximum(_float_to_key(_slab(jnp.min(cm, axis=0, keepdims=True))), _KEY_NEG_INF)
    hi = _float_to_key(_slab(jnp.max(cm, axis=0, keepdims=True))) + 1
    clo = jnp.full((8, tq), 2 ** 30, jnp.int32)
    chi = jnp.zeros((8, tq), jnp.int32)
    key_zero = 0
    key_tiny = 0x00800000
    hi = jnp.where((hi > key_zero) & (hi < key_tiny), key_tiny, hi)

    def open_rows(lo, hi, clo):
        zero_class = (lo == key_zero) & (hi == key_tiny)
        return (clo != ksel) & (hi - 1 > lo) & jnp.logical_not(zero_class)

    def any_rows(mask):
        return jnp.max(jnp.where(mask, 1, 0))

    def probe_key(lo, hi, clo, chi):
        mid = (lo >> 1) + (hi >> 1) + (lo & hi & 1)
        straddle = (lo < key_zero) & (hi > key_zero)
        width = hi - lo
        lo_f = _key_to_float(lo)
        frac = (clo - ksel).astype(F32) / (clo - chi).astype(F32)
        guess = _float_to_key(lo_f + (_key_to_float(hi) - lo_f) * frac)
        margin = width >> 3
        guess = jnp.minimum(jnp.maximum(guess, lo + 1 + margin), hi - 1 - margin)
        use = ((clo < 2 ** 30) & (width > INTERP_MIN_WIDTH) & (lo > _KEY_NEG_INF)
               & jnp.logical_not(straddle))
        probe = jnp.where(use, guess, mid)
        probe = jnp.where(straddle, key_zero, probe)
        return jnp.where((lo == key_zero) & (hi > key_tiny), key_tiny, probe)

    def sel_body(st):
        _, lo, hi, clo, chi = st
        mid = probe_key(lo, hi, clo, chi)
        cnt = count_ge(_key_to_float(mid))
        ge = cnt >= ksel
        act = open_rows(lo, hi, clo)
        up = act & ge
        dn = act & jnp.logical_not(ge)
        lo = jnp.where(up, mid, lo)
        clo = jnp.where(up, cnt, clo)
        hi = jnp.where(dn, mid, hi)
        chi = jnp.where(dn, cnt, chi)
        return any_rows(open_rows(lo, hi, clo)), lo, hi, clo, chi

    _, lo, hi, clo, chi = lax.while_loop(
        lambda st: st[0] > 0, sel_body, (any_rows(open_rows(lo, hi, clo)), lo, hi, clo, chi))
    exact = clo == ksel
    upper = _key_to_float(jnp.where(exact, lo, hi))[:1]
    lower = _key_to_float(lo)[:1]

    big = jnp.int32(2 ** 30)

    def rank_body(j, carry):
        sc = as_f32(sc_ref[j])
        sc_ref[j] = jnp.where(sc >= upper, -1, jnp.where(sc >= lower, key_row + j * tq, big))
        return carry

    lax.fori_loop(0, ntile, rank_body, 0)

    ja = jnp.where(exact, -1, 0)
    jb = jnp.where(exact, 0, ntile * tq)
    ca = jnp.where(exact, ksel, chi)
    cb = clo

    def idx_body(st):
        _, ja, jb, ca, cb = st
        width = jb - ja
        frac = (ksel - ca).astype(F32) / (cb - ca).astype(F32)
        guess = ja + (width.astype(F32) * frac).astype(jnp.int32)
        margin = width >> 3
        guess = jnp.minimum(jnp.maximum(guess, ja + 1 + margin), jb - 1 - margin)
        mid = jnp.where((cb < 2 ** 30) & (width > INTERP_MIN_INDEX), guess, (ja + jb) >> 1)
        cnt = count_cols(lambda words, t: words < t, mid)
        act = width > 1
        hit = act & (cnt == ksel)
        dn = act & (cnt >= ksel)
        up = act & (cnt < ksel)
        jb = jnp.where(dn, mid, jb)
        cb = jnp.where(dn, cnt, cb)
        ja = jnp.where(hit, mid - 1, jnp.where(up, mid, ja))
        ca = jnp.where(up, cnt, ca)
        return any_rows(jb - ja > 1), ja, jb, ca, cb

    _, _, jb, _, _ = lax.while_loop(lambda st: st[0] > 0, idx_body,
                                   (any_rows(jb - ja > 1), ja, jb, ca, cb))
    bound = jb[:1]

    _flash_init(m_ref, l_ref, acc_ref)
    q_stack = _head_stack(q_ref[...], HEADS_DSA)

    def attn_tile(j, diag):
        start = pl.multiple_of(j * tq, tq)
        sel = sc_ref[j] < bound
        if diag:
            sel = sel & causal
        cap = jnp.where(sel, -NEG, NEG)
        _flash_tile_t(q_stack, k_ref[pl.ds(start, tq), :], v_ref[pl.ds(start, tq), :], HEADS_DSA,
                      lambda s: jnp.minimum(s, cap), m_ref, l_ref, acc_ref)

    def attn_body(j, carry):
        attn_tile(j, False)
        return carry

    lax.fori_loop(0, i, attn_body, 0)
    attn_tile(i, True)
    o_ref[...] = _flash_out_t(HEADS_DSA, l_ref, acc_ref).astype(o_ref.dtype)


def _dsa_attention(iq, iw, ik, sq, sk, sv, batch, seq):
    T = sq.shape[0]
    tq = TQ
    nq = seq // tq
    ksel = min(DSA_TOPK, seq // 4)
    assert tq >= ksel
    width = HEADS_DSA * HEAD_DIM
    qrow = lambda b, i: (b * nq + i, 0)
    full = lambda b, i: (b, 0)
    return pl.pallas_call(
        functools.partial(_dsa_kernel, ksel=ksel),
        grid=(batch, nq),
        in_specs=[
            pl.BlockSpec((tq, IDX_HEADS * IDX_DIM), qrow),
            pl.BlockSpec((IDX_HEADS, tq), lambda b, i: (0, b * nq + i)),
            _resident((seq, IDX_HEADS * IDX_DIM), full),
            pl.BlockSpec((tq, width), qrow),
            _resident((seq, width), full),
            _resident((seq, width), full),
        ],
        out_specs=pl.BlockSpec((tq, width), qrow),
        out_shape=jax.ShapeDtypeStruct((T, width), BF16),
        scratch_shapes=[
            pltpu.VMEM((nq, tq, tq), jnp.int32),
            pltpu.VMEM((tq, tq), F32),
            pltpu.VMEM((HEADS_DSA, 8, tq), F32),
            pltpu.VMEM((HEADS_DSA, 8, tq), F32),
            pltpu.VMEM((HEADS_DSA, tq, width), F32),
        ],
        compiler_params=_cparams(("parallel", "arbitrary")),
        name="dsa_attn",
    )(iq, jnp.transpose(iw[:, :IDX_HEADS]), ik, sq, sk, sv)


def _oproj_kernel(x_ref, oa_ref, ob_ref, oc_ref, wo_ref, g_ref, x1_ref, h_ref):
    na = HEADS_MLA * HEAD_DIM
    nb = HEADS_DIL * HEAD_DIM
    x1 = (x_ref[...] + _dot(oa_ref[...], wo_ref[:na, :])
          + _dot(ob_ref[...], wo_ref[na:na + nb, :])
          + _dot(oc_ref[...], wo_ref[na + nb:, :]))
    x1_ref[...] = x1
    h_ref[...] = _rms(x1, g_ref[...]).astype(BF16)


def _out_projection(x2, oa, ob, oc, wo, g):
    T = x2.shape[0]
    tm = TM_FFN
    row = lambda i: (i, 0)
    const = lambda i: (0, 0)
    return pl.pallas_call(
        _oproj_kernel,
        grid=(T // tm,),
        in_specs=[
            pl.BlockSpec((tm, D_MODEL), row),
            pl.BlockSpec((tm, oa.shape[1]), row),
            pl.BlockSpec((tm, ob.shape[1]), row),
            pl.BlockSpec((tm, oc.shape[1]), row),
            pl.BlockSpec(wo.shape, const),
            pl.BlockSpec((1, D_MODEL), const),
        ],
        out_specs=[pl.BlockSpec((tm, D_MODEL), row), pl.BlockSpec((tm, D_MODEL), row)],
        out_shape=[jax.ShapeDtypeStruct((T, D_MODEL), F32), jax.ShapeDtypeStruct((T, D_MODEL), BF16)],
        compiler_params=_cparams(("parallel",)),
        name="oproj",
    )(x2, oa, ob, oc, wo, g)


def _ffn_kernel(h_ref, halo_ref, x1_ref, wg_ref, wu_ref, cg_ref, cu_ref, wd_ref, *rest,
                blocks_per_seq, final_norm):
    gf_ref = rest[0] if final_norm else None
    o_ref, uga_ref, uua_ref, ugb_ref, uub_ref, acc_ref = rest[1:] if final_norm else rest
    tm = h_ref.shape[0]
    hal = halo_ref.shape[0]
    nchunk = wg_ref.shape[0]
    first = (pl.program_id(0) % blocks_per_seq) == 0
    keep = jnp.where(first, 0.0, 1.0).astype(F32)
    h = h_ref[...]
    hh = halo_ref[...]
    acc_ref[...] = jnp.zeros(acc_ref.shape, F32)

    def conv(u_ref, w):
        y = w[CONV_WIDTH:CONV_WIDTH + 1, :]
        for tap in range(CONV_WIDTH):
            off = hal - (CONV_WIDTH - 1) + tap
            y = y + u_ref[pl.ds(off, tm), :] * w[tap:tap + 1, :]
        return y

    def up_proj(j, slot):
        ug_ref, uu_ref = slot
        wg = wg_ref[j]
        wu = wu_ref[j]
        ug_ref[:hal, :] = _dot(hh, wg) * keep
        ug_ref[hal:, :] = _dot(h, wg)
        uu_ref[:hal, :] = _dot(hh, wu) * keep
        uu_ref[hal:, :] = _dot(h, wu)

    def down_proj(j, slot):
        ug_ref, uu_ref = slot
        gate = conv(ug_ref, cg_ref[j])
        up = conv(uu_ref, cu_ref[j])
        act = gate * jax.nn.sigmoid(gate) * up
        acc_ref[...] += _dot(act.astype(BF16), wd_ref[j])

    slot_a = (uga_ref, uua_ref)
    slot_b = (ugb_ref, uub_ref)
    up_proj(0, slot_a)

    def chunk_pair(jj, carry):
        j = 2 * jj
        up_proj(j + 1, slot_b)
        down_proj(j, slot_a)
        up_proj(j + 2, slot_a)
        down_proj(j + 1, slot_b)
        return carry

    lax.fori_loop(0, (nchunk - 1) // 2, chunk_pair, 0)
    if nchunk % 2 == 0:
        up_proj(nchunk - 1, slot_b)
        down_proj(nchunk - 2, slot_a)
        down_proj(nchunk - 1, slot_b)
    else:
        down_proj(nchunk - 1, slot_a)
    out = x1_ref[...] + acc_ref[...]
    o_ref[...] = out if gf_ref is None else _rms(out, gf_ref[...])


def _ffn(h2, x1, wg, wu, cg, cu, wd, seq, g_final=None):
    T = x1.shape[0]
    tm = TM_FFN
    hal = BF16_SUBLANES
    tn = wg.shape[2]
    row = lambda i: (i, 0)
    c3 = lambda i: (0, 0, 0)
    final_norm = g_final is not None
    in_specs = [
        pl.BlockSpec((tm, D_MODEL), row),
        pl.BlockSpec((hal, D_MODEL), lambda i: (jnp.maximum(i * (tm // hal) - 1, 0), 0)),
        pl.BlockSpec((tm, D_MODEL), row),
        _resident(wg.shape, c3),
        _resident(wu.shape, c3),
        _resident(cg.shape, c3),
        _resident(cu.shape, c3),
        _resident(wd.shape, c3),
    ]
    args = [h2, h2, x1, wg, wu, cg, cu, wd]
    if final_norm:
        in_specs.append(pl.BlockSpec((1, D_MODEL), lambda i: (0, 0)))
        args.append(g_final)
    return pl.pallas_call(
        functools.partial(_ffn_kernel, blocks_per_seq=seq // tm, final_norm=final_norm),
        grid=(T // tm,),
        in_specs=in_specs,
        out_specs=pl.BlockSpec((tm, D_MODEL), row),
        out_shape=jax.ShapeDtypeStruct((T, D_MODEL), F32),
        scratch_shapes=[pltpu.VMEM((hal + tm, tn), F32)] * 4 + [pltpu.VMEM((tm, D_MODEL), F32)],
        compiler_params=_cparams(("parallel",)),
        name="ffn",
    )(*args)


def _rope_tables(seq):
    def base(dim):
        inv = jnp.power(jnp.float32(ROPE_THETA), -jnp.arange(0, dim, 2, dtype=F32) / dim)
        ang = jnp.arange(seq, dtype=F32)[:, None] * inv[None, :]
        return jnp.cos(ang), jnp.sin(ang)

    def tables(dim, period):
        cos, sin = base(dim)
        half = dim // 2
        pad = period - dim
        c = jnp.concatenate([cos, cos, jnp.ones((seq, pad), F32)], axis=1)
        sn = jnp.concatenate([-sin, jnp.zeros((seq, half + pad), F32)], axis=1)
        sp = jnp.concatenate([jnp.zeros((seq, half), F32), sin, jnp.zeros((seq, pad), F32)], axis=1)
        reps = LANES // period
        return [jnp.tile(t, (1, reps)) for t in (c, sn, sp)]

    tabs = (tables(MLA_ROPE, LANES) + tables(HEAD_DIM // ROT_FRAC, HEAD_DIM)
            + tables(IDX_DIM // ROT_FRAC, IDX_DIM))
    return jnp.stack(tabs)


def _pad_cols(w, width):
    return jnp.pad(w, ((0, 0), (0, width - w.shape[1])))


def _layer_weights(w_in, w_uq, w_ukv, w_up, conv_w, conv_b, w_down):
    o = 0
    parts = {}
    for name, width in (("cq", Q_LORA), ("ckv", KV_LORA), ("kpe", MLA_ROPE),
                        ("dil", 3 * HEADS_DIL * HEAD_DIM), ("dsa", 3 * HEADS_DSA * HEAD_DIM),
                        ("qi", IDX_HEADS * IDX_DIM), ("ki", IDX_DIM), ("wi", IDX_HEADS)):
        parts[name] = w_in[:, o:o + width]
        o += width
    w1 = jnp.concatenate([
        parts["cq"], parts["ckv"], _pad_cols(parts["kpe"], LANES), parts["dil"], parts["dsa"],
        parts["qi"], jnp.tile(parts["ki"], (1, IDX_HEADS)), _pad_cols(parts["wi"], LANES)],
        axis=1).astype(BF16)
    assert w1.shape[1] == N_PROJ

    dq = MLA_NOPE + MLA_ROPE
    cols = []
    for hd in range(HEADS_MLA):
        nope = w_uq[:, dq * hd:dq * hd + MLA_NOPE]
        pe = w_uq[:, dq * hd + MLA_NOPE:dq * (hd + 1)]
        z = jnp.zeros((Q_LORA, MLA_NOPE), w_uq.dtype)
        slot = [nope, z] if hd % 2 == 0 else [z, nope]
        cols += slot + [_pad_cols(pe, LANES)]
    wuq = jnp.concatenate(cols, axis=1).astype(BF16)

    dkv = MLA_NOPE + HEAD_DIM
    kn = [w_ukv[:, dkv * hd:dkv * hd + MLA_NOPE] for hd in range(HEADS_MLA)]
    vv = [w_ukv[:, dkv * hd + MLA_NOPE:dkv * (hd + 1)] for hd in range(HEADS_MLA)]
    wukv = jnp.concatenate(kn + vv, axis=1).astype(BF16)

    nchunk = D_FF // TN_FFN
    def chunks(w):
        return w.reshape(w.shape[0], nchunk, TN_FFN).transpose(1, 0, 2)
    wg = chunks(w_up[:, :D_FF]).astype(BF16)
    wu = chunks(w_up[:, D_FF:]).astype(BF16)
    cw = jnp.concatenate([conv_w, conv_b[None, :],
                          jnp.zeros((8 - CONV_WIDTH - 1, 2 * D_FF), F32)], axis=0)
    cg = chunks(cw[:, :D_FF])
    cu = chunks(cw[:, D_FF:])
    wd = w_down.reshape(nchunk, TN_FFN, D_MODEL).astype(BF16)
    return w1, wuq, wukv, wg, wu, cg, cu, wd


def kernel(x, g_attn, w_in, g_q_lat, w_uq, g_kv_lat, w_ukv, w_o, g_ffn, w_up, conv_w, conv_b,
           w_down, g_final):
    batch, seq, _ = x.shape
    depth = w_in.shape[0]
    assert seq % DIL_BAND == 0 and seq % TM_FFN == 0 and seq % TQ == 0
    assert D_FF % TN_FFN == 0
    T = batch * seq
    tabs = _rope_tables(seq)
    bias = _dil_bias(TQ)
    xf = x.reshape(T, D_MODEL)
    for l in range(depth):
        w1, wuq, wukv, wg, wu, cg, cu, wd = _layer_weights(
            w_in[l], w_uq[l], w_ukv[l], w_up[l], conv_w[l], conv_b[l], w_down[l])
        (mq, mk, mv, dq, dk, dv, sq, sk, sv, iq, ik, iw) = _projection(
            xf, g_attn[l][None, :], w1, g_q_lat[l][None, :], wuq, g_kv_lat[l][None, :], wukv,
            tabs, seq)
        oa = _mla_attention(mq, mk, mv, batch, seq)
        ob = _dil_attention(dq, dk, dv, bias, batch, seq)
        oc = _dsa_attention(iq, iw, ik, sq, sk, sv, batch, seq)
        x1, h2 = _out_projection(xf, oa, ob, oc, w_o[l].astype(BF16), g_ffn[l][None, :])
        xf = _ffn(h2, x1, wg, wu, cg, cu, wd, seq,
                  g_final=g_final[None, :] if l == depth - 1 else None)
    return xf.reshape(batch, seq, D_MODEL)
```

```python
import functools
import math

import jax
import jax.numpy as jnp
from jax import lax
from jax.experimental import pallas as pl
from jax.experimental.pallas import tpu as pltpu

D_MODEL = 1024
HEAD_DIM = 64
HEADS_MLA = 4
HEADS_DIL = 8
HEADS_DSA = 4
Q_LORA = 256
KV_LORA = 128
MLA_NOPE = 64
MLA_ROPE = 32
DIL_PAIRS = ((128, 1), (512, 4), (2048, 16))
IDX_HEADS = 8
IDX_DIM = 32
DSA_TOPK = 256
D_FF = 2816
CONV_WIDTH = 3
ROPE_THETA = 500000.0
ROT_FRAC = 4
NORM_EPS = 1e-6

LANES = 128
BF16_SUBLANES = 16
VMEM_LIMIT = 56 * 1024 * 1024
NEG = -1e30
LOG2E = math.log2(math.e)
F32 = jnp.float32
BF16 = jnp.bfloat16

TQ = 512
TM_ROW = 512
TM_OPROJ = 1024
TM_FFN = 512
TN_FFN = 1408
MLA_UNROLL = 4
SCORE_UNROLL = 4
DIL_BAND = max(w for w, _ in DIL_PAIRS)

C_CQ = 0
C_CKV = C_CQ + Q_LORA
C_KPE = C_CKV + KV_LORA
C_DIL = C_KPE + LANES
C_DSA = C_DIL + 3 * HEADS_DIL * HEAD_DIM
C_QI = C_DSA + 3 * HEADS_DSA * HEAD_DIM
C_KI = C_QI + IDX_HEADS * IDX_DIM
C_WI = C_KI + IDX_HEADS * IDX_DIM
N_PROJ = C_WI + LANES


def _cparams(sem):
    return pltpu.CompilerParams(dimension_semantics=sem, vmem_limit_bytes=VMEM_LIMIT)


def _resident(shape, index_map):
    return pl.BlockSpec(shape, index_map, pipeline_mode=pl.Buffered(1))


def _rms(x, g):
    return x * lax.rsqrt(jnp.mean(x * x, axis=-1, keepdims=True) + NORM_EPS) * g


def _dot(a, b):
    return jnp.dot(a, b, preferred_element_type=F32)


def _dot_nt(a, b):
    return lax.dot_general(a, b, (((1,), (1,)), ((), ())), preferred_element_type=F32)


def _rope(x, tab_ref, t0, half):
    c = tab_ref[t0]
    sn = tab_ref[t0 + 1]
    sp = tab_ref[t0 + 2]
    return (x * c + pltpu.roll(x, LANES - half, axis=1) * sn
            + pltpu.roll(x, half, axis=1) * sp)


def _proj_kernel(x_ref, g_ref, w1_ref, gq_ref, wuq_ref, gkv_ref, wukv_ref, tab_ref,
                 mq_ref, mk_ref, mv_ref, dq_ref, dk_ref, dv_ref,
                 sq_ref, sk_ref, sv_ref, iq_ref, ik_ref, iw_ref):
    h = _rms(x_ref[...], g_ref[...]).astype(BF16)

    def proj(lo, width):
        return _dot(h, w1_ref[:, lo:lo + width])

    mla_scale = (MLA_NOPE + MLA_ROPE) ** -0.5 * LOG2E
    cq = _rms(proj(C_CQ, Q_LORA), gq_ref[...]).astype(BF16)
    for hd in range(HEADS_MLA):
        base = 2 * LANES * hd
        q2 = _dot(cq, wuq_ref[:, base:base + 2 * LANES])
        qp = _rope(q2[:, LANES:], tab_ref, 0, MLA_ROPE // 2)
        mq_ref[:, base:base + LANES] = (q2[:, :LANES] * mla_scale).astype(BF16)
        mq_ref[:, base + LANES:base + 2 * LANES] = (qp * mla_scale).astype(BF16)
    zkv = proj(C_CKV, KV_LORA + LANES)
    ckv = _rms(zkv[:, :KV_LORA], gkv_ref[...]).astype(BF16)
    kpe = _rope(zkv[:, KV_LORA:], tab_ref, 0, MLA_ROPE // 2).astype(BF16)
    kv = _dot(ckv, wukv_ref[...])
    for pr in range(HEADS_MLA // 2):
        mk_ref[:, 2 * LANES * pr:2 * LANES * pr + LANES] = kv[:, LANES * pr:LANES * (pr + 1)].astype(BF16)
        mk_ref[:, 2 * LANES * pr + LANES:2 * LANES * (pr + 1)] = kpe
    mv_ref[...] = kv[:, HEADS_MLA * MLA_NOPE:].astype(BF16)

    head_scale = HEAD_DIM ** -0.5 * LOG2E
    half_h = HEAD_DIM // ROT_FRAC // 2

    def roped(c0, width, t0, half, scale, o_ref):
        step = min(width, 2 * LANES)
        for c in range(0, width, step):
            z = proj(c0 + c, step)
            for off in range(0, step, LANES):
                r = _rope(z[:, off:off + LANES], tab_ref, t0, half)
                if scale is not None:
                    r = r * scale
                o_ref[:, c + off:c + off + LANES] = r.astype(BF16)

    def qkv(c0, nheads, q_ref, k_ref, v_ref):
        width = nheads * HEAD_DIM
        roped(c0, width, 3, half_h, head_scale, q_ref)
        roped(c0 + width, width, 3, half_h, None, k_ref)
        v_ref[...] = proj(c0 + 2 * width, width).astype(BF16)

    qkv(C_DIL, HEADS_DIL, dq_ref, dk_ref, dv_ref)
    qkv(C_DSA, HEADS_DSA, sq_ref, sk_ref, sv_ref)

    half_i = IDX_DIM // ROT_FRAC // 2
    roped(C_QI, IDX_HEADS * IDX_DIM, 6, half_i, None, iq_ref)
    roped(C_KI, IDX_HEADS * IDX_DIM, 6, half_i, None, ik_ref)
    iw_ref[...] = proj(C_WI, LANES) * ((IDX_HEADS * IDX_DIM) ** -0.5)


def _projection(x2, g, w1, gq, wuq, gkv, wukv, tabs, seq):
    T = x2.shape[0]
    tm = TM_ROW
    nsb = seq // tm
    row = lambda i: (i, 0)
    const = lambda i: (0, 0)
    widths = [2 * LANES * HEADS_MLA, 2 * LANES * (HEADS_MLA // 2), HEADS_MLA * HEAD_DIM,
              HEADS_DIL * HEAD_DIM, HEADS_DIL * HEAD_DIM, HEADS_DIL * HEAD_DIM,
              HEADS_DSA * HEAD_DIM, HEADS_DSA * HEAD_DIM, HEADS_DSA * HEAD_DIM,
              IDX_HEADS * IDX_DIM, IDX_HEADS * IDX_DIM]
    out_shape = [jax.ShapeDtypeStruct((T, w), BF16) for w in widths]
    out_shape.append(jax.ShapeDtypeStruct((T, LANES), F32))
    out_specs = [pl.BlockSpec((tm, w), row) for w in widths] + [pl.BlockSpec((tm, LANES), row)]
    return pl.pallas_call(
        _proj_kernel,
        grid=(T // tm,),
        in_specs=[
            pl.BlockSpec((tm, D_MODEL), row),
            pl.BlockSpec((1, D_MODEL), const),
            pl.BlockSpec(w1.shape, const),
            pl.BlockSpec((1, Q_LORA), const),
            pl.BlockSpec(wuq.shape, const),
            pl.BlockSpec((1, KV_LORA), const),
            pl.BlockSpec(wukv.shape, const),
            pl.BlockSpec((9, tm, LANES), lambda i: (0, i % nsb, 0)),
        ],
        out_specs=out_specs,
        out_shape=out_shape,
        compiler_params=_cparams(("parallel",)),
        name="proj",
    )(x2, g, w1, gq, wuq, gkv, wukv, tabs)


def _rep(x, n):
    return x if n == 1 else jnp.concatenate([x] * n, axis=1)


def _flash_tile(q_stack, k, v, nheads, mask_fn, m_ref, l_ref, acc_ref):
    tq = q_stack.shape[0] // nheads
    tk = k.shape[0]
    dv = v.shape[1]
    s_all = _dot_nt(q_stack, k)
    ps = []
    alphas = []
    for h in range(nheads):
        s = mask_fn(s_all[h * tq:(h + 1) * tq])
        m_prev = m_ref[h]
        m_new = jnp.maximum(m_prev, jnp.max(s, axis=1, keepdims=True))
        alpha = jnp.exp2(m_prev - m_new)
        p = jnp.exp2(s - _rep(m_new, tk // LANES))
        l_ref[h] = alpha * l_ref[h] + jnp.sum(p, axis=1, keepdims=True)
        m_ref[h] = m_new
        ps.append(p.astype(BF16))
        alphas.append(alpha)
    pv = _dot(jnp.concatenate(ps, axis=0), v)
    for h in range(nheads):
        acc_ref[h] = _rep(alphas[h], dv // LANES) * acc_ref[h] + pv[h * tq:(h + 1) * tq]


def _flash_init(m_ref, l_ref, acc_ref):
    m_ref[...] = jnp.full(m_ref.shape, NEG, F32)
    l_ref[...] = jnp.zeros(l_ref.shape, F32)
    acc_ref[...] = jnp.zeros(acc_ref.shape, F32)


def _flash_out(nheads, l_ref, acc_ref):
    tq, dv = acc_ref.shape[1:]
    lane = lax.broadcasted_iota(jnp.int32, (tq, dv), 1)
    out = jnp.zeros((tq, dv), F32)
    for h in range(nheads):
        out = jnp.where((lane // HEAD_DIM) == h, acc_ref[h] / _rep(l_ref[h], dv // LANES), out)
    return out


def _slab(x):
    return jnp.broadcast_to(x, (8, x.shape[1]))


def _to_rows(x, width):
    tq = x.shape[1]
    return _rep(jnp.transpose(jnp.broadcast_to(x[:1], (LANES, tq))), width // LANES)


def _flash_tile_t(q_stack, k, v, nheads, mask_fn, m_ref, l_ref, acc_ref):
    tq = q_stack.shape[0] // nheads
    dv = v.shape[1]
    s_all = _dot_nt(k, q_stack)
    ps = []
    alphas = []
    for h in range(nheads):
        s = mask_fn(s_all[:, h * tq:(h + 1) * tq])
        m_prev = m_ref[h]
        m_new = jnp.maximum(m_prev, _slab(jnp.max(s, axis=0, keepdims=True)))
        alpha = jnp.exp2(m_prev - m_new)
        p = jnp.exp2(s - m_new[:1])
        l_ref[h] = alpha * l_ref[h] + _slab(jnp.sum(p, axis=0, keepdims=True))
        m_ref[h] = m_new
        ps.append(p.astype(BF16))
        alphas.append(alpha)
    pv = lax.dot_general(jnp.concatenate(ps, axis=1), v, (((0,), (0,)), ((), ())),
                         preferred_element_type=F32)
    for h in range(nheads):
        acc_ref[h] = _to_rows(alphas[h], dv) * acc_ref[h] + pv[h * tq:(h + 1) * tq]


def _flash_out_t(nheads, l_ref, acc_ref):
    tq, dv = acc_ref.shape[1:]
    lane = lax.broadcasted_iota(jnp.int32, (tq, dv), 1)
    out = jnp.zeros((tq, dv), F32)
    for h in range(nheads):
        out = jnp.where((lane // HEAD_DIM) == h, acc_ref[h] / _to_rows(l_ref[h], dv), out)
    return out


def _head_stack(q, nheads):
    lane = lax.broadcasted_iota(jnp.int32, q.shape, 1)
    zero = jnp.zeros_like(q)
    return jnp.concatenate([jnp.where((lane // HEAD_DIM) == h, q, zero) for h in range(nheads)], axis=0)


def _causal_tile(tq):
    r = lax.broadcasted_iota(jnp.int32, (tq, tq), 0)
    c = lax.broadcasted_iota(jnp.int32, (tq, tq), 1)
    return r >= c


def _mla_kernel(q_ref, k_ref, v_ref, o_ref, m_ref, l_ref, acc_ref):
    tq = q_ref.shape[0]
    i = pl.program_id(2)
    _flash_init(m_ref, l_ref, acc_ref)
    q_stack = jnp.concatenate([q_ref[:, :2 * LANES], q_ref[:, 2 * LANES:]], axis=0)

    def step(j, mask_fn):
        start = pl.multiple_of(j * tq, tq)
        _flash_tile(q_stack, k_ref[pl.ds(start, tq), :], v_ref[pl.ds(start, tq), :], 2,
                    mask_fn, m_ref, l_ref, acc_ref)

    def group(g, carry):
        for u in range(MLA_UNROLL):
            step(g * MLA_UNROLL + u, lambda s: s)
        return carry

    def single(j, carry):
        step(j, lambda s: s)
        return carry

    ngroup = i // MLA_UNROLL
    lax.fori_loop(0, ngroup, group, 0)
    lax.fori_loop(ngroup * MLA_UNROLL, i, single, 0)
    step(i, lambda s: jnp.where(_causal_tile(tq), s, NEG))
    o_ref[...] = _flash_out(2, l_ref, acc_ref).astype(o_ref.dtype)


def _mla_attention(mq, mk, mv, batch, seq):
    T = mq.shape[0]
    tq = TQ
    nq = seq // tq
    npair = HEADS_MLA // 2
    return pl.pallas_call(
        _mla_kernel,
        grid=(batch, npair, nq),
        in_specs=[
            pl.BlockSpec((tq, 4 * LANES), lambda b, p, i: (b * nq + i, p)),
            _resident((seq, 2 * LANES), lambda b, p, i: (b, p)),
            _resident((seq, LANES), lambda b, p, i: (b, p)),
        ],
        out_specs=pl.BlockSpec((tq, LANES), lambda b, p, i: (b * nq + i, p)),
        out_shape=jax.ShapeDtypeStruct((T, HEADS_MLA * HEAD_DIM), BF16),
        scratch_shapes=[pltpu.VMEM((2, tq, LANES), F32)] * 3,
        compiler_params=_cparams(("parallel", "parallel", "arbitrary")),
        name="mla_attn",
    )(mq, mk, mv)


def _dil_bias(tq):
    nband = DIL_BAND // tq + 1
    d = (lax.broadcasted_iota(jnp.int32, (nband, tq, tq), 0) * tq
         + lax.broadcasted_iota(jnp.int32, (nband, tq, tq), 1)
         - lax.broadcasted_iota(jnp.int32, (nband, tq, tq), 2))
    cnt = jnp.zeros(d.shape, F32)
    for window, dil in DIL_PAIRS:
        ok = (d >= 0) & (d % dil == 0) & (d <= (window // dil) * dil)
        cnt = cnt + ok.astype(F32)
    return jnp.where(cnt > 0, jnp.log2(jnp.maximum(cnt, 1.0)), NEG)


def _dil_kernel(q_ref, k_ref, v_ref, bias_ref, o_ref, m_ref, l_ref, acc_ref):
    tq = q_ref.shape[0]
    nband = bias_ref.shape[0]
    i = pl.program_id(2)
    _flash_init(m_ref, l_ref, acc_ref)
    q_stack = _head_stack(q_ref[...], 2)

    def step(delta):
        start = pl.multiple_of((i - delta) * tq, tq)
        bias = bias_ref[delta]
        _flash_tile(q_stack, k_ref[pl.ds(start, tq), :], v_ref[pl.ds(start, tq), :], 2,
                    lambda s: s + bias, m_ref, l_ref, acc_ref)

    @pl.when(i >= nband - 1)
    def _():
        for delta in range(nband):
            step(delta)

    @pl.when(i < nband - 1)
    def _():
        step(0)

        def body(delta, carry):
            step(delta)
            return carry

        lax.fori_loop(1, i + 1, body, 0)

    o_ref[...] = _flash_out(2, l_ref, acc_ref).astype(o_ref.dtype)


def _dil_attention(dq, dk, dv, bias, batch, seq):
    T = dq.shape[0]
    tq = TQ
    nq = seq // tq
    npair = HEADS_DIL // 2
    return pl.pallas_call(
        _dil_kernel,
        grid=(batch, npair, nq),
        in_specs=[
            pl.BlockSpec((tq, LANES), lambda b, p, i: (b * nq + i, p)),
            _resident((seq, LANES), lambda b, p, i: (b, p)),
            _resident((seq, LANES), lambda b, p, i: (b, p)),
            _resident(bias.shape, lambda b, p, i: (0, 0, 0)),
        ],
        out_specs=pl.BlockSpec((tq, LANES), lambda b, p, i: (b * nq + i, p)),
        out_shape=jax.ShapeDtypeStruct((T, HEADS_DIL * HEAD_DIM), BF16),
        scratch_shapes=[pltpu.VMEM((2, tq, LANES), F32)] * 3,
        compiler_params=_cparams(("parallel", "parallel", "arbitrary")),
        name="dil_attn",
    )(dq, dk, dv, bias)


_INT_MIN = -(2 ** 31)
_KEY_NEG_INF = _INT_MIN + 0x7FFFFF
INTERP_MIN_WIDTH = 4096
INTERP_MIN_INDEX = 32


def _ordered_bits(x):
    return x ^ ((x >> 31) & 0x7FFFFFFF)


def _key_to_float(key):
    return lax.bitcast_convert_type(_ordered_bits(key), F32)


def _float_to_key(x):
    return _ordered_bits(lax.bitcast_convert_type(x, jnp.int32))


def _dsa_kernel(iq_ref, iwt_ref, ik_ref, q_ref, k_ref, v_ref, o_ref,
                sc_ref, cm_ref, m_ref, l_ref, acc_ref, *, ksel):
    tq = q_ref.shape[0]
    i = pl.program_id(1)
    ntile = i + 1
    key_row = lax.broadcasted_iota(jnp.int32, (tq, tq), 0)
    causal = key_row <= lax.broadcasted_iota(jnp.int32, (tq, tq), 1)
    neg_inf = jnp.float32(-jnp.inf)
    as_f32 = lambda v: lax.bitcast_convert_type(v, F32)
    as_i32 = lambda v: lax.bitcast_convert_type(v, jnp.int32)

    iq_stack = jnp.concatenate(
        [jnp.where((lax.broadcasted_iota(jnp.int32, iq_ref.shape, 1) // IDX_DIM) == h,
                   iq_ref[...], jnp.zeros(iq_ref.shape, BF16)) for h in range(IDX_HEADS)], axis=0)
    wt = iwt_ref[...]
    cm_ref[...] = jnp.full(cm_ref.shape, neg_inf, F32)

    def score_tile(j, diag):
        kk = ik_ref[pl.ds(pl.multiple_of(j * tq, tq), tq), :]
        sc = jnp.zeros((tq, tq), F32)
        for half in range(2):
            nh = IDX_HEADS // 2
            d = _dot_nt(kk, iq_stack[half * nh * tq:(half + 1) * nh * tq])
            for hh in range(nh):
                h = half * nh + hh
                sc = sc + jnp.maximum(d[:, hh * tq:(hh + 1) * tq], 0.0) * wt[h:h + 1, :]
        if diag:
            sc = jnp.where(causal, sc, neg_inf)
        sc_ref[j] = as_i32(sc)
        cm_ref[...] = jnp.maximum(cm_ref[...], sc)

    def score_group(g, carry):
        for u in range(SCORE_UNROLL):
            score_tile(g * SCORE_UNROLL + u, False)
        return carry

    def score_body(j, carry):
        score_tile(j, False)
        return carry

    ngroup = i // SCORE_UNROLL
    lax.fori_loop(0, ngroup, score_group, 0)
    lax.fori_loop(ngroup * SCORE_UNROLL, i, score_body, 0)
    score_tile(i, True)

    def count_cols(hit, thr):
        def body(j, cnt):
            words = sc_ref[j]
            for g in range(tq // 8):
                cnt = cnt + jnp.where(hit(words[8 * g:8 * (g + 1)], thr), 1, 0)
            return cnt

        cnt = lax.fori_loop(0, ntile, body, jnp.zeros((8, tq), jnp.int32))
        return _slab(jnp.sum(cnt, axis=0, keepdims=True))

    def count_ge(thr):
        return count_cols(lambda words, t: as_f32(words) >= t, thr)

    cm = cm_ref[...]
    lo = jnp.maximum(_float_to_key(_slab(jnp.min(cm, axis=0, keepdims=True))), _KEY_NEG_INF)
    hi = _float_to_key(_slab(jnp.max(cm, axis=0, keepdims=True))) + 1
    clo = jnp.full((8, tq), 2 ** 30, jnp.int32)
    chi = jnp.zeros((8, tq), jnp.int32)
    key_zero = 0
    key_tiny = 0x00800000
    hi = jnp.where((hi > key_zero) & (hi < key_tiny), key_tiny, hi)

    def open_rows(lo, hi, clo):
        zero_class = (lo == key_zero) & (hi == key_tiny)
        return (clo != ksel) & (hi - 1 > lo) & jnp.logical_not(zero_class)

    def any_rows(mask):
        return jnp.max(jnp.where(mask, 1, 0))

    def probe_key(lo, hi, clo, chi):
        mid = (lo >> 1) + (hi >> 1) + (lo & hi & 1)
        straddle = (lo < key_zero) & (hi > key_zero)
        width = hi - lo
        lo_f = _key_to_float(lo)
        frac = (clo - ksel).astype(F32) / (clo - chi).astype(F32)
        guess = _float_to_key(lo_f + (_key_to_float(hi) - lo_f) * frac)
        margin = width >> 3
        guess = jnp.minimum(jnp.maximum(guess, lo + 1 + margin), hi - 1 - margin)
        use = ((clo < 2 ** 30) & (width > INTERP_MIN_WIDTH) & (lo > _KEY_NEG_INF)
               & jnp.logical_not(straddle))
        probe = jnp.where(use, guess, mid)
        probe = jnp.where(straddle, key_zero, probe)
        return jnp.where((lo == key_zero) & (hi > key_tiny), key_tiny, probe)

    def sel_body(st):
        _, lo, hi, clo, chi = st
        mid = probe_key(lo, hi, clo, chi)
        cnt = count_ge(_key_to_float(mid))
        ge = cnt >= ksel
        act = open_rows(lo, hi, clo)
        up = act & ge
        dn = act & jnp.logical_not(ge)
        lo = jnp.where(up, mid, lo)
        clo = jnp.where(up, cnt, clo)
        hi = jnp.where(dn, mid, hi)
        chi = jnp.where(dn, cnt, chi)
        return any_rows(open_rows(lo, hi, clo)), lo, hi, clo, chi

    _, lo, hi, clo, chi = lax.while_loop(
        lambda st: st[0] > 0, sel_body, (any_rows(open_rows(lo, hi, clo)), lo, hi, clo, chi))
    exact = clo == ksel
    upper = _key_to_float(jnp.where(exact, lo, hi))[:1]
    lower = _key_to_float(lo)[:1]

    big = jnp.int32(2 ** 30)

    def rank_body(j, carry):
        sc = as_f32(sc_ref[j])
        sc_ref[j] = jnp.where(sc >= upper, -1, jnp.where(sc >= lower, key_row + j * tq, big))
        return carry

    lax.fori_loop(0, ntile, rank_body, 0)

    ja = jnp.where(exact, -1, 0)
    jb = jnp.where(exact, 0, ntile * tq)
    ca = jnp.where(exact, ksel, chi)
    cb = clo

    def idx_body(st):
        _, ja, jb, ca, cb = st
        width = jb - ja
        frac = (ksel - ca).astype(F32) / (cb - ca).astype(F32)
        guess = ja + (width.astype(F32) * frac).astype(jnp.int32)
        margin = width >> 3
        guess = jnp.minimum(jnp.maximum(guess, ja + 1 + margin), jb - 1 - margin)
        mid = jnp.where((cb < 2 ** 30) & (width > INTERP_MIN_INDEX), guess, (ja + jb) >> 1)
        cnt = count_cols(lambda words, t: words < t, mid)
        act = width > 1
        hit = act & (cnt == ksel)
        dn = act & (cnt >= ksel)
        up = act & (cnt < ksel)
        jb = jnp.where(dn, mid, jb)
        cb = jnp.where(dn, cnt, cb)
        ja = jnp.where(hit, mid - 1, jnp.where(up, mid, ja))
        ca = jnp.where(up, cnt, ca)
        return any_rows(jb - ja > 1), ja, jb, ca, cb

    _, _, jb, _, _ = lax.while_loop(lambda st: st[0] > 0, idx_body,
                                   (any_rows(jb - ja > 1), ja, jb, ca, cb))
    bound = jb[:1]

    _flash_init(m_ref, l_ref, acc_ref)
    q_stack = _head_stack(q_ref[...], HEADS_DSA)

    def attn_tile(j, diag):
        start = pl.multiple_of(j * tq, tq)
        sel = sc_ref[j] < bound
        if diag:
            sel = sel & causal
        cap = jnp.where(sel, -NEG, NEG)
        _flash_tile_t(q_stack, k_ref[pl.ds(start, tq), :], v_ref[pl.ds(start, tq), :], HEADS_DSA,
                      lambda s: jnp.minimum(s, cap), m_ref, l_ref, acc_ref)

    def attn_body(j, carry):
        attn_tile(j, False)
        return carry

    lax.fori_loop(0, i, attn_body, 0)
    attn_tile(i, True)
    o_ref[...] = _flash_out_t(HEADS_DSA, l_ref, acc_ref).astype(o_ref.dtype)


def _dsa_attention(iq, iw, ik, sq, sk, sv, batch, seq):
    T = sq.shape[0]
    tq = TQ
    nq = seq // tq
    ksel = min(DSA_TOPK, seq // 4)
    assert tq >= ksel
    width = HEADS_DSA * HEAD_DIM
    qrow = lambda b, i: (b * nq + i, 0)
    full = lambda b, i: (b, 0)
    return pl.pallas_call(
        functools.partial(_dsa_kernel, ksel=ksel),
        grid=(batch, nq),
        in_specs=[
            pl.BlockSpec((tq, IDX_HEADS * IDX_DIM), qrow),
            pl.BlockSpec((IDX_HEADS, tq), lambda b, i: (0, b * nq + i)),
            _resident((seq, IDX_HEADS * IDX_DIM), full),
            pl.BlockSpec((tq, width), qrow),
            _resident((seq, width), full),
            _resident((seq, width), full),
        ],
        out_specs=pl.BlockSpec((tq, width), qrow),
        out_shape=jax.ShapeDtypeStruct((T, width), BF16),
        scratch_shapes=[
            pltpu.VMEM((nq, tq, tq), jnp.int32),
            pltpu.VMEM((tq, tq), F32),
            pltpu.VMEM((HEADS_DSA, 8, tq), F32),
            pltpu.VMEM((HEADS_DSA, 8, tq), F32),
            pltpu.VMEM((HEADS_DSA, tq, width), F32),
        ],
        compiler_params=_cparams(("parallel", "arbitrary")),
        name="dsa_attn",
    )(iq, jnp.transpose(iw[:, :IDX_HEADS]), ik, sq, sk, sv)


def _oproj_kernel(x_ref, oa_ref, ob_ref, oc_ref, wo_ref, g_ref, x1_ref, h_ref):
    na = HEADS_MLA * HEAD_DIM
    nb = HEADS_DIL * HEAD_DIM
    x1 = (x_ref[...] + _dot(oa_ref[...], wo_ref[:na, :])
          + _dot(ob_ref[...], wo_ref[na:na + nb, :])
          + _dot(oc_ref[...], wo_ref[na + nb:, :]))
    x1_ref[...] = x1
    h_ref[...] = _rms(x1, g_ref[...]).astype(BF16)


def _out_projection(x2, oa, ob, oc, wo, g):
    T = x2.shape[0]
    tm = TM_OPROJ
    row = lambda i: (i, 0)
    const = lambda i: (0, 0)
    return pl.pallas_call(
        _oproj_kernel,
        grid=(T // tm,),
        in_specs=[
            pl.BlockSpec((tm, D_MODEL), row),
            pl.BlockSpec((tm, oa.shape[1]), row),
            pl.BlockSpec((tm, ob.shape[1]), row),
            pl.BlockSpec((tm, oc.shape[1]), row),
            pl.BlockSpec(wo.shape, const),
            pl.BlockSpec((1, D_MODEL), const),
        ],
        out_specs=[pl.BlockSpec((tm, D_MODEL), row), pl.BlockSpec((tm, D_MODEL), row)],
        out_shape=[jax.ShapeDtypeStruct((T, D_MODEL), F32), jax.ShapeDtypeStruct((T, D_MODEL), BF16)],
        compiler_params=_cparams(("parallel",)),
        name="oproj",
    )(x2, oa, ob, oc, wo, g)


def _ffn_kernel(h_ref, halo_ref, x1_ref, wg_ref, wu_ref, cg_ref, cu_ref, wd_ref, *rest,
                blocks_per_seq, final_norm):
    gf_ref = rest[0] if final_norm else None
    o_ref, uga_ref, uua_ref, ugb_ref, uub_ref, acc_ref = rest[1:] if final_norm else rest
    tm = h_ref.shape[0]
    hal = halo_ref.shape[0]
    nchunk = wg_ref.shape[0]
    first = (pl.program_id(0) % blocks_per_seq) == 0
    keep = jnp.where(first, 0.0, 1.0).astype(F32)
    h = h_ref[...]
    hh = halo_ref[...]
    acc_ref[...] = jnp.zeros(acc_ref.shape, F32)

    def conv(u_ref, w):
        y = w[CONV_WIDTH:CONV_WIDTH + 1, :]
        for tap in range(CONV_WIDTH):
            off = hal - (CONV_WIDTH - 1) + tap
            y = y + u_ref[pl.ds(off, tm), :] * w[tap:tap + 1, :]
        return y

    def up_proj(j, slot):
        ug_ref, uu_ref = slot
        wg = wg_ref[j]
        wu = wu_ref[j]
        ug_ref[:hal, :] = _dot(hh, wg) * keep
        ug_ref[hal:, :] = _dot(h, wg)
        uu_ref[:hal, :] = _dot(hh, wu) * keep
        uu_ref[hal:, :] = _dot(h, wu)

    def down_proj(j, slot):
        ug_ref, uu_ref = slot
        gate = conv(ug_ref, cg_ref[j])
        up = conv(uu_ref, cu_ref[j])
        act = gate * jax.nn.sigmoid(gate) * up
        acc_ref[...] += _dot(act.astype(BF16), wd_ref[j])

    slot_a = (uga_ref, uua_ref)
    slot_b = (ugb_ref, uub_ref)
    up_proj(0, slot_a)

    def chunk_pair(jj, carry):
        j = 2 * jj
        up_proj(j + 1, slot_b)
        down_proj(j, slot_a)
        up_proj(j + 2, slot_a)
        down_proj(j + 1, slot_b)
        return carry

    lax.fori_loop(0, (nchunk - 1) // 2, chunk_pair, 0)
    if nchunk % 2 == 0:
        up_proj(nchunk - 1, slot_b)
        down_proj(nchunk - 2, slot_a)
        down_proj(nchunk - 1, slot_b)
    else:
        down_proj(nchunk - 1, slot_a)
    out = x1_ref[...] + acc_ref[...]
    o_ref[...] = out if gf_ref is None else _rms(out, gf_ref[...])


def _ffn(h2, x1, wg, wu, cg, cu, wd, seq, g_final=None):
    T = x1.shape[0]
    tm = TM_FFN
    hal = BF16_SUBLANES
    tn = wg.shape[2]
    row = lambda i: (i, 0)
    c3 = lambda i: (0, 0, 0)
    final_norm = g_final is not None
    in_specs = [
        pl.BlockSpec((tm, D_MODEL), row),
        pl.BlockSpec((hal, D_MODEL), lambda i: (jnp.maximum(i * (tm // hal) - 1, 0), 0)),
        pl.BlockSpec((tm, D_MODEL), row),
        _resident(wg.shape, c3),
        _resident(wu.shape, c3),
        _resident(cg.shape, c3),
        _resident(cu.shape, c3),
        _resident(wd.shape, c3),
    ]
    args = [h2, h2, x1, wg, wu, cg, cu, wd]
    if final_norm:
        in_specs.append(pl.BlockSpec((1, D_MODEL), lambda i: (0, 0)))
        args.append(g_final)
    return pl.pallas_call(
        functools.partial(_ffn_kernel, blocks_per_seq=seq // tm, final_norm=final_norm),
        grid=(T // tm,),
        in_specs=in_specs,
        out_specs=pl.BlockSpec((tm, D_MODEL), row),
        out_shape=jax.ShapeDtypeStruct((T, D_MODEL), F32),
        scratch_shapes=[pltpu.VMEM((hal + tm, tn), F32)] * 4 + [pltpu.VMEM((tm, D_MODEL), F32)],
        compiler_params=_cparams(("parallel",)),
        name="ffn",
    )(*args)


def _rope_tables(seq):
    def base(dim):
        inv = jnp.power(jnp.float32(ROPE_THETA), -jnp.arange(0, dim, 2, dtype=F32) / dim)
        ang = jnp.arange(seq, dtype=F32)[:, None] * inv[None, :]
        return jnp.cos(ang), jnp.sin(ang)

    def tables(dim, period):
        cos, sin = base(dim)
        half = dim // 2
        pad = period - dim
        c = jnp.concatenate([cos, cos, jnp.ones((seq, pad), F32)], axis=1)
        sn = jnp.concatenate([-sin, jnp.zeros((seq, half + pad), F32)], axis=1)
        sp = jnp.concatenate([jnp.zeros((seq, half), F32), sin, jnp.zeros((seq, pad), F32)], axis=1)
        reps = LANES // period
        return [jnp.tile(t, (1, reps)) for t in (c, sn, sp)]

    tabs = (tables(MLA_ROPE, LANES) + tables(HEAD_DIM // ROT_FRAC, HEAD_DIM)
            + tables(IDX_DIM // ROT_FRAC, IDX_DIM))
    return jnp.stack(tabs)


def _pad_cols(w, width):
    return jnp.pad(w, ((0, 0), (0, width - w.shape[1])))


def _layer_weights(w_in, w_uq, w_ukv, w_up, conv_w, conv_b, w_down):
    o = 0
    parts = {}
    for name, width in (("cq", Q_LORA), ("ckv", KV_LORA), ("kpe", MLA_ROPE),
                        ("dil", 3 * HEADS_DIL * HEAD_DIM), ("dsa", 3 * HEADS_DSA * HEAD_DIM),
                        ("qi", IDX_HEADS * IDX_DIM), ("ki", IDX_DIM), ("wi", IDX_HEADS)):
        parts[name] = w_in[:, o:o + width]
        o += width
    w1 = jnp.concatenate([
        parts["cq"], parts["ckv"], _pad_cols(parts["kpe"], LANES), parts["dil"], parts["dsa"],
        parts["qi"], jnp.tile(parts["ki"], (1, IDX_HEADS)), _pad_cols(parts["wi"], LANES)],
        axis=1).astype(BF16)
    assert w1.shape[1] == N_PROJ

    dq = MLA_NOPE + MLA_ROPE
    cols = []
    for hd in range(HEADS_MLA):
        nope = w_uq[:, dq * hd:dq * hd + MLA_NOPE]
        pe = w_uq[:, dq * hd + MLA_NOPE:dq * (hd + 1)]
        z = jnp.zeros((Q_LORA, MLA_NOPE), w_uq.dtype)
        slot = [nope, z] if hd % 2 == 0 else [z, nope]
        cols += slot + [_pad_cols(pe, LANES)]
    wuq = jnp.concatenate(cols, axis=1).astype(BF16)

    dkv = MLA_NOPE + HEAD_DIM
    kn = [w_ukv[:, dkv * hd:dkv * hd + MLA_NOPE] for hd in range(HEADS_MLA)]
    vv = [w_ukv[:, dkv * hd + MLA_NOPE:dkv * (hd + 1)] for hd in range(HEADS_MLA)]
    wukv = jnp.concatenate(kn + vv, axis=1).astype(BF16)

    nchunk = D_FF // TN_FFN
    def chunks(w):
        return w.reshape(w.shape[0], nchunk, TN_FFN).transpose(1, 0, 2)
    wg = chunks(w_up[:, :D_FF]).astype(BF16)
    wu = chunks(w_up[:, D_FF:]).astype(BF16)
    cw = jnp.concatenate([conv_w, conv_b[None, :],
                          jnp.zeros((8 - CONV_WIDTH - 1, 2 * D_FF), F32)], axis=0)
    cg = chunks(cw[:, :D_FF])
    cu = chunks(cw[:, D_FF:])
    wd = w_down.reshape(nchunk, TN_FFN, D_MODEL).astype(BF16)
    return w1, wuq, wukv, wg, wu, cg, cu, wd


def kernel(x, g_attn, w_in, g_q_lat, w_uq, g_kv_lat, w_ukv, w_o, g_ffn, w_up, conv_w, conv_b,
           w_down, g_final):
    batch, seq, _ = x.shape
    depth = w_in.shape[0]
    assert seq % DIL_BAND == 0 and seq % TM_FFN == 0 and seq % TM_OPROJ == 0 and seq % TQ == 0
    assert D_FF % TN_FFN == 0
    T = batch * seq
    tabs = _rope_tables(seq)
    bias = _dil_bias(TQ)
    xf = x.reshape(T, D_MODEL)
    for l in range(depth):
        w1, wuq, wukv, wg, wu, cg, cu, wd = _layer_weights(
            w_in[l], w_uq[l], w_ukv[l], w_up[l], conv_w[l], conv_b[l], w_down[l])
        (mq, mk, mv, dq, dk, dv, sq, sk, sv, iq, ik, iw) = _projection(
            xf, g_attn[l][None, :], w1, g_q_lat[l][None, :], wuq, g_kv_lat[l][None, :], wukv,
            tabs, seq)
        oa = _mla_attention(mq, mk, mv, batch, seq)
        ob = _dil_attention(dq, dk, dv, bias, batch, seq)
        oc = _dsa_attention(iq, iw, ik, sq, sk, sv, batch, seq)
        x1, h2 = _out_projection(xf, oa, ob, oc, w_o[l].astype(BF16), g_ffn[l][None, :])
        xf = _ffn(h2, x1, wg, wu, cg, cu, wd, seq,
                  g_final=g_final[None, :] if l == depth - 1 else None)
    return xf.reshape(batch, seq, D_MODEL)
```

```python
import functools
import math

import jax
import jax.numpy as jnp
from jax import lax
from jax.experimental import pallas as pl
from jax.experimental.pallas import tpu as pltpu

D_MODEL = 1024
HEAD_DIM = 64
HEADS_MLA = 4
HEADS_DIL = 8
HEADS_DSA = 4
Q_LORA = 256
KV_LORA = 128
MLA_NOPE = 64
MLA_ROPE = 32
DIL_PAIRS = ((128, 1), (512, 4), (2048, 16))
IDX_HEADS = 8
IDX_DIM = 32
DSA_TOPK = 256
D_FF = 2816
CONV_WIDTH = 3
ROPE_THETA = 500000.0
ROT_FRAC = 4
NORM_EPS = 1e-6

LANES = 128
BF16_SUBLANES = 16
VMEM_LIMIT = 56 * 1024 * 1024
NEG = -1e30
LOG2E = math.log2(math.e)
F32 = jnp.float32
BF16 = jnp.bfloat16

TQ = 512
TM_ROW = 1024
TM_OPROJ = 1024
TM_FFN = 512
TN_FFN = 1408
MLA_UNROLL = 4
SCORE_UNROLL = 4
DIL_BAND = max(w for w, _ in DIL_PAIRS)

C_CQ = 0
C_CKV = C_CQ + Q_LORA
C_KPE = C_CKV + KV_LORA
C_DIL = C_KPE + LANES
C_DSA = C_DIL + 3 * HEADS_DIL * HEAD_DIM
C_QI = C_DSA + 3 * HEADS_DSA * HEAD_DIM
C_KI = C_QI + IDX_HEADS * IDX_DIM
C_WI = C_KI + IDX_HEADS * IDX_DIM
N_PROJ = C_WI + LANES


def _cparams(sem):
    return pltpu.CompilerParams(dimension_semantics=sem, vmem_limit_bytes=VMEM_LIMIT)


def _resident(shape, index_map):
    return pl.BlockSpec(shape, index_map, pipeline_mode=pl.Buffered(1))


def _rms(x, g):
    return x * lax.rsqrt(jnp.mean(x * x, axis=-1, keepdims=True) + NORM_EPS) * g


def _dot(a, b):
    return jnp.dot(a, b, preferred_element_type=F32)


def _dot_nt(a, b):
    return lax.dot_general(a, b, (((1,), (1,)), ((), ())), preferred_element_type=F32)


def _rope(x, tab_ref, t0, half):
    c = tab_ref[t0]
    sn = tab_ref[t0 + 1]
    sp = tab_ref[t0 + 2]
    return (x * c + pltpu.roll(x, LANES - half, axis=1) * sn
            + pltpu.roll(x, half, axis=1) * sp)


def _proj_kernel(x_ref, g_ref, w1_ref, gq_ref, wuq_ref, gkv_ref, wukv_ref, tab_ref,
                 mq_ref, mk_ref, mv_ref, dq_ref, dk_ref, dv_ref,
                 sq_ref, sk_ref, sv_ref, iq_ref, ik_ref, iw_ref):
    h = _rms(x_ref[...], g_ref[...]).astype(BF16)

    def proj(lo, width):
        return _dot(h, w1_ref[:, lo:lo + width])

    mla_scale = (MLA_NOPE + MLA_ROPE) ** -0.5 * LOG2E
    cq = _rms(proj(C_CQ, Q_LORA), gq_ref[...]).astype(BF16)
    for hd in range(HEADS_MLA):
        base = 2 * LANES * hd
        q2 = _dot(cq, wuq_ref[:, base:base + 2 * LANES])
        qp = _rope(q2[:, LANES:], tab_ref, 0, MLA_ROPE // 2)
        mq_ref[:, base:base + LANES] = (q2[:, :LANES] * mla_scale).astype(BF16)
        mq_ref[:, base + LANES:base + 2 * LANES] = (qp * mla_scale).astype(BF16)
    zkv = proj(C_CKV, KV_LORA + LANES)
    ckv = _rms(zkv[:, :KV_LORA], gkv_ref[...]).astype(BF16)
    kpe = _rope(zkv[:, KV_LORA:], tab_ref, 0, MLA_ROPE // 2).astype(BF16)
    kv = _dot(ckv, wukv_ref[...])
    for pr in range(HEADS_MLA // 2):
        mk_ref[:, 2 * LANES * pr:2 * LANES * pr + LANES] = kv[:, LANES * pr:LANES * (pr + 1)].astype(BF16)
        mk_ref[:, 2 * LANES * pr + LANES:2 * LANES * (pr + 1)] = kpe
    mv_ref[...] = kv[:, HEADS_MLA * MLA_NOPE:].astype(BF16)

    head_scale = HEAD_DIM ** -0.5 * LOG2E
    half_h = HEAD_DIM // ROT_FRAC // 2

    def roped(c0, width, t0, half, scale, o_ref):
        step = min(width, 2 * LANES)
        for c in range(0, width, step):
            z = proj(c0 + c, step)
            for off in range(0, step, LANES):
                r = _rope(z[:, off:off + LANES], tab_ref, t0, half)
                if scale is not None:
                    r = r * scale
                o_ref[:, c + off:c + off + LANES] = r.astype(BF16)

    def qkv(c0, nheads, q_ref, k_ref, v_ref):
        width = nheads * HEAD_DIM
        roped(c0, width, 3, half_h, head_scale, q_ref)
        roped(c0 + width, width, 3, half_h, None, k_ref)
        v_ref[...] = proj(c0 + 2 * width, width).astype(BF16)

    qkv(C_DIL, HEADS_DIL, dq_ref, dk_ref, dv_ref)
    qkv(C_DSA, HEADS_DSA, sq_ref, sk_ref, sv_ref)

    half_i = IDX_DIM // ROT_FRAC // 2
    roped(C_QI, IDX_HEADS * IDX_DIM, 6, half_i, None, iq_ref)
    roped(C_KI, IDX_HEADS * IDX_DIM, 6, half_i, None, ik_ref)
    iw_ref[...] = proj(C_WI, LANES) * ((IDX_HEADS * IDX_DIM) ** -0.5)


def _projection(x2, g, w1, gq, wuq, gkv, wukv, tabs, seq):
    T = x2.shape[0]
    tm = TM_ROW
    nsb = seq // tm
    row = lambda i: (i, 0)
    const = lambda i: (0, 0)
    widths = [2 * LANES * HEADS_MLA, 2 * LANES * (HEADS_MLA // 2), HEADS_MLA * HEAD_DIM,
              HEADS_DIL * HEAD_DIM, HEADS_DIL * HEAD_DIM, HEADS_DIL * HEAD_DIM,
              HEADS_DSA * HEAD_DIM, HEADS_DSA * HEAD_DIM, HEADS_DSA * HEAD_DIM,
              IDX_HEADS * IDX_DIM, IDX_HEADS * IDX_DIM]
    out_shape = [jax.ShapeDtypeStruct((T, w), BF16) for w in widths]
    out_shape.append(jax.ShapeDtypeStruct((T, LANES), F32))
    out_specs = [pl.BlockSpec((tm, w), row) for w in widths] + [pl.BlockSpec((tm, LANES), row)]
    return pl.pallas_call(
        _proj_kernel,
        grid=(T // tm,),
        in_specs=[
            pl.BlockSpec((tm, D_MODEL), row),
            pl.BlockSpec((1, D_MODEL), const),
            pl.BlockSpec(w1.shape, const),
            pl.BlockSpec((1, Q_LORA), const),
            pl.BlockSpec(wuq.shape, const),
            pl.BlockSpec((1, KV_LORA), const),
            pl.BlockSpec(wukv.shape, const),
            pl.BlockSpec((9, tm, LANES), lambda i: (0, i % nsb, 0)),
        ],
        out_specs=out_specs,
        out_shape=out_shape,
        compiler_params=_cparams(("parallel",)),
        name="proj",
    )(x2, g, w1, gq, wuq, gkv, wukv, tabs)


def _rep(x, n):
    return x if n == 1 else jnp.concatenate([x] * n, axis=1)


def _flash_tile(q_stack, k, v, nheads, mask_fn, m_ref, l_ref, acc_ref):
    tq = q_stack.shape[0] // nheads
    tk = k.shape[0]
    dv = v.shape[1]
    s_all = _dot_nt(q_stack, k)
    ps = []
    alphas = []
    for h in range(nheads):
        s = mask_fn(s_all[h * tq:(h + 1) * tq])
        m_prev = m_ref[h]
        m_new = jnp.maximum(m_prev, jnp.max(s, axis=1, keepdims=True))
        alpha = jnp.exp2(m_prev - m_new)
        p = jnp.exp2(s - _rep(m_new, tk // LANES))
        l_ref[h] = alpha * l_ref[h] + jnp.sum(p, axis=1, keepdims=True)
        m_ref[h] = m_new
        ps.append(p.astype(BF16))
        alphas.append(alpha)
    pv = _dot(jnp.concatenate(ps, axis=0), v)
    for h in range(nheads):
        acc_ref[h] = _rep(alphas[h], dv // LANES) * acc_ref[h] + pv[h * tq:(h + 1) * tq]


def _flash_init(m_ref, l_ref, acc_ref):
    m_ref[...] = jnp.full(m_ref.shape, NEG, F32)
    l_ref[...] = jnp.zeros(l_ref.shape, F32)
    acc_ref[...] = jnp.zeros(acc_ref.shape, F32)


def _flash_out(nheads, l_ref, acc_ref):
    tq, dv = acc_ref.shape[1:]
    lane = lax.broadcasted_iota(jnp.int32, (tq, dv), 1)
    out = jnp.zeros((tq, dv), F32)
    for h in range(nheads):
        out = jnp.where((lane // HEAD_DIM) == h, acc_ref[h] / _rep(l_ref[h], dv // LANES), out)
    return out


def _slab(x):
    return jnp.broadcast_to(x, (8, x.shape[1]))


def _to_rows(x, width):
    tq = x.shape[1]
    return _rep(jnp.transpose(jnp.broadcast_to(x[:1], (LANES, tq))), width // LANES)


def _flash_tile_t(q_stack, k, v, nheads, mask_fn, m_ref, l_ref, acc_ref):
    tq = q_stack.shape[0] // nheads
    dv = v.shape[1]
    s_all = _dot_nt(k, q_stack)
    ps = []
    alphas = []
    for h in range(nheads):
        s = mask_fn(s_all[:, h * tq:(h + 1) * tq])
        m_prev = m_ref[h]
        m_new = jnp.maximum(m_prev, _slab(jnp.max(s, axis=0, keepdims=True)))
        alpha = jnp.exp2(m_prev - m_new)
        p = jnp.exp2(s - m_new[:1])
        l_ref[h] = alpha * l_ref[h] + _slab(jnp.sum(p, axis=0, keepdims=True))
        m_ref[h] = m_new
        ps.append(p.astype(BF16))
        alphas.append(alpha)
    pv = lax.dot_general(jnp.concatenate(ps, axis=1), v, (((0,), (0,)), ((), ())),
                         preferred_element_type=F32)
    for h in range(nheads):
        acc_ref[h] = _to_rows(alphas[h], dv) * acc_ref[h] + pv[h * tq:(h + 1) * tq]


def _flash_out_t(nheads, l_ref, acc_ref):
    tq, dv = acc_ref.shape[1:]
    lane = lax.broadcasted_iota(jnp.int32, (tq, dv), 1)
    out = jnp.zeros((tq, dv), F32)
    for h in range(nheads):
        out = jnp.where((lane // HEAD_DIM) == h, acc_ref[h] / _to_rows(l_ref[h], dv), out)
    return out


def _head_stack(q, nheads):
    lane = lax.broadcasted_iota(jnp.int32, q.shape, 1)
    zero = jnp.zeros_like(q)
    return jnp.concatenate([jnp.where((lane // HEAD_DIM) == h, q, zero) for h in range(nheads)], axis=0)


def _causal_tile(tq):
    r = lax.broadcasted_iota(jnp.int32, (tq, tq), 0)
    c = lax.broadcasted_iota(jnp.int32, (tq, tq), 1)
    return r >= c


def _mla_kernel(q_ref, k_ref, v_ref, o_ref, m_ref, l_ref, acc_ref):
    tq = q_ref.shape[0]
    i = pl.program_id(2)
    _flash_init(m_ref, l_ref, acc_ref)
    q_stack = jnp.concatenate([q_ref[:, :2 * LANES], q_ref[:, 2 * LANES:]], axis=0)

    def step(j, mask_fn):
        start = pl.multiple_of(j * tq, tq)
        _flash_tile(q_stack, k_ref[pl.ds(start, tq), :], v_ref[pl.ds(start, tq), :], 2,
                    mask_fn, m_ref, l_ref, acc_ref)

    def group(g, carry):
        for u in range(MLA_UNROLL):
            step(g * MLA_UNROLL + u, lambda s: s)
        return carry

    def single(j, carry):
        step(j, lambda s: s)
        return carry

    ngroup = i // MLA_UNROLL
    lax.fori_loop(0, ngroup, group, 0)
    lax.fori_loop(ngroup * MLA_UNROLL, i, single, 0)
    step(i, lambda s: jnp.where(_causal_tile(tq), s, NEG))
    o_ref[...] = _flash_out(2, l_ref, acc_ref).astype(o_ref.dtype)


def _mla_attention(mq, mk, mv, batch, seq):
    T = mq.shape[0]
    tq = TQ
    nq = seq // tq
    npair = HEADS_MLA // 2
    return pl.pallas_call(
        _mla_kernel,
        grid=(batch, npair, nq),
        in_specs=[
            pl.BlockSpec((tq, 4 * LANES), lambda b, p, i: (b * nq + i, p)),
            _resident((seq, 2 * LANES), lambda b, p, i: (b, p)),
            _resident((seq, LANES), lambda b, p, i: (b, p)),
        ],
        out_specs=pl.BlockSpec((tq, LANES), lambda b, p, i: (b * nq + i, p)),
        out_shape=jax.ShapeDtypeStruct((T, HEADS_MLA * HEAD_DIM), BF16),
        scratch_shapes=[pltpu.VMEM((2, tq, LANES), F32)] * 3,
        compiler_params=_cparams(("parallel", "parallel", "arbitrary")),
        name="mla_attn",
    )(mq, mk, mv)


def _dil_bias(tq):
    nband = DIL_BAND // tq + 1
    d = (lax.broadcasted_iota(jnp.int32, (nband, tq, tq), 0) * tq
         + lax.broadcasted_iota(jnp.int32, (nband, tq, tq), 1)
         - lax.broadcasted_iota(jnp.int32, (nband, tq, tq), 2))
    cnt = jnp.zeros(d.shape, F32)
    for window, dil in DIL_PAIRS:
        ok = (d >= 0) & (d % dil == 0) & (d <= (window // dil) * dil)
        cnt = cnt + ok.astype(F32)
    return jnp.where(cnt > 0, jnp.log2(jnp.maximum(cnt, 1.0)), NEG)


def _dil_kernel(q_ref, k_ref, v_ref, bias_ref, o_ref, m_ref, l_ref, acc_ref):
    tq = q_ref.shape[0]
    nband = bias_ref.shape[0]
    i = pl.program_id(2)
    _flash_init(m_ref, l_ref, acc_ref)
    q_stack = _head_stack(q_ref[...], 2)

    def step(delta):
        start = pl.multiple_of((i - delta) * tq, tq)
        bias = bias_ref[delta]
        _flash_tile(q_stack, k_ref[pl.ds(start, tq), :], v_ref[pl.ds(start, tq), :], 2,
                    lambda s: s + bias, m_ref, l_ref, acc_ref)

    @pl.when(i >= nband - 1)
    def _():
        for delta in range(nband):
            step(delta)

    @pl.when(i < nband - 1)
    def _():
        step(0)

        def body(delta, carry):
            step(delta)
            return carry

        lax.fori_loop(1, i + 1, body, 0)

    o_ref[...] = _flash_out(2, l_ref, acc_ref).astype(o_ref.dtype)


def _dil_attention(dq, dk, dv, bias, batch, seq):
    T = dq.shape[0]
    tq = TQ
    nq = seq // tq
    npair = HEADS_DIL // 2
    return pl.pallas_call(
        _dil_kernel,
        grid=(batch, npair, nq),
        in_specs=[
            pl.BlockSpec((tq, LANES), lambda b, p, i: (b * nq + i, p)),
            _resident((seq, LANES), lambda b, p, i: (b, p)),
            _resident((seq, LANES), lambda b, p, i: (b, p)),
            _resident(bias.shape, lambda b, p, i: (0, 0, 0)),
        ],
        out_specs=pl.BlockSpec((tq, LANES), lambda b, p, i: (b * nq + i, p)),
        out_shape=jax.ShapeDtypeStruct((T, HEADS_DIL * HEAD_DIM), BF16),
        scratch_shapes=[pltpu.VMEM((2, tq, LANES), F32)] * 3,
        compiler_params=_cparams(("parallel", "parallel", "arbitrary")),
        name="dil_attn",
    )(dq, dk, dv, bias)


_INT_MIN = -(2 ** 31)
_KEY_NEG_INF = _INT_MIN + 0x7FFFFF
INTERP_MIN_WIDTH = 4096
INTERP_MIN_INDEX = 32


def _ordered_bits(x):
    return x ^ ((x >> 31) & 0x7FFFFFFF)


def _key_to_float(key):
    return lax.bitcast_convert_type(_ordered_bits(key), F32)


def _float_to_key(x):
    return _ordered_bits(lax.bitcast_convert_type(x, jnp.int32))


def _dsa_kernel(iq_ref, iwt_ref, ik_ref, q_ref, k_ref, v_ref, o_ref,
                sc_ref, cm_ref, m_ref, l_ref, acc_ref, *, ksel):
    tq = q_ref.shape[0]
    i = pl.program_id(1)
    ntile = i + 1
    key_row = lax.broadcasted_iota(jnp.int32, (tq, tq), 0)
    causal = key_row <= lax.broadcasted_iota(jnp.int32, (tq, tq), 1)
    neg_inf = jnp.float32(-jnp.inf)
    as_f32 = lambda v: lax.bitcast_convert_type(v, F32)
    as_i32 = lambda v: lax.bitcast_convert_type(v, jnp.int32)

    iq_stack = jnp.concatenate(
        [jnp.where((lax.broadcasted_iota(jnp.int32, iq_ref.shape, 1) // IDX_DIM) == h,
                   iq_ref[...], jnp.zeros(iq_ref.shape, BF16)) for h in range(IDX_HEADS)], axis=0)
    wt = iwt_ref[...]
    cm_ref[...] = jnp.full(cm_ref.shape, neg_inf, F32)

    def score_tile(j, diag):
        kk = ik_ref[pl.ds(pl.multiple_of(j * tq, tq), tq), :]
        sc = jnp.zeros((tq, tq), F32)
        for half in range(2):
            nh = IDX_HEADS // 2
            d = _dot_nt(kk, iq_stack[half * nh * tq:(half + 1) * nh * tq])
            for hh in range(nh):
                h = half * nh + hh
                sc = sc + jnp.maximum(d[:, hh * tq:(hh + 1) * tq], 0.0) * wt[h:h + 1, :]
        if diag:
            sc = jnp.where(causal, sc, neg_inf)
        sc_ref[j] = as_i32(sc)
        cm_ref[...] = jnp.maximum(cm_ref[...], sc)

    def score_group(g, carry):
        for u in range(SCORE_UNROLL):
            score_tile(g * SCORE_UNROLL + u, False)
        return carry

    def score_body(j, carry):
        score_tile(j, False)
        return carry

    ngroup = i // SCORE_UNROLL
    lax.fori_loop(0, ngroup, score_group, 0)
    lax.fori_loop(ngroup * SCORE_UNROLL, i, score_body, 0)
    score_tile(i, True)

    def count_cols(hit, thr):
        def body(j, cnt):
            words = sc_ref[j]
            for g in range(tq // 8):
                cnt = cnt + jnp.where(hit(words[8 * g:8 * (g + 1)], thr), 1, 0)
            return cnt

        cnt = lax.fori_loop(0, ntile, body, jnp.zeros((8, tq), jnp.int32))
        return _slab(jnp.sum(cnt, axis=0, keepdims=True))

    def count_ge(thr):
        return count_cols(lambda words, t: as_f32(words) >= t, thr)

    cm = cm_ref[...]
    lo = jnp.maximum(_float_to_key(_slab(jnp.min(cm, axis=0, keepdims=True))), _KEY_NEG_INF)
    hi = _float_to_key(_slab(jnp.max(cm, axis=0, keepdims=True))) + 1
    clo = jnp.full((8, tq), 2 ** 30, jnp.int32)
    chi = jnp.zeros((8, tq), jnp.int32)
    key_zero = 0
    key_tiny = 0x00800000
    hi = jnp.where((hi > key_zero) & (hi < key_tiny), key_tiny, hi)

    def open_rows(lo, hi, clo):
        zero_class = (lo == key_zero) & (hi == key_tiny)
        return (clo != ksel) & (hi - 1 > lo) & jnp.logical_not(zero_class)

    def any_rows(mask):
        return jnp.max(jnp.where(mask, 1, 0))

    def probe_key(lo, hi, clo, chi):
        mid = (lo >> 1) + (hi >> 1) + (lo & hi & 1)
        straddle = (lo < key_zero) & (hi > key_zero)
        width = hi - lo
        lo_f = _key_to_float(lo)
        frac = (clo - ksel).astype(F32) / (clo - chi).astype(F32)
        guess = _float_to_key(lo_f + (_key_to_float(hi) - lo_f) * frac)
        margin = width >> 3
        guess = jnp.minimum(jnp.maximum(guess, lo + 1 + margin), hi - 1 - margin)
        use = ((clo < 2 ** 30) & (width > INTERP_MIN_WIDTH) & (lo > _KEY_NEG_INF)
               & jnp.logical_not(straddle))
        probe = jnp.where(use, guess, mid)
        probe = jnp.where(straddle, key_zero, probe)
        return jnp.where((lo == key_zero) & (hi > key_tiny), key_tiny, probe)

    def sel_body(st):
        _, lo, hi, clo, chi = st
        mid = probe_key(lo, hi, clo, chi)
        cnt = count_ge(_key_to_float(mid))
        ge = cnt >= ksel
        act = open_rows(lo, hi, clo)
        up = act & ge
        dn = act & jnp.logical_not(ge)
        lo = jnp.where(up, mid, lo)
        clo = jnp.where(up, cnt, clo)
        hi = jnp.where(dn, mid, hi)
        chi = jnp.where(dn, cnt, chi)
        return any_rows(open_rows(lo, hi, clo)), lo, hi, clo, chi

    _, lo, hi, clo, chi = lax.while_loop(
        lambda st: st[0] > 0, sel_body, (any_rows(open_rows(lo, hi, clo)), lo, hi, clo, chi))
    exact = clo == ksel
    upper = _key_to_float(jnp.where(exact, lo, hi))[:1]
    lower = _key_to_float(lo)[:1]

    big = jnp.int32(2 ** 30)

    def rank_body(j, carry):
        sc = as_f32(sc_ref[j])
        sc_ref[j] = jnp.where(sc >= upper, -1, jnp.where(sc >= lower, key_row + j * tq, big))
        return carry

    lax.fori_loop(0, ntile, rank_body, 0)

    ja = jnp.where(exact, -1, 0)
    jb = jnp.where(exact, 0, ntile * tq)
    ca = jnp.where(exact, ksel, chi)
    cb = clo

    def idx_body(st):
        _, ja, jb, ca, cb = st
        width = jb - ja
        frac = (ksel - ca).astype(F32) / (cb - ca).astype(F32)
        guess = ja + (width.astype(F32) * frac).astype(jnp.int32)
        margin = width >> 3
        guess = jnp.minimum(jnp.maximum(guess, ja + 1 + margin), jb - 1 - margin)
        mid = jnp.where((cb < 2 ** 30) & (width > INTERP_MIN_INDEX), guess, (ja + jb) >> 1)
        cnt = count_cols(lambda words, t: words < t, mid)
        act = width > 1
        hit = act & (cnt == ksel)
        dn = act & (cnt >= ksel)
        up = act & (cnt < ksel)
        jb = jnp.where(dn, mid, jb)
        cb = jnp.where(dn, cnt, cb)
        ja = jnp.where(hit, mid - 1, jnp.where(up, mid, ja))
        ca = jnp.where(up, cnt, ca)
        return any_rows(jb - ja > 1), ja, jb, ca, cb

    _, _, jb, _, _ = lax.while_loop(lambda st: st[0] > 0, idx_body,
                                   (any_rows(jb - ja > 1), ja, jb, ca, cb))
    bound = jb[:1]

    _flash_init(m_ref, l_ref, acc_ref)
    q_stack = _head_stack(q_ref[...], HEADS_DSA)

    def attn_tile(j, diag):
        start = pl.multiple_of(j * tq, tq)
        sel = sc_ref[j] < bound
        if diag:
            sel = sel & causal
        cap = jnp.where(sel, -NEG, NEG)
        _flash_tile_t(q_stack, k_ref[pl.ds(start, tq), :], v_ref[pl.ds(start, tq), :], HEADS_DSA,
                      lambda s: jnp.minimum(s, cap), m_ref, l_ref, acc_ref)

    def attn_body(j, carry):
        attn_tile(j, False)
        return carry

    lax.fori_loop(0, i, attn_body, 0)
    attn_tile(i, True)
    o_ref[...] = _flash_out_t(HEADS_DSA, l_ref, acc_ref).astype(o_ref.dtype)


def _dsa_attention(iq, iw, ik, sq, sk, sv, batch, seq):
    T = sq.shape[0]
    tq = TQ
    nq = seq // tq
    ksel = min(DSA_TOPK, seq // 4)
    assert tq >= ksel
    width = HEADS_DSA * HEAD_DIM
    qrow = lambda b, i: (b * nq + i, 0)
    full = lambda b, i: (b, 0)
    return pl.pallas_call(
        functools.partial(_dsa_kernel, ksel=ksel),
        grid=(batch, nq),
        in_specs=[
            pl.BlockSpec((tq, IDX_HEADS * IDX_DIM), qrow),
            pl.BlockSpec((IDX_HEADS, tq), lambda b, i: (0, b * nq + i)),
            _resident((seq, IDX_HEADS * IDX_DIM), full),
            pl.BlockSpec((tq, width), qrow),
            _resident((seq, width), full),
            _resident((seq, width), full),
        ],
        out_specs=pl.BlockSpec((tq, width), qrow),
        out_shape=jax.ShapeDtypeStruct((T, width), BF16),
        scratch_shapes=[
            pltpu.VMEM((nq, tq, tq), jnp.int32),
            pltpu.VMEM((tq, tq), F32),
            pltpu.VMEM((HEADS_DSA, 8, tq), F32),
            pltpu.VMEM((HEADS_DSA, 8, tq), F32),
            pltpu.VMEM((HEADS_DSA, tq, width), F32),
        ],
        compiler_params=_cparams(("parallel", "arbitrary")),
        name="dsa_attn",
    )(iq, jnp.transpose(iw[:, :IDX_HEADS]), ik, sq, sk, sv)


def _oproj_kernel(x_ref, oa_ref, ob_ref, oc_ref, wo_ref, g_ref, x1_ref, h_ref):
    na = HEADS_MLA * HEAD_DIM
    nb = HEADS_DIL * HEAD_DIM
    x1 = (x_ref[...] + _dot(oa_ref[...], wo_ref[:na, :])
          + _dot(ob_ref[...], wo_ref[na:na + nb, :])
          + _dot(oc_ref[...], wo_ref[na + nb:, :]))
    x1_ref[...] = x1
    h_ref[...] = _rms(x1, g_ref[...]).astype(BF16)


def _out_projection(x2, oa, ob, oc, wo, g):
    T = x2.shape[0]
    tm = TM_OPROJ
    row = lambda i: (i, 0)
    const = lambda i: (0, 0)
    return pl.pallas_call(
        _oproj_kernel,
        grid=(T // tm,),
        in_specs=[
            pl.BlockSpec((tm, D_MODEL), row),
            pl.BlockSpec((tm, oa.shape[1]), row),
            pl.BlockSpec((tm, ob.shape[1]), row),
            pl.BlockSpec((tm, oc.shape[1]), row),
            pl.BlockSpec(wo.shape, const),
            pl.BlockSpec((1, D_MODEL), const),
        ],
        out_specs=[pl.BlockSpec((tm, D_MODEL), row), pl.BlockSpec((tm, D_MODEL), row)],
        out_shape=[jax.ShapeDtypeStruct((T, D_MODEL), F32), jax.ShapeDtypeStruct((T, D_MODEL), BF16)],
        compiler_params=_cparams(("parallel",)),
        name="oproj",
    )(x2, oa, ob, oc, wo, g)


def _ffn_kernel(h_ref, halo_ref, x1_ref, wg_ref, wu_ref, cg_ref, cu_ref, wd_ref, *rest,
                blocks_per_seq, final_norm):
    gf_ref = rest[0] if final_norm else None
    o_ref, uga_ref, uua_ref, ugb_ref, uub_ref, acc_ref = rest[1:] if final_norm else rest
    tm = h_ref.shape[0]
    hal = halo_ref.shape[0]
    nchunk = wg_ref.shape[0]
    first = (pl.program_id(0) % blocks_per_seq) == 0
    keep = jnp.where(first, 0.0, 1.0).astype(F32)
    h = h_ref[...]
    hh = halo_ref[...]
    acc_ref[...] = jnp.zeros(acc_ref.shape, F32)

    def conv(u_ref, w):
        y = w[CONV_WIDTH:CONV_WIDTH + 1, :]
        for tap in range(CONV_WIDTH):
            off = hal - (CONV_WIDTH - 1) + tap
            y = y + u_ref[pl.ds(off, tm), :] * w[tap:tap + 1, :]
        return y

    def up_proj(j, slot):
        ug_ref, uu_ref = slot
        wg = wg_ref[j]
        wu = wu_ref[j]
        ug_ref[:hal, :] = _dot(hh, wg) * keep
        ug_ref[hal:, :] = _dot(h, wg)
        uu_ref[:hal, :] = _dot(hh, wu) * keep
        uu_ref[hal:, :] = _dot(h, wu)

    def down_proj(j, slot):
        ug_ref, uu_ref = slot
        gate = conv(ug_ref, cg_ref[j])
        up = conv(uu_ref, cu_ref[j])
        act = gate * jax.nn.sigmoid(gate) * up
        acc_ref[...] += _dot(act.astype(BF16), wd_ref[j])

    slot_a = (uga_ref, uua_ref)
    slot_b = (ugb_ref, uub_ref)
    up_proj(0, slot_a)

    def chunk_pair(jj, carry):
        j = 2 * jj
        up_proj(j + 1, slot_b)
        down_proj(j, slot_a)
        up_proj(j + 2, slot_a)
        down_proj(j + 1, slot_b)
        return carry

    lax.fori_loop(0, (nchunk - 1) // 2, chunk_pair, 0)
    if nchunk % 2 == 0:
        up_proj(nchunk - 1, slot_b)
        down_proj(nchunk - 2, slot_a)
        down_proj(nchunk - 1, slot_b)
    else:
        down_proj(nchunk - 1, slot_a)
    out = x1_ref[...] + acc_ref[...]
    o_ref[...] = out if gf_ref is None else _rms(out, gf_ref[...])


def _ffn(h2, x1, wg, wu, cg, cu, wd, seq, g_final=None):
    T = x1.shape[0]
    tm = TM_FFN
    hal = BF16_SUBLANES
    tn = wg.shape[2]
    row = lambda i: (i, 0)
    c3 = lambda i: (0, 0, 0)
    final_norm = g_final is not None
    in_specs = [
        pl.BlockSpec((tm, D_MODEL), row),
        pl.BlockSpec((hal, D_MODEL), lambda i: (jnp.maximum(i * (tm // hal) - 1, 0), 0)),
        pl.BlockSpec((tm, D_MODEL), row),
        _resident(wg.shape, c3),
        _resident(wu.shape, c3),
        _resident(cg.shape, c3),
        _resident(cu.shape, c3),
        _resident(wd.shape, c3),
    ]
    args = [h2, h2, x1, wg, wu, cg, cu, wd]
    if final_norm:
        in_specs.append(pl.BlockSpec((1, D_MODEL), lambda i: (0, 0)))
        args.append(g_final)
    return pl.pallas_call(
        functools.partial(_ffn_kernel, blocks_per_seq=seq // tm, final_norm=final_norm),
        grid=(T // tm,),
        in_specs=in_specs,
        out_specs=pl.BlockSpec((tm, D_MODEL), row),
        out_shape=jax.ShapeDtypeStruct((T, D_MODEL), F32),
        scratch_shapes=[pltpu.VMEM((hal + tm, tn), F32)] * 4 + [pltpu.VMEM((tm, D_MODEL), F32)],
        compiler_params=_cparams(("parallel",)),
        name="ffn",
    )(*args)


def _rope_tables(seq):
    def base(dim):
        inv = jnp.power(jnp.float32(ROPE_THETA), -jnp.arange(0, dim, 2, dtype=F32) / dim)
        ang = jnp.arange(seq, dtype=F32)[:, None] * inv[None, :]
        return jnp.cos(ang), jnp.sin(ang)

    def tables(dim, period):
        cos, sin = base(dim)
        half = dim // 2
        pad = period - dim
        c = jnp.concatenate([cos, cos, jnp.ones((seq, pad), F32)], axis=1)
        sn = jnp.concatenate([-sin, jnp.zeros((seq, half + pad), F32)], axis=1)
        sp = jnp.concatenate([jnp.zeros((seq, half), F32), sin, jnp.zeros((seq, pad), F32)], axis=1)
        reps = LANES // period
        return [jnp.tile(t, (1, reps)) for t in (c, sn, sp)]

    tabs = (tables(MLA_ROPE, LANES) + tables(HEAD_DIM // ROT_FRAC, HEAD_DIM)
            + tables(IDX_DIM // ROT_FRAC, IDX_DIM))
    return jnp.stack(tabs)


def _pad_cols(w, width):
    return jnp.pad(w, ((0, 0), (0, width - w.shape[1])))


def _layer_weights(w_in, w_uq, w_ukv, w_up, conv_w, conv_b, w_down):
    o = 0
    parts = {}
    for name, width in (("cq", Q_LORA), ("ckv", KV_LORA), ("kpe", MLA_ROPE),
                        ("dil", 3 * HEADS_DIL * HEAD_DIM), ("dsa", 3 * HEADS_DSA * HEAD_DIM),
                        ("qi", IDX_HEADS * IDX_DIM), ("ki", IDX_DIM), ("wi", IDX_HEADS)):
        parts[name] = w_in[:, o:o + width]
        o += width
    w1 = jnp.concatenate([
        parts["cq"], parts["ckv"], _pad_cols(parts["kpe"], LANES), parts["dil"], parts["dsa"],
        parts["qi"], jnp.tile(parts["ki"], (1, IDX_HEADS)), _pad_cols(parts["wi"], LANES)],
        axis=1).astype(BF16)
    assert w1.shape[1] == N_PROJ

    dq = MLA_NOPE + MLA_ROPE
    cols = []
    for hd in range(HEADS_MLA):
        nope = w_uq[:, dq * hd:dq * hd + MLA_NOPE]
        pe = w_uq[:, dq * hd + MLA_NOPE:dq * (hd + 1)]
        z = jnp.zeros((Q_LORA, MLA_NOPE), w_uq.dtype)
        slot = [nope, z] if hd % 2 == 0 else [z, nope]
        cols += slot + [_pad_cols(pe, LANES)]
    wuq = jnp.concatenate(cols, axis=1).astype(BF16)

    dkv = MLA_NOPE + HEAD_DIM
    kn = [w_ukv[:, dkv * hd:dkv * hd + MLA_NOPE] for hd in range(HEADS_MLA)]
    vv = [w_ukv[:, dkv * hd + MLA_NOPE:dkv * (hd + 1)] for hd in range(HEADS_MLA)]
    wukv = jnp.concatenate(kn + vv, axis=1).astype(BF16)

    nchunk = D_FF // TN_FFN
    def chunks(w):
        return w.reshape(w.shape[0], nchunk, TN_FFN).transpose(1, 0, 2)
    wg = chunks(w_up[:, :D_FF]).astype(BF16)
    wu = chunks(w_up[:, D_FF:]).astype(BF16)
    cw = jnp.concatenate([conv_w, conv_b[None, :],
                          jnp.zeros((8 - CONV_WIDTH - 1, 2 * D_FF), F32)], axis=0)
    cg = chunks(cw[:, :D_FF])
    cu = chunks(cw[:, D_FF:])
    wd = w_down.reshape(nchunk, TN_FFN, D_MODEL).astype(BF16)
    return w1, wuq, wukv, wg, wu, cg, cu, wd


def kernel(x, g_attn, w_in, g_q_lat, w_uq, g_kv_lat, w_ukv, w_o, g_ffn, w_up, conv_w, conv_b,
           w_down, g_final):
    batch, seq, _ = x.shape
    depth = w_in.shape[0]
    assert seq % DIL_BAND == 0 and seq % TM_FFN == 0 and seq % TM_OPROJ == 0 and seq % TQ == 0
    assert D_FF % TN_FFN == 0
    T = batch * seq
    tabs = _rope_tables(seq)
    bias = _dil_bias(TQ)
    xf = x.reshape(T, D_MODEL)
    for l in range(depth):
        w1, wuq, wukv, wg, wu, cg, cu, wd = _layer_weights(
            w_in[l], w_uq[l], w_ukv[l], w_up[l], conv_w[l], conv_b[l], w_down[l])
        (mq, mk, mv, dq, dk, dv, sq, sk, sv, iq, ik, iw) = _projection(
            xf, g_attn[l][None, :], w1, g_q_lat[l][None, :], wuq, g_kv_lat[l][None, :], wukv,
            tabs, seq)
        oa = _mla_attention(mq, mk, mv, batch, seq)
        ob = _dil_attention(dq, dk, dv, bias, batch, seq)
        oc = _dsa_attention(iq, iw, ik, sq, sk, sv, batch, seq)
        x1, h2 = _out_projection(xf, oa, ob, oc, w_o[l].astype(BF16), g_ffn[l][None, :])
        xf = _ffn(h2, x1, wg, wu, cg, cu, wd, seq,
                  g_final=g_final[None, :] if l == depth - 1 else None)
    return xf.reshape(batch, seq, D_MODEL)
```

```python
import functools
import math

import jax
import jax.numpy as jnp
from jax import lax
from jax.experimental import pallas as pl
from jax.experimental.pallas import tpu as pltpu

D_MODEL = 1024
HEAD_DIM = 64
HEADS_MLA = 4
HEADS_DIL = 8
HEADS_DSA = 4
Q_LORA = 256
KV_LORA = 128
MLA_NOPE = 64
MLA_ROPE = 32
DIL_PAIRS = ((128, 1), (512, 4), (2048, 16))
IDX_HEADS = 8
IDX_DIM = 32
DSA_TOPK = 256
D_FF = 2816
CONV_WIDTH = 3
ROPE_THETA = 500000.0
ROT_FRAC = 4
NORM_EPS = 1e-6

LANES = 128
BF16_SUBLANES = 16
VMEM_LIMIT = 56 * 1024 * 1024
NEG = -1e30
LOG2E = math.log2(math.e)
F32 = jnp.float32
BF16 = jnp.bfloat16

TQ = 512
TM_ROW = 512
TM_OPROJ = 1024
TM_FFN = 512
TN_FFN = 1408
MLA_UNROLL = 4
SCORE_UNROLL = 4
DIL_BAND = max(w for w, _ in DIL_PAIRS)

C_CQ = 0
C_CKV = C_CQ + Q_LORA
C_KPE = C_CKV + KV_LORA
C_DIL = C_KPE + LANES
C_DSA = C_DIL + 3 * HEADS_DIL * HEAD_DIM
C_QI = C_DSA + 3 * HEADS_DSA * HEAD_DIM
C_KI = C_QI + IDX_HEADS * IDX_DIM
C_WI = C_KI + IDX_HEADS * IDX_DIM
N_PROJ = C_WI + LANES


def _cparams(sem):
    return pltpu.CompilerParams(dimension_semantics=sem, vmem_limit_bytes=VMEM_LIMIT)


def _resident(shape, index_map):
    return pl.BlockSpec(shape, index_map, pipeline_mode=pl.Buffered(1))


def _rms(x, g):
    return x * lax.rsqrt(jnp.mean(x * x, axis=-1, keepdims=True) + NORM_EPS) * g


def _dot(a, b):
    return jnp.dot(a, b, preferred_element_type=F32)


def _dot_nt(a, b):
    return lax.dot_general(a, b, (((1,), (1,)), ((), ())), preferred_element_type=F32)


def _rope(x, tab_ref, t0, half):
    c = tab_ref[t0]
    sn = tab_ref[t0 + 1]
    sp = tab_ref[t0 + 2]
    return (x * c + pltpu.roll(x, LANES - half, axis=1) * sn
            + pltpu.roll(x, half, axis=1) * sp)


def _proj_kernel(x_ref, g_ref, w1_ref, gq_ref, wuq_ref, gkv_ref, wukv_ref, tab_ref,
                 mq_ref, mk_ref, mv_ref, dq_ref, dk_ref, dv_ref,
                 sq_ref, sk_ref, sv_ref, iq_ref, ik_ref, iw_ref):
    h = _rms(x_ref[...], g_ref[...]).astype(BF16)

    def proj(lo, width):
        return _dot(h, w1_ref[:, lo:lo + width])

    mla_scale = (MLA_NOPE + MLA_ROPE) ** -0.5 * LOG2E
    cq = _rms(proj(C_CQ, Q_LORA), gq_ref[...]).astype(BF16)
    for hd in range(HEADS_MLA):
        base = 2 * LANES * hd
        q2 = _dot(cq, wuq_ref[:, base:base + 2 * LANES])
        qp = _rope(q2[:, LANES:], tab_ref, 0, MLA_ROPE // 2)
        mq_ref[:, base:base + LANES] = (q2[:, :LANES] * mla_scale).astype(BF16)
        mq_ref[:, base + LANES:base + 2 * LANES] = (qp * mla_scale).astype(BF16)
    zkv = proj(C_CKV, KV_LORA + LANES)
    ckv = _rms(zkv[:, :KV_LORA], gkv_ref[...]).astype(BF16)
    kpe = _rope(zkv[:, KV_LORA:], tab_ref, 0, MLA_ROPE // 2).astype(BF16)
    kv = _dot(ckv, wukv_ref[...])
    for pr in range(HEADS_MLA // 2):
        mk_ref[:, 2 * LANES * pr:2 * LANES * pr + LANES] = kv[:, LANES * pr:LANES * (pr + 1)].astype(BF16)
        mk_ref[:, 2 * LANES * pr + LANES:2 * LANES * (pr + 1)] = kpe
    mv_ref[...] = kv[:, HEADS_MLA * MLA_NOPE:].astype(BF16)

    head_scale = HEAD_DIM ** -0.5 * LOG2E
    half_h = HEAD_DIM // ROT_FRAC // 2

    def roped(c0, width, t0, half, scale, o_ref):
        step = min(width, 2 * LANES)
        for c in range(0, width, step):
            z = proj(c0 + c, step)
            for off in range(0, step, LANES):
                r = _rope(z[:, off:off + LANES], tab_ref, t0, half)
                if scale is not None:
                    r = r * scale
                o_ref[:, c + off:c + off + LANES] = r.astype(BF16)

    def qkv(c0, nheads, q_ref, k_ref, v_ref):
        width = nheads * HEAD_DIM
        roped(c0, width, 3, half_h, head_scale, q_ref)
        roped(c0 + width, width, 3, half_h, None, k_ref)
        v_ref[...] = proj(c0 + 2 * width, width).astype(BF16)

    qkv(C_DIL, HEADS_DIL, dq_ref, dk_ref, dv_ref)
    qkv(C_DSA, HEADS_DSA, sq_ref, sk_ref, sv_ref)

    half_i = IDX_DIM // ROT_FRAC // 2
    roped(C_QI, IDX_HEADS * IDX_DIM, 6, half_i, None, iq_ref)
    roped(C_KI, IDX_HEADS * IDX_DIM, 6, half_i, None, ik_ref)
    iw_ref[...] = proj(C_WI, LANES) * ((IDX_HEADS * IDX_DIM) ** -0.5)


def _projection(x2, g, w1, gq, wuq, gkv, wukv, tabs, seq):
    T = x2.shape[0]
    tm = TM_ROW
    nsb = seq // tm
    row = lambda i: (i, 0)
    const = lambda i: (0, 0)
    widths = [2 * LANES * HEADS_MLA, 2 * LANES * (HEADS_MLA // 2), HEADS_MLA * HEAD_DIM,
              HEADS_DIL * HEAD_DIM, HEADS_DIL * HEAD_DIM, HEADS_DIL * HEAD_DIM,
              HEADS_DSA * HEAD_DIM, HEADS_DSA * HEAD_DIM, HEADS_DSA * HEAD_DIM,
              IDX_HEADS * IDX_DIM, IDX_HEADS * IDX_DIM]
    out_shape = [jax.ShapeDtypeStruct((T, w), BF16) for w in widths]
    out_shape.append(jax.ShapeDtypeStruct((T, LANES), F32))
    out_specs = [pl.BlockSpec((tm, w), row) for w in widths] + [pl.BlockSpec((tm, LANES), row)]
    return pl.pallas_call(
        _proj_kernel,
        grid=(T // tm,),
        in_specs=[
            pl.BlockSpec((tm, D_MODEL), row),
            pl.BlockSpec((1, D_MODEL), const),
            pl.BlockSpec(w1.shape, const),
            pl.BlockSpec((1, Q_LORA), const),
            pl.BlockSpec(wuq.shape, const),
            pl.BlockSpec((1, KV_LORA), const),
            pl.BlockSpec(wukv.shape, const),
            pl.BlockSpec((9, tm, LANES), lambda i: (0, i % nsb, 0)),
        ],
        out_specs=out_specs,
        out_shape=out_shape,
        compiler_params=_cparams(("parallel",)),
        name="proj",
    )(x2, g, w1, gq, wuq, gkv, wukv, tabs)


def _rep(x, n):
    return x if n == 1 else jnp.concatenate([x] * n, axis=1)


def _flash_tile(q_stack, k, v, nheads, mask_fn, m_ref, l_ref, acc_ref):
    tq = q_stack.shape[0] // nheads
    tk = k.shape[0]
    dv = v.shape[1]
    s_all = _dot_nt(q_stack, k)
    ps = []
    alphas = []
    for h in range(nheads):
        s = mask_fn(s_all[h * tq:(h + 1) * tq])
        m_prev = m_ref[h]
        m_new = jnp.maximum(m_prev, jnp.max(s, axis=1, keepdims=True))
        alpha = jnp.exp2(m_prev - m_new)
        p = jnp.exp2(s - _rep(m_new, tk // LANES))
        l_ref[h] = alpha * l_ref[h] + jnp.sum(p, axis=1, keepdims=True)
        m_ref[h] = m_new
        ps.append(p.astype(BF16))
        alphas.append(alpha)
    pv = _dot(jnp.concatenate(ps, axis=0), v)
    for h in range(nheads):
        acc_ref[h] = _rep(alphas[h], dv // LANES) * acc_ref[h] + pv[h * tq:(h + 1) * tq]


def _flash_init(m_ref, l_ref, acc_ref):
    m_ref[...] = jnp.full(m_ref.shape, NEG, F32)
    l_ref[...] = jnp.zeros(l_ref.shape, F32)
    acc_ref[...] = jnp.zeros(acc_ref.shape, F32)


def _flash_out(nheads, l_ref, acc_ref):
    tq, dv = acc_ref.shape[1:]
    lane = lax.broadcasted_iota(jnp.int32, (tq, dv), 1)
    out = jnp.zeros((tq, dv), F32)
    for h in range(nheads):
        out = jnp.where((lane // HEAD_DIM) == h, acc_ref[h] / _rep(l_ref[h], dv // LANES), out)
    return out


def _slab(x):
    return jnp.broadcast_to(x, (8, x.shape[1]))


def _to_rows(x, width):
    tq = x.shape[1]
    return _rep(jnp.transpose(jnp.broadcast_to(x[:1], (LANES, tq))), width // LANES)


def _flash_tile_t(q_stack, k, v, nheads, mask_fn, m_ref, l_ref, acc_ref):
    tq = q_stack.shape[0] // nheads
    dv = v.shape[1]
    s_all = _dot_nt(k, q_stack)
    ps = []
    alphas = []
    half = tq // 2
    for h in range(nheads):
        for c in range(2):
            lanes = slice(c * half, (c + 1) * half)
            s = mask_fn(s_all[:, h * tq + c * half:h * tq + (c + 1) * half], lanes)
            m_prev = m_ref[h, :, lanes]
            m_new = jnp.maximum(m_prev, _slab(jnp.max(s, axis=0, keepdims=True)))
            alpha = jnp.exp2(m_prev - m_new)
            p = jnp.exp2(s - m_new[:1])
            l_ref[h, :, lanes] = alpha * l_ref[h, :, lanes] + _slab(jnp.sum(p, axis=0, keepdims=True))
            m_ref[h, :, lanes] = m_new
            ps.append(p.astype(BF16))
            alphas.append(alpha)
    alphas = [jnp.concatenate(alphas[2 * h:2 * h + 2], axis=1) for h in range(nheads)]
    pv = lax.dot_general(jnp.concatenate(ps, axis=1), v, (((0,), (0,)), ((), ())),
                         preferred_element_type=F32)
    for h in range(nheads):
        acc_ref[h] = _to_rows(alphas[h], dv) * acc_ref[h] + pv[h * tq:(h + 1) * tq]


def _flash_out_t(nheads, l_ref, acc_ref):
    tq, dv = acc_ref.shape[1:]
    lane = lax.broadcasted_iota(jnp.int32, (tq, dv), 1)
    out = jnp.zeros((tq, dv), F32)
    for h in range(nheads):
        out = jnp.where((lane // HEAD_DIM) == h, acc_ref[h] / _to_rows(l_ref[h], dv), out)
    return out


def _head_stack(q, nheads):
    lane = lax.broadcasted_iota(jnp.int32, q.shape, 1)
    zero = jnp.zeros_like(q)
    return jnp.concatenate([jnp.where((lane // HEAD_DIM) == h, q, zero) for h in range(nheads)], axis=0)


def _causal_tile(tq):
    r = lax.broadcasted_iota(jnp.int32, (tq, tq), 0)
    c = lax.broadcasted_iota(jnp.int32, (tq, tq), 1)
    return r >= c


def _mla_kernel(q_ref, k_ref, v_ref, o_ref, m_ref, l_ref, acc_ref):
    tq = q_ref.shape[0]
    i = pl.program_id(2)
    _flash_init(m_ref, l_ref, acc_ref)
    q_stack = jnp.concatenate([q_ref[:, :2 * LANES], q_ref[:, 2 * LANES:]], axis=0)

    def step(j, mask_fn):
        start = pl.multiple_of(j * tq, tq)
        _flash_tile(q_stack, k_ref[pl.ds(start, tq), :], v_ref[pl.ds(start, tq), :], 2,
                    mask_fn, m_ref, l_ref, acc_ref)

    def group(g, carry):
        for u in range(MLA_UNROLL):
            step(g * MLA_UNROLL + u, lambda s: s)
        return carry

    def single(j, carry):
        step(j, lambda s: s)
        return carry

    ngroup = i // MLA_UNROLL
    lax.fori_loop(0, ngroup, group, 0)
    lax.fori_loop(ngroup * MLA_UNROLL, i, single, 0)
    step(i, lambda s: jnp.where(_causal_tile(tq), s, NEG))
    o_ref[...] = _flash_out(2, l_ref, acc_ref).astype(o_ref.dtype)


def _mla_attention(mq, mk, mv, batch, seq):
    T = mq.shape[0]
    tq = TQ
    nq = seq // tq
    npair = HEADS_MLA // 2
    return pl.pallas_call(
        _mla_kernel,
        grid=(batch, npair, nq),
        in_specs=[
            pl.BlockSpec((tq, 4 * LANES), lambda b, p, i: (b * nq + i, p)),
            _resident((seq, 2 * LANES), lambda b, p, i: (b, p)),
            _resident((seq, LANES), lambda b, p, i: (b, p)),
        ],
        out_specs=pl.BlockSpec((tq, LANES), lambda b, p, i: (b * nq + i, p)),
        out_shape=jax.ShapeDtypeStruct((T, HEADS_MLA * HEAD_DIM), BF16),
        scratch_shapes=[pltpu.VMEM((2, tq, LANES), F32)] * 3,
        compiler_params=_cparams(("parallel", "parallel", "arbitrary")),
        name="mla_attn",
    )(mq, mk, mv)


def _dil_bias(tq):
    nband = DIL_BAND // tq + 1
    d = (lax.broadcasted_iota(jnp.int32, (nband, tq, tq), 0) * tq
         + lax.broadcasted_iota(jnp.int32, (nband, tq, tq), 1)
         - lax.broadcasted_iota(jnp.int32, (nband, tq, tq), 2))
    cnt = jnp.zeros(d.shape, F32)
    for window, dil in DIL_PAIRS:
        ok = (d >= 0) & (d % dil == 0) & (d <= (window // dil) * dil)
        cnt = cnt + ok.astype(F32)
    return jnp.where(cnt > 0, jnp.log2(jnp.maximum(cnt, 1.0)), NEG)


def _dil_kernel(q_ref, k_ref, v_ref, bias_ref, o_ref, m_ref, l_ref, acc_ref):
    tq = q_ref.shape[0]
    nband = bias_ref.shape[0]
    i = pl.program_id(2)
    _flash_init(m_ref, l_ref, acc_ref)
    q_stack = _head_stack(q_ref[...], 2)

    def step(delta):
        start = pl.multiple_of((i - delta) * tq, tq)
        bias = bias_ref[delta]
        _flash_tile(q_stack, k_ref[pl.ds(start, tq), :], v_ref[pl.ds(start, tq), :], 2,
                    lambda s: s + bias, m_ref, l_ref, acc_ref)

    @pl.when(i >= nband - 1)
    def _():
        for delta in range(nband):
            step(delta)

    @pl.when(i < nband - 1)
    def _():
        step(0)

        def body(delta, carry):
            step(delta)
            return carry

        lax.fori_loop(1, i + 1, body, 0)

    o_ref[...] = _flash_out(2, l_ref, acc_ref).astype(o_ref.dtype)


def _dil_attention(dq, dk, dv, bias, batch, seq):
    T = dq.shape[0]
    tq = TQ
    nq = seq // tq
    npair = HEADS_DIL // 2
    return pl.pallas_call(
        _dil_kernel,
        grid=(batch, npair, nq),
        in_specs=[
            pl.BlockSpec((tq, LANES), lambda b, p, i: (b * nq + i, p)),
            _resident((seq, LANES), lambda b, p, i: (b, p)),
            _resident((seq, LANES), lambda b, p, i: (b, p)),
            _resident(bias.shape, lambda b, p, i: (0, 0, 0)),
        ],
        out_specs=pl.BlockSpec((tq, LANES), lambda b, p, i: (b * nq + i, p)),
        out_shape=jax.ShapeDtypeStruct((T, HEADS_DIL * HEAD_DIM), BF16),
        scratch_shapes=[pltpu.VMEM((2, tq, LANES), F32)] * 3,
        compiler_params=_cparams(("parallel", "parallel", "arbitrary")),
        name="dil_attn",
    )(dq, dk, dv, bias)


_INT_MIN = -(2 ** 31)
_KEY_NEG_INF = _INT_MIN + 0x7FFFFF
INTERP_MIN_WIDTH = 4096
INTERP_MIN_INDEX = 32


def _ordered_bits(x):
    return x ^ ((x >> 31) & 0x7FFFFFFF)


def _key_to_float(key):
    return lax.bitcast_convert_type(_ordered_bits(key), F32)


def _float_to_key(x):
    return _ordered_bits(lax.bitcast_convert_type(x, jnp.int32))


def _dsa_kernel(iq_ref, iwt_ref, ik_ref, q_ref, k_ref, v_ref, o_ref,
                sc_ref, cm_ref, m_ref, l_ref, acc_ref, *, ksel):
    tq = q_ref.shape[0]
    i = pl.program_id(1)
    ntile = i + 1
    key_row = lax.broadcasted_iota(jnp.int32, (tq, tq), 0)
    causal = key_row <= lax.broadcasted_iota(jnp.int32, (tq, tq), 1)
    neg_inf = jnp.float32(-jnp.inf)
    as_f32 = lambda v: lax.bitcast_convert_type(v, F32)
    as_i32 = lambda v: lax.bitcast_convert_type(v, jnp.int32)

    iq_stack = jnp.concatenate(
        [jnp.where((lax.broadcasted_iota(jnp.int32, iq_ref.shape, 1) // IDX_DIM) == h,
                   iq_ref[...], jnp.zeros(iq_ref.shape, BF16)) for h in range(IDX_HEADS)], axis=0)
    wt = iwt_ref[...]
    cm_ref[...] = jnp.full(cm_ref.shape, neg_inf, F32)

    def score_tile(j, diag):
        kk = ik_ref[pl.ds(pl.multiple_of(j * tq, tq), tq), :]
        sc = jnp.zeros((tq, tq), F32)
        for half in range(2):
            nh = IDX_HEADS // 2
            d = _dot_nt(kk, iq_stack[half * nh * tq:(half + 1) * nh * tq])
            for hh in range(nh):
                h = half * nh + hh
                sc = sc + jnp.maximum(d[:, hh * tq:(hh + 1) * tq], 0.0) * wt[h:h + 1, :]
        if diag:
            sc = jnp.where(causal, sc, neg_inf)
        sc_ref[j] = as_i32(sc)
        cm_ref[...] = jnp.maximum(cm_ref[...], sc)

    def score_group(g, carry):
        for u in range(SCORE_UNROLL):
            score_tile(g * SCORE_UNROLL + u, False)
        return carry

    def score_body(j, carry):
        score_tile(j, False)
        return carry

    ngroup = i // SCORE_UNROLL
    lax.fori_loop(0, ngroup, score_group, 0)
    lax.fori_loop(ngroup * SCORE_UNROLL, i, score_body, 0)
    score_tile(i, True)

    def count_cols(hit, thr):
        def body(j, cnt):
            words = sc_ref[j]
            for g in range(tq // 8):
                cnt = cnt + jnp.where(hit(words[8 * g:8 * (g + 1)], thr), 1, 0)
            return cnt

        cnt = lax.fori_loop(0, ntile, body, jnp.zeros((8, tq), jnp.int32))
        return _slab(jnp.sum(cnt, axis=0, keepdims=True))

    def count_ge(thr):
        return count_cols(lambda words, t: as_f32(words) >= t, thr)

    cm = cm_ref[...]
    lo = jnp.maximum(_float_to_key(_slab(jnp.min(cm, axis=0, keepdims=True))), _KEY_NEG_INF)
    hi = _float_to_key(_slab(jnp.max(cm, axis=0, keepdims=True))) + 1
    clo = jnp.full((8, tq), 2 ** 30, jnp.int32)
    chi = jnp.zeros((8, tq), jnp.int32)
    key_zero = 0
    key_tiny = 0x00800000
    hi = jnp.where((hi > key_zero) & (hi < key_tiny), key_tiny, hi)

    def open_rows(lo, hi, clo):
        zero_class = (lo == key_zero) & (hi == key_tiny)
        return (clo != ksel) & (hi - 1 > lo) & jnp.logical_not(zero_class)

    def any_rows(mask):
        return jnp.max(jnp.where(mask, 1, 0))

    def probe_key(lo, hi, clo, chi):
        mid = (lo >> 1) + (hi >> 1) + (lo & hi & 1)
        straddle = (lo < key_zero) & (hi > key_zero)
        width = hi - lo
        lo_f = _key_to_float(lo)
        frac = (clo - ksel).astype(F32) / (clo - chi).astype(F32)
        guess = _float_to_key(lo_f + (_key_to_float(hi) - lo_f) * frac)
        margin = width >> 3
        guess = jnp.minimum(jnp.maximum(guess, lo + 1 + margin), hi - 1 - margin)
        use = ((clo < 2 ** 30) & (width > INTERP_MIN_WIDTH) & (lo > _KEY_NEG_INF)
               & jnp.logical_not(straddle))
        probe = jnp.where(use, guess, mid)
        probe = jnp.where(straddle, key_zero, probe)
        return jnp.where((lo == key_zero) & (hi > key_tiny), key_tiny, probe)

    def sel_body(st):
        _, lo, hi, clo, chi = st
        mid = probe_key(lo, hi, clo, chi)
        cnt = count_ge(_key_to_float(mid))
        ge = cnt >= ksel
        act = open_rows(lo, hi, clo)
        up = act & ge
        dn = act & jnp.logical_not(ge)
        lo = jnp.where(up, mid, lo)
        clo = jnp.where(up, cnt, clo)
        hi = jnp.where(dn, mid, hi)
        chi = jnp.where(dn, cnt, chi)
        return any_rows(open_rows(lo, hi, clo)), lo, hi, clo, chi

    _, lo, hi, clo, chi = lax.while_loop(
        lambda st: st[0] > 0, sel_body, (any_rows(open_rows(lo, hi, clo)), lo, hi, clo, chi))
    exact = clo == ksel
    upper = _key_to_float(jnp.where(exact, lo, hi))[:1]
    lower = _key_to_float(lo)[:1]

    big = jnp.int32(2 ** 30)

    def rank_body(j, carry):
        sc = as_f32(sc_ref[j])
        sc_ref[j] = jnp.where(sc >= upper, -1, jnp.where(sc >= lower, key_row + j * tq, big))
        return carry

    lax.fori_loop(0, ntile, rank_body, 0)

    ja = jnp.where(exact, -1, 0)
    jb = jnp.where(exact, 0, ntile * tq)
    ca = jnp.where(exact, ksel, chi)
    cb = clo

    def idx_body(st):
        _, ja, jb, ca, cb = st
        width = jb - ja
        frac = (ksel - ca).astype(F32) / (cb - ca).astype(F32)
        guess = ja + (width.astype(F32) * frac).astype(jnp.int32)
        margin = width >> 3
        guess = jnp.minimum(jnp.maximum(guess, ja + 1 + margin), jb - 1 - margin)
        mid = jnp.where((cb < 2 ** 30) & (width > INTERP_MIN_INDEX), guess, (ja + jb) >> 1)
        cnt = count_cols(lambda words, t: words < t, mid)
        act = width > 1
        hit = act & (cnt == ksel)
        dn = act & (cnt >= ksel)
        up = act & (cnt < ksel)
        jb = jnp.where(dn, mid, jb)
        cb = jnp.where(dn, cnt, cb)
        ja = jnp.where(hit, mid - 1, jnp.where(up, mid, ja))
        ca = jnp.where(up, cnt, ca)
        return any_rows(jb - ja > 1), ja, jb, ca, cb

    _, _, jb, _, _ = lax.while_loop(lambda st: st[0] > 0, idx_body,
                                   (any_rows(jb - ja > 1), ja, jb, ca, cb))
    bound = jb[:1]

    _flash_init(m_ref, l_ref, acc_ref)
    q_stack = _head_stack(q_ref[...], HEADS_DSA)

    def attn_tile(j, diag):
        start = pl.multiple_of(j * tq, tq)
        sel = sc_ref[j] < bound
        if diag:
            sel = sel & causal
        cap = jnp.where(sel, -NEG, NEG)
        _flash_tile_t(q_stack, k_ref[pl.ds(start, tq), :], v_ref[pl.ds(start, tq), :], HEADS_DSA,
                      lambda s, lanes: jnp.minimum(s, cap[:, lanes]), m_ref, l_ref, acc_ref)

    def attn_body(j, carry):
        attn_tile(j, False)
        return carry

    lax.fori_loop(0, i, attn_body, 0)
    attn_tile(i, True)
    o_ref[...] = _flash_out_t(HEADS_DSA, l_ref, acc_ref).astype(o_ref.dtype)


def _dsa_attention(iq, iw, ik, sq, sk, sv, batch, seq):
    T = sq.shape[0]
    tq = TQ
    nq = seq // tq
    ksel = min(DSA_TOPK, seq // 4)
    assert tq >= ksel
    width = HEADS_DSA * HEAD_DIM
    qrow = lambda b, i: (b * nq + i, 0)
    full = lambda b, i: (b, 0)
    return pl.pallas_call(
        functools.partial(_dsa_kernel, ksel=ksel),
        grid=(batch, nq),
        in_specs=[
            pl.BlockSpec((tq, IDX_HEADS * IDX_DIM), qrow),
            pl.BlockSpec((IDX_HEADS, tq), lambda b, i: (0, b * nq + i)),
            _resident((seq, IDX_HEADS * IDX_DIM), full),
            pl.BlockSpec((tq, width), qrow),
            _resident((seq, width), full),
            _resident((seq, width), full),
        ],
        out_specs=pl.BlockSpec((tq, width), qrow),
        out_shape=jax.ShapeDtypeStruct((T, width), BF16),
        scratch_shapes=[
            pltpu.VMEM((nq, tq, tq), jnp.int32),
            pltpu.VMEM((tq, tq), F32),
            pltpu.VMEM((HEADS_DSA, 8, tq), F32),
            pltpu.VMEM((HEADS_DSA, 8, tq), F32),
            pltpu.VMEM((HEADS_DSA, tq, width), F32),
        ],
        compiler_params=_cparams(("parallel", "arbitrary")),
        name="dsa_attn",
    )(iq, jnp.transpose(iw[:, :IDX_HEADS]), ik, sq, sk, sv)


def _oproj_kernel(x_ref, oa_ref, ob_ref, oc_ref, wo_ref, g_ref, x1_ref, h_ref):
    na = HEADS_MLA * HEAD_DIM
    nb = HEADS_DIL * HEAD_DIM
    x1 = (x_ref[...] + _dot(oa_ref[...], wo_ref[:na, :])
          + _dot(ob_ref[...], wo_ref[na:na + nb, :])
          + _dot(oc_ref[...], wo_ref[na + nb:, :]))
    x1_ref[...] = x1
    h_ref[...] = _rms(x1, g_ref[...]).astype(BF16)


def _out_projection(x2, oa, ob, oc, wo, g):
    T = x2.shape[0]
    tm = TM_OPROJ
    row = lambda i: (i, 0)
    const = lambda i: (0, 0)
    return pl.pallas_call(
        _oproj_kernel,
        grid=(T // tm,),
        in_specs=[
            pl.BlockSpec((tm, D_MODEL), row),
            pl.BlockSpec((tm, oa.shape[1]), row),
            pl.BlockSpec((tm, ob.shape[1]), row),
            pl.BlockSpec((tm, oc.shape[1]), row),
            pl.BlockSpec(wo.shape, const),
            pl.BlockSpec((1, D_MODEL), const),
        ],
        out_specs=[pl.BlockSpec((tm, D_MODEL), row), pl.BlockSpec((tm, D_MODEL), row)],
        out_shape=[jax.ShapeDtypeStruct((T, D_MODEL), F32), jax.ShapeDtypeStruct((T, D_MODEL), BF16)],
        compiler_params=_cparams(("parallel",)),
        name="oproj",
    )(x2, oa, ob, oc, wo, g)


def _ffn_kernel(h_ref, halo_ref, x1_ref, wg_ref, wu_ref, cg_ref, cu_ref, wd_ref, *rest,
                blocks_per_seq, final_norm):
    gf_ref = rest[0] if final_norm else None
    o_ref, uga_ref, uua_ref, ugb_ref, uub_ref, acc_ref = rest[1:] if final_norm else rest
    tm = h_ref.shape[0]
    hal = halo_ref.shape[0]
    nchunk = wg_ref.shape[0]
    first = (pl.program_id(0) % blocks_per_seq) == 0
    keep = jnp.where(first, 0.0, 1.0).astype(F32)
    h = h_ref[...]
    hh = halo_ref[...]
    acc_ref[...] = jnp.zeros(acc_ref.shape, F32)

    def conv(u_ref, w):
        y = w[CONV_WIDTH:CONV_WIDTH + 1, :]
        for tap in range(CONV_WIDTH):
            off = hal - (CONV_WIDTH - 1) + tap
            y = y + u_ref[pl.ds(off, tm), :] * w[tap:tap + 1, :]
        return y

    def up_proj(j, slot):
        ug_ref, uu_ref = slot
        wg = wg_ref[j]
        wu = wu_ref[j]
        ug_ref[:hal, :] = _dot(hh, wg) * keep
        ug_ref[hal:, :] = _dot(h, wg)
        uu_ref[:hal, :] = _dot(hh, wu) * keep
        uu_ref[hal:, :] = _dot(h, wu)

    def down_proj(j, slot):
        ug_ref, uu_ref = slot
        gate = conv(ug_ref, cg_ref[j])
        up = conv(uu_ref, cu_ref[j])
        act = gate * jax.nn.sigmoid(gate) * up
        acc_ref[...] += _dot(act.astype(BF16), wd_ref[j])

    slot_a = (uga_ref, uua_ref)
    slot_b = (ugb_ref, uub_ref)
    up_proj(0, slot_a)

    def chunk_pair(jj, carry):
        j = 2 * jj
        up_proj(j + 1, slot_b)
        down_proj(j, slot_a)
        up_proj(j + 2, slot_a)
        down_proj(j + 1, slot_b)
        return carry

    lax.fori_loop(0, (nchunk - 1) // 2, chunk_pair, 0)
    if nchunk % 2 == 0:
        up_proj(nchunk - 1, slot_b)
        down_proj(nchunk - 2, slot_a)
        down_proj(nchunk - 1, slot_b)
    else:
        down_proj(nchunk - 1, slot_a)
    out = x1_ref[...] + acc_ref[...]
    o_ref[...] = out if gf_ref is None else _rms(out, gf_ref[...])


def _ffn(h2, x1, wg, wu, cg, cu, wd, seq, g_final=None):
    T = x1.shape[0]
    tm = TM_FFN
    hal = BF16_SUBLANES
    tn = wg.shape[2]
    row = lambda i: (i, 0)
    c3 = lambda i: (0, 0, 0)
    final_norm = g_final is not None
    in_specs = [
        pl.BlockSpec((tm, D_MODEL), row),
        pl.BlockSpec((hal, D_MODEL), lambda i: (jnp.maximum(i * (tm // hal) - 1, 0), 0)),
        pl.BlockSpec((tm, D_MODEL), row),
        _resident(wg.shape, c3),
        _resident(wu.shape, c3),
        _resident(cg.shape, c3),
        _resident(cu.shape, c3),
        _resident(wd.shape, c3),
    ]
    args = [h2, h2, x1, wg, wu, cg, cu, wd]
    if final_norm:
        in_specs.append(pl.BlockSpec((1, D_MODEL), lambda i: (0, 0)))
        args.append(g_final)
    return pl.pallas_call(
        functools.partial(_ffn_kernel, blocks_per_seq=seq // tm, final_norm=final_norm),
        grid=(T // tm,),
        in_specs=in_specs,
        out_specs=pl.BlockSpec((tm, D_MODEL), row),
        out_shape=jax.ShapeDtypeStruct((T, D_MODEL), F32),
        scratch_shapes=[pltpu.VMEM((hal + tm, tn), F32)] * 4 + [pltpu.VMEM((tm, D_MODEL), F32)],
        compiler_params=_cparams(("parallel",)),
        name="ffn",
    )(*args)


def _rope_tables(seq):
    def base(dim):
        inv = jnp.power(jnp.float32(ROPE_THETA), -jnp.arange(0, dim, 2, dtype=F32) / dim)
        ang = jnp.arange(seq, dtype=F32)[:, None] * inv[None, :]
        return jnp.cos(ang), jnp.sin(ang)

    def tables(dim, period):
        cos, sin = base(dim)
        half = dim // 2
        pad = period - dim
        c = jnp.concatenate([cos, cos, jnp.ones((seq, pad), F32)], axis=1)
        sn = jnp.concatenate([-sin, jnp.zeros((seq, half + pad), F32)], axis=1)
        sp = jnp.concatenate([jnp.zeros((seq, half), F32), sin, jnp.zeros((seq, pad), F32)], axis=1)
        reps = LANES // period
        return [jnp.tile(t, (1, reps)) for t in (c, sn, sp)]

    tabs = (tables(MLA_ROPE, LANES) + tables(HEAD_DIM // ROT_FRAC, HEAD_DIM)
            + tables(IDX_DIM // ROT_FRAC, IDX_DIM))
    return jnp.stack(tabs)


def _pad_cols(w, width):
    return jnp.pad(w, ((0, 0), (0, width - w.shape[1])))


def _layer_weights(w_in, w_uq, w_ukv, w_up, conv_w, conv_b, w_down):
    o = 0
    parts = {}
    for name, width in (("cq", Q_LORA), ("ckv", KV_LORA), ("kpe", MLA_ROPE),
                        ("dil", 3 * HEADS_DIL * HEAD_DIM), ("dsa", 3 * HEADS_DSA * HEAD_DIM),
                        ("qi", IDX_HEADS * IDX_DIM), ("ki", IDX_DIM), ("wi", IDX_HEADS)):
        parts[name] = w_in[:, o:o + width]
        o += width
    w1 = jnp.concatenate([
        parts["cq"], parts["ckv"], _pad_cols(parts["kpe"], LANES), parts["dil"], parts["dsa"],
        parts["qi"], jnp.tile(parts["ki"], (1, IDX_HEADS)), _pad_cols(parts["wi"], LANES)],
        axis=1).astype(BF16)
    assert w1.shape[1] == N_PROJ

    dq = MLA_NOPE + MLA_ROPE
    cols = []
    for hd in range(HEADS_MLA):
        nope = w_uq[:, dq * hd:dq * hd + MLA_NOPE]
        pe = w_uq[:, dq * hd + MLA_NOPE:dq * (hd + 1)]
        z = jnp.zeros((Q_LORA, MLA_NOPE), w_uq.dtype)
        slot = [nope, z] if hd % 2 == 0 else [z, nope]
        cols += slot + [_pad_cols(pe, LANES)]
    wuq = jnp.concatenate(cols, axis=1).astype(BF16)

    dkv = MLA_NOPE + HEAD_DIM
    kn = [w_ukv[:, dkv * hd:dkv * hd + MLA_NOPE] for hd in range(HEADS_MLA)]
    vv = [w_ukv[:, dkv * hd + MLA_NOPE:dkv * (hd + 1)] for hd in range(HEADS_MLA)]
    wukv = jnp.concatenate(kn + vv, axis=1).astype(BF16)

    nchunk = D_FF // TN_FFN
    def chunks(w):
        return w.reshape(w.shape[0], nchunk, TN_FFN).transpose(1, 0, 2)
    wg = chunks(w_up[:, :D_FF]).astype(BF16)
    wu = chunks(w_up[:, D_FF:]).astype(BF16)
    cw = jnp.concatenate([conv_w, conv_b[None, :],
                          jnp.zeros((8 - CONV_WIDTH - 1, 2 * D_FF), F32)], axis=0)
    cg = chunks(cw[:, :D_FF])
    cu = chunks(cw[:, D_FF:])
    wd = w_down.reshape(nchunk, TN_FFN, D_MODEL).astype(BF16)
    return w1, wuq, wukv, wg, wu, cg, cu, wd


def kernel(x, g_attn, w_in, g_q_lat, w_uq, g_kv_lat, w_ukv, w_o, g_ffn, w_up, conv_w, conv_b,
           w_down, g_final):
    batch, seq, _ = x.shape
    depth = w_in.shape[0]
    assert seq % DIL_BAND == 0 and seq % TM_FFN == 0 and seq % TM_OPROJ == 0 and seq % TQ == 0
    assert D_FF % TN_FFN == 0
    T = batch * seq
    tabs = _rope_tables(seq)
    bias = _dil_bias(TQ)
    xf = x.reshape(T, D_MODEL)
    for l in range(depth):
        w1, wuq, wukv, wg, wu, cg, cu, wd = _layer_weights(
            w_in[l], w_uq[l], w_ukv[l], w_up[l], conv_w[l], conv_b[l], w_down[l])
        (mq, mk, mv, dq, dk, dv, sq, sk, sv, iq, ik, iw) = _projection(
            xf, g_attn[l][None, :], w1, g_q_lat[l][None, :], wuq, g_kv_lat[l][None, :], wukv,
            tabs, seq)
        oa = _mla_attention(mq, mk, mv, batch, seq)
        ob = _dil_attention(dq, dk, dv, bias, batch, seq)
        oc = _dsa_attention(iq, iw, ik, sq, sk, sv, batch, seq)
        x1, h2 = _out_projection(xf, oa, ob, oc, w_o[l].astype(BF16), g_ffn[l][None, :])
        xf = _ffn(h2, x1, wg, wu, cg, cu, wd, seq,
                  g_final=g_final[None, :] if l == depth - 1 else None)
    return xf.reshape(batch, seq, D_MODEL)
```
